```python
import jax, jax.numpy as jnp
from jax import lax
import numpy as np

D_MODEL = 1024
BATCH = 32
SEQ = 256
DEPTH = 4
DEC_BATCH = 2
DEC_SEQ = 2048
PAST_LEN = 256

GRID_W = 64
N_MIXERS = 3
N_FNET_LAYERS = (DEPTH + 2) // N_MIXERS
N_POOL_LAYERS = (DEPTH + 1) // N_MIXERS
N_MLA_LAYERS = DEPTH // N_MIXERS
N_FOURIER_GROUPS = 4
FOURIER_GROUP = D_MODEL // N_FOURIER_GROUPS
POOL_WINDOWS = (2, 4, 8, 16)
N_POOL_GROUPS = len(POOL_WINDOWS)
POOL_GROUP = D_MODEL // N_POOL_GROUPS
N_HEADS = 8
QK_NOPE = 128
QK_ROPE = 64
V_HEAD = 128
Q_RANK = 768
KV_RANK = 512
ROPE_AXIS = QK_ROPE // 2
ROPE_BASE = 10000.0
Q_BLOCK = 128
SM_SCALE = (QK_NOPE + QK_ROPE) ** -0.5
D_FF = 2816
CONV_W = 3
N_MOD = 6
EPS = 1e-6

kernel_name = 'hybrid_fnet_pool_mla_diffusion_step'


def rms_norm(x):
    xf = x.astype(jnp.float32)
    y = xf * lax.rsqrt(jnp.mean(xf * xf, axis=-1, keepdims=True) + EPS)
    return y.astype(x.dtype)


def rms_norm_gain(x, g):
    return rms_norm(x) * g


def ada_mod(cond, w, b):
    mod = jax.nn.silu(cond) @ w + b
    return tuple(m[:, None, :] for m in jnp.split(mod, N_MOD, axis=-1))


def modulate(x, shift, scale):
    return rms_norm(x) * (1 + scale) + shift


def fourier_mix(h, w_out):
    B, T, D = h.shape
    hg = h.astype(jnp.float32).reshape(B, T, N_FOURIER_GROUPS, FOURIER_GROUP)
    f = jnp.fft.fftn(hg, axes=(1, 3), norm='ortho').real
    return f.astype(h.dtype).reshape(B, T, D) @ w_out


def pool_mix(h, w_pool, scale):
    B, T, D = h.shape
    hg = h.astype(jnp.float32).reshape(B, T, N_POOL_GROUPS, POOL_GROUP)
    cs = jnp.concatenate([jnp.zeros((B, 1, N_POOL_GROUPS, POOL_GROUP), jnp.float32),
                          jnp.cumsum(hg, axis=1)], axis=1)
    t = jnp.arange(T)
    outs = []
    for g, w in enumerate(POOL_WINDOWS):
        lo = jnp.clip(t - w // 2, 0, T)
        hi = jnp.clip(t - w // 2 + w, 0, T)
        s = cs[:, hi, g] - cs[:, lo, g]
        cnt = (hi - lo).astype(jnp.float32)[None, :, None]
        outs.append(s / cnt - hg[:, :, g])
    pooled = jnp.stack(outs, axis=2).astype(h.dtype)
    y = jnp.einsum('btgc,gcd->btgd', pooled, w_pool).reshape(B, T, D)
    return y * scale


def dwconv3(u, w, b):
    up = jnp.pad(u, ((0, 0), (1, 1), (0, 0)))
    return up[:, :-2] * w[0] + up[:, 1:-1] * w[1] + up[:, 2:] * w[2] + b


def conv_ffn(h, w_in, w_conv, b_conv, w_out):
    u = dwconv3(h @ w_in, w_conv, b_conv)
    gate, val = jnp.split(u, 2, axis=-1)
    return (jax.nn.silu(gate) * val) @ w_out


def axial_rope_tables(n_tok):
    rows = n_tok // GRID_W
    row = jnp.repeat(jnp.arange(rows), GRID_W).astype(jnp.float32)
    col = jnp.tile(jnp.arange(GRID_W), rows).astype(jnp.float32)
    inv = 1.0 / (ROPE_BASE ** (jnp.arange(0, ROPE_AXIS, 2, dtype=jnp.float32) / ROPE_AXIS))
    ang_r = (row[:, None] * inv)[:, None, :]
    ang_c = (col[:, None] * inv)[:, None, :]
    return (jnp.cos(ang_r), jnp.sin(ang_r), jnp.cos(ang_c), jnp.sin(ang_c))


def rotate_half(x, cos, sin):
    x1, x2 = jnp.split(x, 2, axis=-1)
    cos = cos.astype(x.dtype)
    sin = sin.astype(x.dtype)
    return jnp.concatenate([x1 * cos - x2 * sin, x2 * cos + x1 * sin], axis=-1)


def apply_axial_rope(x, tabs):
    cos_r, sin_r, cos_c, sin_c = tabs
    xr, xc = jnp.split(x, 2, axis=-1)
    return jnp.concatenate([rotate_half(xr, cos_r, sin_r), rotate_half(xc, cos_c, sin_c)], axis=-1)


def mla_project(h, w_dq, g_q, w_uq, w_dkv, g_kv):
    B, T, _ = h.shape
    cq = rms_norm_gain(h @ w_dq, g_q)
    q = (cq @ w_uq).reshape(B, T, N_HEADS, QK_NOPE + QK_ROPE)
    ckv = h @ w_dkv
    c_kv = rms_norm_gain(ckv[..., :KV_RANK], g_kv)
    k_pe = ckv[..., KV_RANK:]
    return q[..., :QK_NOPE], q[..., QK_NOPE:], c_kv, k_pe


def mla_expand(c_kv, w_ukv):
    B, T, _ = c_kv.shape
    kv = (c_kv @ w_ukv).reshape(B, T, N_HEADS, QK_NOPE + V_HEAD)
    return kv[..., :QK_NOPE], kv[..., QK_NOPE:]


def mla_attend(q_nope, q_pe, k_nope, k_pe, v):
    B, Tq = q_nope.shape[:2]
    nb = Tq // Q_BLOCK
    qn = q_nope.reshape(B, nb, Q_BLOCK, N_HEADS, QK_NOPE).transpose(1, 0, 2, 3, 4)
    qp = q_pe.reshape(B, nb, Q_BLOCK, N_HEADS, QK_ROPE).transpose(1, 0, 2, 3, 4)

    def block(args):
        qn_b, qp_b = args
        s = (jnp.einsum('bqhd,bkhd->bhqk', qn_b, k_nope)
             + jnp.einsum('bqhr,bkr->bhqk', qp_b, k_pe))
        p = jax.nn.softmax(s.astype(jnp.float32) * SM_SCALE, axis=-1).astype(v.dtype)
        return jnp.einsum('bhqk,bkhd->bqhd', p, v)

    o = lax.map(block, (qn, qp))
    return o.transpose(1, 0, 2, 3, 4).reshape(B, Tq, N_HEADS * V_HEAD)


def mla_context(h, w_dq, g_q, w_uq, w_dkv, g_kv, w_ukv, w_o):
    q_nope, q_pe, c_kv, k_pe = mla_project(h, w_dq, g_q, w_uq, w_dkv, g_kv)
    k_nope, v = mla_expand(c_kv, w_ukv)
    out = mla_attend(q_nope, q_pe, k_nope, k_pe, v) @ w_o
    return out, c_kv, k_pe


def mla_latent(h, ctx_ckv, ctx_kpe, tabs, w_dq, g_q, w_uq, w_dkv, g_kv, w_ukv, w_o):
    q_nope, q_pe, c_kv, k_pe = mla_project(h, w_dq, g_q, w_uq, w_dkv, g_kv)
    q_pe = apply_axial_rope(q_pe, tabs)
    k_pe = apply_axial_rope(k_pe[:, :, None, :], tabs)[:, :, 0, :]
    c_all = jnp.concatenate([ctx_ckv.astype(c_kv.dtype), c_kv], axis=1)
    kpe_all = jnp.concatenate([ctx_kpe.astype(k_pe.dtype), k_pe], axis=1)
    k_nope, v = mla_expand(c_all, w_ukv)
    return mla_attend(q_nope, q_pe, k_nope, kpe_all, v) @ w_o


def setup_inputs(seed: int = 0) -> dict:
    key = jax.random.key(seed)
    ks = jax.random.split(key, 24)

    def nrm(k, shape, scale):
        return jax.random.normal(k, shape, jnp.float32) * scale

    D = D_MODEL
    return {
        'x_prompt': nrm(ks[0], (BATCH, SEQ, D), 1.0),
        'x_sample': nrm(ks[1], (DEC_BATCH, DEC_SEQ, D), 1.0),
        'cache_ckv': nrm(ks[2], (DEC_BATCH, N_MLA_LAYERS, PAST_LEN, KV_RANK), 1.0),
        'cache_kpe': nrm(ks[3], (DEC_BATCH, N_MLA_LAYERS, PAST_LEN, QK_ROPE), 1.0),
        'c': nrm(ks[4], (DEC_BATCH, D), 1.0),
        'c_ctx': nrm(ks[5], (D,), 1.0),
        'w_ada': nrm(ks[6], (DEPTH, D, N_MOD * D), D ** -0.5),
        'b_ada': nrm(ks[7], (DEPTH, N_MOD * D), 0.02),
        'w_fnet': nrm(ks[8], (N_FNET_LAYERS, D, D), D ** -0.5),
        'w_pool': nrm(ks[9], (N_POOL_LAYERS, N_POOL_GROUPS, POOL_GROUP, POOL_GROUP), POOL_GROUP ** -0.5),
        'pool_scale': 1.0 + nrm(ks[10], (N_POOL_LAYERS, D), 0.1),
        'w_dq': nrm(ks[11], (N_MLA_LAYERS, D, Q_RANK), D ** -0.5),
        'g_q': 1.0 + nrm(ks[12], (N_MLA_LAYERS, Q_RANK), 0.1),
        'w_uq': nrm(ks[13], (N_MLA_LAYERS, Q_RANK, N_HEADS * (QK_NOPE + QK_ROPE)), Q_RANK ** -0.5),
        'w_dkv': nrm(ks[14], (N_MLA_LAYERS, D, KV_RANK + QK_ROPE), D ** -0.5),
        'g_kv': 1.0 + nrm(ks[15], (N_MLA_LAYERS, KV_RANK), 0.1),
        'w_ukv': nrm(ks[16], (N_MLA_LAYERS, KV_RANK, N_HEADS * (QK_NOPE + V_HEAD)), KV_RANK ** -0.5),
        'w_o': nrm(ks[17], (N_MLA_LAYERS, N_HEADS * V_HEAD, D), (N_HEADS * V_HEAD) ** -0.5),
        'w_ffn_in': nrm(ks[18], (DEPTH, D, 2 * D_FF), D ** -0.5),
        'w_ffn_conv': nrm(ks[19], (DEPTH, CONV_W, 2 * D_FF), CONV_W ** -0.5),
        'b_ffn_conv': nrm(ks[20], (DEPTH, 2 * D_FF), 0.02),
        'w_ffn_out': nrm(ks[21], (DEPTH, D_FF, D), D_FF ** -0.5),
        'g_final': 1.0 + nrm(ks[22], (D,), 0.1),
    }


def reference(x_prompt, x_sample, cache_ckv, cache_kpe, c, c_ctx, w_ada, b_ada, w_fnet,
              w_pool, pool_scale, w_dq, g_q, w_uq, w_dkv, g_kv, w_ukv, w_o,
              w_ffn_in, w_ffn_conv, b_ffn_conv, w_ffn_out, g_final):
    lat_rope = axial_rope_tables(x_sample.shape[1])
    xp, xs = x_prompt, x_sample
    new_ckv, new_kpe = [], []
    j_f = 0
    j_p = 0
    j_a = 0
    for i in range(DEPTH):
        sh1p, sc1p, g1p, sh2p, sc2p, g2p = ada_mod(c_ctx[None, :], w_ada[i], b_ada[i])
        sh1s, sc1s, g1s, sh2s, sc2s, g2s = ada_mod(c, w_ada[i], b_ada[i])
        hp = modulate(xp, sh1p, sc1p)
        hs = modulate(xs, sh1s, sc1s)
        kind = i % N_MIXERS
        if kind == 0:
            mp = fourier_mix(hp, w_fnet[j_f])
            ms = fourier_mix(hs, w_fnet[j_f])
            j_f += 1
        elif kind == 1:
            mp = pool_mix(hp, w_pool[j_p], pool_scale[j_p])
            ms = pool_mix(hs, w_pool[j_p], pool_scale[j_p])
            j_p += 1
        else:
            prm = (w_dq[j_a], g_q[j_a], w_uq[j_a], w_dkv[j_a], g_kv[j_a], w_ukv[j_a], w_o[j_a])
            mp, ckv_p, kpe_p = mla_context(hp, *prm)
            ms = mla_latent(hs, cache_ckv[:, j_a], cache_kpe[:, j_a], lat_rope, *prm)
            new_ckv.append(ckv_p)
            new_kpe.append(kpe_p)
            j_a += 1
        xp = xp + g1p * mp
        xs = xs + g1s * ms
        xp = xp + g2p * conv_ffn(modulate(xp, sh2p, sc2p), w_ffn_in[i], w_ffn_conv[i],
                                 b_ffn_conv[i], w_ffn_out[i])
        xs = xs + g2s * conv_ffn(modulate(xs, sh2s, sc2s), w_ffn_in[i], w_ffn_conv[i],
                                 b_ffn_conv[i], w_ffn_out[i])
    y_prompt = rms_norm_gain(xp, g_final)
    y_sample = rms_norm_gain(xs, g_final)
    state_ckv = jnp.stack(new_ckv, axis=1)
    state_kpe = jnp.stack(new_kpe, axis=1)
    return (y_prompt, y_sample, state_ckv, state_kpe)
```

```python
import functools

import numpy as np
import jax
import jax.numpy as jnp
from jax import lax
from jax.experimental import pallas as pl
from jax.experimental.pallas import tpu as pltpu

F32 = jnp.float32
BF16 = jnp.bfloat16

D_MODEL = 1024
DEPTH = 4
GRID_W = 64
N_MIXERS = 3
N_FOURIER_GROUPS = 4
FOURIER_GROUP = D_MODEL // N_FOURIER_GROUPS
POOL_WINDOWS = (2, 4, 8, 16)
POOL_GROUP = D_MODEL // len(POOL_WINDOWS)
N_HEADS = 8
QK_NOPE = 128
QK_ROPE = 64
V_HEAD = 128
Q_RANK = 768
KV_RANK = 512
ROPE_AXIS = QK_ROPE // 2
ROPE_BASE = 10000.0
SM_SCALE = (QK_NOPE + QK_ROPE) ** -0.5
D_FF = 2816
N_MOD = 6
EPS = 1e-6

SUBLANES = 8
LANES = 128
BF16_SUBLANES = 16
VMEM_LIMIT_BYTES = 56 * 1024 * 1024

ROW_BLOCK = 256
FFN_CHUNK = 256
FFN_TILE = 512
MLA_TILE = 512
ATT_Q_TILE = 256
ADA_COLS = 2048
HEAD_LANES = 2 * QK_ROPE


def _cparams(n_axes):
    return pltpu.CompilerParams(dimension_semantics=("arbitrary",) * n_axes,
                                vmem_limit_bytes=VMEM_LIMIT_BYTES)


def _rms(x):
    return x * lax.rsqrt(jnp.mean(x * x, axis=-1, keepdims=True) + EPS)


def _modulate(x, mod, sub):
    shift = mod[:, (3 * sub) * D_MODEL:(3 * sub + 1) * D_MODEL]
    scale = mod[:, (3 * sub + 1) * D_MODEL:(3 * sub + 2) * D_MODEL]
    return _rms(x) * (1.0 + scale) + shift


def _gate(mod, sub):
    return mod[:, (3 * sub + 2) * D_MODEL:(3 * sub + 3) * D_MODEL]


def _dot(a, b):
    return jnp.dot(a, b, preferred_element_type=F32)


def _dot_nt(a, b):
    return lax.dot_general(a, b, (((1,), (1,)), ((), ())), preferred_element_type=F32)


def _ada_body(cond_ref, w_ref, b_ref, o_ref):
    a = jax.nn.silu(cond_ref[...]).astype(BF16)
    o_ref[...] = _dot(a, w_ref[...].astype(BF16)) + b_ref[...]


def _ada_all_layers(cond, w_ada, b_ada):
    n_out = N_MOD * D_MODEL
    return pl.pallas_call(
        _ada_body,
        grid=(DEPTH, n_out // ADA_COLS),
        in_specs=[
            pl.BlockSpec((SUBLANES, D_MODEL), lambda l, j: (0, 0)),
            pl.BlockSpec((None, D_MODEL, ADA_COLS), lambda l, j: (l, 0, j)),
            pl.BlockSpec((None, 1, ADA_COLS), lambda l, j: (l, 0, j)),
        ],
        out_specs=pl.BlockSpec((None, SUBLANES, ADA_COLS), lambda l, j: (l, 0, j)),
        out_shape=jax.ShapeDtypeStruct((DEPTH, SUBLANES, n_out), F32),
        compiler_params=_cparams(2),
        name="ada_mod",
    )(cond, w_ada, b_ada.reshape(DEPTH, 1, n_out))


def _ffn_body(seq_len, n_seg, seg, final, *refs):
    if final:
        (x_ref, xp_ref, xn_ref, mod_ref, win_ref, wc_ref, bc_ref, wout_ref, gfin_ref,
         o_ref, hext_ref, u_ref, act_ref, acc_ref) = refs
    else:
        (x_ref, xp_ref, xn_ref, mod_ref, win_ref, wc_ref, bc_ref, wout_ref,
         o_ref, hext_ref, u_ref, act_ref, acc_ref) = refs
    tile = n_seg * seg
    gap = BF16_SUBLANES
    i = pl.program_id(0)
    mod = mod_ref[...]
    x = x_ref[...]

    start = i * tile
    prev_ok = (start % seq_len != 0).astype(F32)
    next_ok = ((start + tile) % seq_len != 0).astype(F32)
    zeros8 = jnp.zeros((SUBLANES, D_MODEL), F32)
    h_prev = _modulate(xp_ref[...], mod, 1) * prev_ok
    h_next = _modulate(xn_ref[...], mod, 1) * next_ok
    hext_ref[0:gap, :] = jnp.concatenate([zeros8, h_prev], axis=0).astype(BF16)
    h = _modulate(x, mod, 1).astype(BF16)
    for k in range(n_seg):
        base = gap + k * (seg + gap)
        hext_ref[base:base + seg, :] = h[k * seg:(k + 1) * seg, :]
        if k == n_seg - 1:
            tail = jnp.concatenate([h_next, zeros8], axis=0).astype(BF16)
        else:
            tail = jnp.zeros((gap, D_MODEL), BF16)
        hext_ref[base + seg:base + seg + gap, :] = tail
    acc_ref[...] = jnp.zeros_like(acc_ref)

    def chunk(c, carry):
        u_ref[...] = _dot(hext_ref[...], win_ref[c])
        wc = wc_ref[c]
        bc = bc_ref[c]
        for k in range(n_seg):
            base = gap + k * (seg + gap)
            conv = (u_ref[base - 1:base - 1 + seg, :] * wc[0:1, :]
                    + u_ref[base:base + seg, :] * wc[1:2, :]
                    + u_ref[base + 1:base + 1 + seg, :] * wc[2:3, :] + bc)
            a = jax.nn.silu(conv[:, :FFN_CHUNK]) * conv[:, FFN_CHUNK:]
            act_ref[k * seg:(k + 1) * seg, :] = a.astype(BF16)
        acc_ref[...] += _dot(act_ref[...], wout_ref[c])
        return carry

    lax.fori_loop(0, D_FF // FFN_CHUNK, chunk, 0)
    y = x + _gate(mod, 1) * acc_ref[...]
    if final:
        y = _rms(y) * gfin_ref[...]
    o_ref[...] = y


def _conv_ffn(x, mod, seq_len, w_in, w_conv, b_conv, w_out, g_final=None):
    n = x.shape[0]
    tile = FFN_TILE
    seg = min(seq_len, tile)
    n_seg = tile // seg
    rows = BF16_SUBLANES + n_seg * (seg + BF16_SUBLANES)
    n_chunks = D_FF // FFN_CHUNK
    per_seq_mod = mod.shape[0] > 1
    final = g_final is not None
    blocks8 = n // SUBLANES
    t8 = tile // SUBLANES

    def resident(shape):
        return pl.BlockSpec(shape, lambda i: (0,) * len(shape), pipeline_mode=pl.Buffered(1))

    in_specs = [
        pl.BlockSpec((tile, D_MODEL), lambda i: (i, 0)),
        pl.BlockSpec((SUBLANES, D_MODEL), lambda i: (jnp.maximum(i * t8 - 1, 0), 0)),
        pl.BlockSpec((SUBLANES, D_MODEL), lambda i: (jnp.minimum((i + 1) * t8, blocks8 - 1), 0)),
        pl.BlockSpec((None, 1, N_MOD * D_MODEL),
                     (lambda i: ((i * tile) // seq_len, 0, 0)) if per_seq_mod else (lambda i: (0, 0, 0))),
        resident((n_chunks, D_MODEL, 2 * FFN_CHUNK)),
        resident((n_chunks, 3, 2 * FFN_CHUNK)),
        resident((n_chunks, 1, 2 * FFN_CHUNK)),
        resident((n_chunks, FFN_CHUNK, D_MODEL)),
    ]
    args = [x, x, x, mod, w_in, w_conv, b_conv, w_out]
    if final:
        in_specs.append(pl.BlockSpec((1, D_MODEL), lambda i: (0, 0)))
        args.append(g_final)
    return pl.pallas_call(
        functools.partial(_ffn_body, seq_len, n_seg, seg, final),
        grid=(n // tile,),
        in_specs=in_specs,
        out_specs=pl.BlockSpec((tile, D_MODEL), lambda i: (i, 0)),
        out_shape=jax.ShapeDtypeStruct((n, D_MODEL), F32),
        scratch_shapes=[
            pltpu.VMEM((rows, D_MODEL), BF16),
            pltpu.VMEM((rows, 2 * FFN_CHUNK), F32),
            pltpu.VMEM((tile, FFN_CHUNK), BF16),
            pltpu.VMEM((tile, D_MODEL), F32),
        ],
        compiler_params=_cparams(1),
        name="conv_ffn",
    )(*args)


def _fnet_body(seq_len, x_ref, mod_ref, csc_ref, csrow_ref, w_ref, o_ref, stack_ref):
    r = pl.program_id(1)
    mod = mod_ref[...]

    @pl.when(r == 0)
    def _():
        def rows(j, carry):
            r0 = pl.multiple_of(j * ROW_BLOCK, ROW_BLOCK)
            h = _modulate(x_ref[pl.ds(r0, ROW_BLOCK), :], mod, 0).astype(BF16)
            for g in range(N_FOURIER_GROUPS):
                cols = slice(g * FOURIER_GROUP, (g + 1) * FOURIER_GROUP)
                y = _dot(h[:, cols], csc_ref[...])
                stack_ref[pl.ds(r0, ROW_BLOCK), cols] = y[:, :FOURIER_GROUP].astype(BF16)
                stack_ref[pl.ds(seq_len + r0, ROW_BLOCK), cols] = y[:, FOURIER_GROUP:].astype(BF16)
            return carry
        lax.fori_loop(0, seq_len // ROW_BLOCK, rows, 0)

    f = _dot(csrow_ref[...], stack_ref[...]) * ((seq_len * FOURIER_GROUP) ** -0.5)
    y = _dot(f.astype(BF16), w_ref[...])
    r0 = pl.multiple_of(r * ROW_BLOCK, ROW_BLOCK)
    o_ref[...] = x_ref[pl.ds(r0, ROW_BLOCK), :] + _gate(mod, 0) * y


def _dft_tables(seq_len):
    c = np.arange(FOURIER_GROUP)
    ang_c = 2.0 * np.pi * ((c[:, None] * c[None, :]) % FOURIER_GROUP) / FOURIER_GROUP
    csc = np.concatenate([np.cos(ang_c), np.sin(ang_c)], axis=1)
    t = np.arange(seq_len)
    ang_t = 2.0 * np.pi * ((t[:, None] * t[None, :]) % seq_len) / seq_len
    csrow = np.concatenate([np.cos(ang_t), -np.sin(ang_t)], axis=1)
    return jnp.asarray(csc, F32).astype(BF16), jnp.asarray(csrow, F32).astype(BF16)


def _fourier_mix(x, mod, seq_len, w):
    n = x.shape[0]
    batch = n // seq_len
    n_row_blocks = seq_len // ROW_BLOCK
    csc, csrow = _dft_tables(seq_len)
    return pl.pallas_call(
        functools.partial(_fnet_body, seq_len),
        grid=(batch, n_row_blocks),
        in_specs=[
            pl.BlockSpec((seq_len, D_MODEL), lambda b, r: (b, 0)),
            pl.BlockSpec((None, 1, N_MOD * D_MODEL),
                         (lambda b, r: (b, 0, 0)) if mod.shape[0] > 1 else (lambda b, r: (0, 0, 0))),
            pl.BlockSpec((FOURIER_GROUP, 2 * FOURIER_GROUP), lambda b, r: (0, 0)),
            pl.BlockSpec((ROW_BLOCK, 2 * seq_len), lambda b, r: (r, 0)),
            pl.BlockSpec((D_MODEL, D_MODEL), lambda b, r: (0, 0)),
        ],
        out_specs=pl.BlockSpec((ROW_BLOCK, D_MODEL), lambda b, r: (b * n_row_blocks + r, 0)),
        out_shape=jax.ShapeDtypeStruct((n, D_MODEL), F32),
        scratch_shapes=[pltpu.VMEM((2 * seq_len, D_MODEL), BF16)],
        compiler_params=_cparams(2),
        name="fourier_mix",
    )(x, mod, csc, csrow, w)


def _pool_body(seq_len, x_ref, mod_ref, w_ref, ps_ref, o_ref, hpad_ref):
    mod = mod_ref[...]
    pad = max(POOL_WINDOWS) // 2
    zeros = jnp.zeros((pad, D_MODEL), F32)
    hpad_ref[0:pad, :] = zeros
    hpad_ref[pad + seq_len:pad + seq_len + pad, :] = zeros
    n_blocks = seq_len // ROW_BLOCK
    for j in range(n_blocks):
        rows = slice(j * ROW_BLOCK, (j + 1) * ROW_BLOCK)
        hpad_ref[pad + j * ROW_BLOCK:pad + (j + 1) * ROW_BLOCK, :] = _modulate(x_ref[rows, :], mod, 0)
    gate = _gate(mod, 0)
    for j in range(n_blocks):
        rows = slice(j * ROW_BLOCK, (j + 1) * ROW_BLOCK)
        base = pad + j * ROW_BLOCK
        t = j * ROW_BLOCK + lax.broadcasted_iota(jnp.int32, (ROW_BLOCK, POOL_GROUP), 0)
        for g, w in enumerate(POOL_WINDOWS):
            cols = slice(g * POOL_GROUP, (g + 1) * POOL_GROUP)
            s = hpad_ref[base - w // 2:base - w // 2 + ROW_BLOCK, cols]
            for off in range(-w // 2 + 1, w // 2):
                s = s + hpad_ref[base + off:base + off + ROW_BLOCK, cols]
            cnt = jnp.minimum(t - w // 2 + w, seq_len) - jnp.maximum(t - w // 2, 0)
            pooled = s / cnt.astype(F32) - hpad_ref[base:base + ROW_BLOCK, cols]
            y = _dot(pooled.astype(BF16), w_ref[g]) * ps_ref[:, cols]
            o_ref[rows, cols] = x_ref[rows, cols] + gate[:, cols] * y


def _pool_mix(x, mod, seq_len, w_pool, pool_scale):
    n = x.shape[0]
    pad = max(POOL_WINDOWS) // 2
    return pl.pallas_call(
        functools.partial(_pool_body, seq_len),
        grid=(n // seq_len,),
        in_specs=[
            pl.BlockSpec((seq_len, D_MODEL), lambda b: (b, 0)),
            pl.BlockSpec((None, 1, N_MOD * D_MODEL),
                         (lambda b: (b, 0, 0)) if mod.shape[0] > 1 else (lambda b: (0, 0, 0))),
            pl.BlockSpec((len(POOL_WINDOWS), POOL_GROUP, POOL_GROUP), lambda b: (0, 0, 0)),
            pl.BlockSpec((1, D_MODEL), lambda b: (0, 0)),
        ],
        out_specs=pl.BlockSpec((seq_len, D_MODEL), lambda b: (b, 0)),
        out_shape=jax.ShapeDtypeStruct((n, D_MODEL), F32),
        scratch_shapes=[pltpu.VMEM((seq_len + 2 * pad, D_MODEL), F32)],
        compiler_params=_cparams(1),
        name="pool_mix",
    )(x, mod, w_pool, pool_scale)


def _rope_pair(a, cs):
    t = a * cs
    return t + pltpu.roll(t, QK_ROPE, 1)


def _mla_proj_body(rope, state, *refs):
    refs = list(refs)
    x_ref, mod_ref = refs[:2]
    del refs[:2]
    cs_ref = refs.pop(0) if rope else None
    wdq_ref, gq_ref, wuq_ref, wdkv_ref, gkv_ref, wukv_ref = refs[:6]
    del refs[:6]
    qn_ref, qr_ref, kn_ref, kr_ref, v_ref = refs[:5]
    del refs[:5]
    h = _modulate(x_ref[...], mod_ref[...], 0).astype(BF16)
    cq = (_rms(_dot(h, wdq_ref[...])) * gq_ref[...]).astype(BF16)
    q = _dot(cq, wuq_ref[...])
    n_nope = N_HEADS * QK_NOPE
    qn_ref[...] = q[:, :n_nope].astype(BF16)
    for hd in range(N_HEADS):
        a = q[:, n_nope + hd * HEAD_LANES:n_nope + (hd + 1) * HEAD_LANES]
        if rope:
            a = _rope_pair(a, cs_ref[...])
        qr_ref[:, hd * HEAD_LANES:(hd + 1) * HEAD_LANES] = a.astype(BF16)
    ckv = _dot(h, wdkv_ref[...])
    c = _rms(ckv[:, :KV_RANK]) * gkv_ref[...]
    kp = ckv[:, KV_RANK:]
    if state:
        ckv_out_ref, kpe_out_ref = refs
        ckv_out_ref[...] = c
        kpe_out_ref[...] = kp[:, :QK_ROPE]
    if rope:
        kp = _rope_pair(kp, cs_ref[...])
    lane = lax.broadcasted_iota(jnp.int32, kp.shape, 1)
    kr_ref[...] = jnp.where(lane < QK_ROPE, kp, 0.0).astype(BF16)
    kv = _dot(c.astype(BF16), wukv_ref[...])
    kn_ref[...] = kv[:, :n_nope].astype(BF16)
    v_ref[...] = kv[:, n_nope:].astype(BF16)


def _mla_project(x, mod, seq_len, prm, cs=None, state=False):
    n = x.shape[0]
    tile = min(MLA_TILE, seq_len)
    tiles_per_seq = seq_len // tile
    rope = cs is not None
    wide = N_HEADS * QK_NOPE

    def const(shape):
        return pl.BlockSpec(shape, lambda i: (0,) * len(shape))

    def rows(width):
        return pl.BlockSpec((tile, width), lambda i: (i, 0))

    in_specs = [rows(D_MODEL),
                pl.BlockSpec((None, 1, N_MOD * D_MODEL),
                             (lambda i: (i // tiles_per_seq, 0, 0)) if mod.shape[0] > 1
                             else (lambda i: (0, 0, 0)))]
    args = [x, mod]
    if rope:
        in_specs.append(pl.BlockSpec((tile, HEAD_LANES), lambda i: (i % tiles_per_seq, 0)))
        args.append(cs)
    for name in ("w_dq", "g_q", "w_uq", "w_dkv", "g_kv", "w_ukv"):
        in_specs.append(const(prm[name].shape))
        args.append(prm[name])
    out_specs = [rows(wide), rows(wide), rows(wide), rows(HEAD_LANES), rows(wide)]
    out_shape = [jax.ShapeDtypeStruct((n, wide), BF16), jax.ShapeDtypeStruct((n, wide), BF16),
                 jax.ShapeDtypeStruct((n, wide), BF16), jax.ShapeDtypeStruct((n, HEAD_LANES), BF16),
                 jax.ShapeDtypeStruct((n, wide), BF16)]
    if state:
        out_specs += [rows(KV_RANK), rows(QK_ROPE)]
        out_shape += [jax.ShapeDtypeStruct((n, KV_RANK), F32), jax.ShapeDtypeStruct((n, QK_ROPE), F32)]
    return pl.pallas_call(
        functools.partial(_mla_proj_body, rope, state),
        grid=(n // tile,),
        in_specs=in_specs,
        out_specs=out_specs,
        out_shape=out_shape,
        compiler_params=_cparams(1),
        name="mla_project",
    )(*args)


def _ctx_expand_body(c_ref, w_ref, kn_ref, v_ref):
    kv = _dot(c_ref[...].astype(BF16), w_ref[...])
    n_nope = N_HEADS * QK_NOPE
    kn_ref[...] = kv[:, :n_nope].astype(BF16)
    v_ref[...] = kv[:, n_nope:].astype(BF16)


def _ctx_expand(c_kv, w_ukv):
    n = c_kv.shape[0]
    wide = N_HEADS * QK_NOPE
    return pl.pallas_call(
        _ctx_expand_body,
        grid=(n // ROW_BLOCK,),
        in_specs=[pl.BlockSpec((ROW_BLOCK, KV_RANK), lambda i: (i, 0)),
                  pl.BlockSpec(w_ukv.shape, lambda i: (0, 0))],
        out_specs=[pl.BlockSpec((ROW_BLOCK, wide), lambda i: (i, 0))] * 2,
        out_shape=[jax.ShapeDtypeStruct((n, wide), BF16)] * 2,
        compiler_params=_cparams(1),
        name="mla_ctx_expand",
    )(c_kv, w_ukv)


def _attn_body(has_ctx, *refs):
    refs = list(refs)
    qn_ref, qr_ref, kn_ref, kr_ref, v_ref = refs[:5]
    del refs[:5]
    if has_ctx:
        knc_ref, krc_ref, vc_ref = refs[:3]
        del refs[:3]
    x_ref, mod_ref, wo_ref, o_ref, heads_ref = refs
    for hd in range(N_HEADS):
        nope = slice(hd * QK_NOPE, (hd + 1) * QK_NOPE)
        pe = slice(hd * HEAD_LANES, (hd + 1) * HEAD_LANES)
        vcols = slice(hd * V_HEAD, (hd + 1) * V_HEAD)
        qn = qn_ref[:, nope]
        qr = qr_ref[:, pe]
        s = (_dot_nt(qn, kn_ref[:, nope]) + _dot_nt(qr, kr_ref[...])) * SM_SCALE
        m = jnp.max(s, axis=-1, keepdims=True)
        if has_ctx:
            sc = (_dot_nt(qn, knc_ref[:, nope]) + _dot_nt(qr, krc_ref[...])) * SM_SCALE
            m = jnp.maximum(m, jnp.max(sc, axis=-1, keepdims=True))
        p = jnp.exp(s - m)
        l = jnp.sum(p, axis=-1, keepdims=True)
        o = _dot(p.astype(BF16), v_ref[:, vcols])
        if has_ctx:
            pc = jnp.exp(sc - m)
            l = l + jnp.sum(pc, axis=-1, keepdims=True)
            o = o + _dot(pc.astype(BF16), vc_ref[:, vcols])
        heads_ref[:, vcols] = (o / l).astype(BF16)
    y = _dot(heads_ref[...], wo_ref[...])
    o_ref[...] = x_ref[...] + _gate(mod_ref[...], 0) * y


def _mla_attend(x, mod, seq_len, qn, qr, kn, kr, v, w_o, ctx=None):
    n = x.shape[0]
    tq = ATT_Q_TILE
    q_tiles = seq_len // tq
    wide = N_HEADS * QK_NOPE
    has_ctx = ctx is not None

    def q_rows(width):
        return pl.BlockSpec((tq, width), lambda b, r: (b * q_tiles + r, 0))

    def seq_rows(length, width):
        return pl.BlockSpec((length, width), lambda b, r: (b, 0))

    in_specs = [q_rows(wide), q_rows(wide), seq_rows(seq_len, wide), seq_rows(seq_len, HEAD_LANES),
                seq_rows(seq_len, wide)]
    args = [qn, qr, kn, kr, v]
    if has_ctx:
        knc, krc, vc = ctx
        past = knc.shape[0] // (n // seq_len)
        in_specs += [seq_rows(past, wide), seq_rows(past, HEAD_LANES), seq_rows(past, wide)]
        args += [knc, krc, vc]
    in_specs += [q_rows(D_MODEL),
                 pl.BlockSpec((None, 1, N_MOD * D_MODEL),
                              (lambda b, r: (b, 0, 0)) if mod.shape[0] > 1 else (lambda b, r: (0, 0, 0))),
                 pl.BlockSpec(w_o.shape, lambda b, r: (0, 0))]
    args += [x, mod, w_o]
    return pl.pallas_call(
        functools.partial(_attn_body, has_ctx),
        grid=(n // seq_len, q_tiles),
        in_specs=in_specs,
        out_specs=q_rows(D_MODEL),
        out_shape=jax.ShapeDtypeStruct((n, D_MODEL), F32),
        scratch_shapes=[pltpu.VMEM((tq, wide), BF16)],
        compiler_params=_cparams(2),
        name="mla_attend",
    )(*args)


def _rope_table(n_tok):
    rows = n_tok // GRID_W
    row = np.repeat(np.arange(rows), GRID_W).astype(np.float32)
    col = np.tile(np.arange(GRID_W), rows).astype(np.float32)
    inv = (1.0 / (ROPE_BASE ** (np.arange(0, ROPE_AXIS, 2, dtype=np.float32) / ROPE_AXIS))).astype(np.float32)
    ang_r = row[:, None] * inv
    ang_c = col[:, None] * inv
    cos = np.concatenate([np.cos(ang_r), np.cos(ang_r), np.cos(ang_c), np.cos(ang_c)], axis=1)
    sin = np.concatenate([np.sin(ang_r), np.sin(ang_r), np.sin(ang_c), np.sin(ang_c)], axis=1)
    return jnp.asarray(np.concatenate([cos, sin], axis=1), F32)


def _swap_cols(w):
    half = ROPE_AXIS // 2
    return jnp.concatenate([-w[..., half:2 * half], w[..., 0:half],
                            -w[..., 3 * half:4 * half], w[..., 2 * half:3 * half]], axis=-1)


def _mla_weights(w_dq, g_q, w_uq, w_dkv, g_kv, w_ukv, w_o):
    wq = w_uq.reshape(Q_RANK, N_HEADS, QK_NOPE + QK_ROPE)
    wq_pe = wq[:, :, QK_NOPE:]
    wq_pairs = jnp.concatenate([wq_pe, _swap_cols(wq_pe)], axis=-1)
    w_uq2 = jnp.concatenate([wq[:, :, :QK_NOPE].reshape(Q_RANK, -1), wq_pairs.reshape(Q_RANK, -1)], axis=1)
    w_pe = w_dkv[:, KV_RANK:]
    w_dkv2 = jnp.concatenate([w_dkv[:, :KV_RANK], w_pe, _swap_cols(w_pe)], axis=1)
    wkv = w_ukv.reshape(KV_RANK, N_HEADS, QK_NOPE + V_HEAD)
    w_ukv2 = jnp.concatenate([wkv[:, :, :QK_NOPE].reshape(KV_RANK, -1),
                              wkv[:, :, QK_NOPE:].reshape(KV_RANK, -1)], axis=1)
    return {"w_dq": w_dq.astype(BF16), "g_q": g_q.reshape(1, Q_RANK), "w_uq": w_uq2.astype(BF16),
            "w_dkv": w_dkv2.astype(BF16), "g_kv": g_kv.reshape(1, KV_RANK), "w_ukv": w_ukv2.astype(BF16),
            "w_o": w_o.astype(BF16)}


def _ffn_weights(w_in, w_conv, b_conv, w_out):
    n_chunks = D_FF // FFN_CHUNK

    def pair(a):
        lead = a.shape[:-1]
        gate = a[..., :D_FF].reshape(*lead, n_chunks, FFN_CHUNK)
        val = a[..., D_FF:].reshape(*lead, n_chunks, FFN_CHUNK)
        return jnp.moveaxis(jnp.concatenate([gate, val], axis=-1), -2, 0)

    return (pair(w_in).astype(BF16), pair(w_conv), pair(b_conv[None, :]),
            w_out.reshape(n_chunks, FFN_CHUNK, D_MODEL).astype(BF16))


def kernel(x_prompt, x_sample, cache_ckv, cache_kpe, c, c_ctx, w_ada, b_ada, w_fnet, w_pool, pool_scale,
           w_dq, g_q, w_uq, w_dkv, g_kv, w_ukv, w_o, w_ffn_in, w_ffn_conv, b_ffn_conv, w_ffn_out, g_final):
    batch, seq, _ = x_prompt.shape
    dec_batch, dec_seq, _ = x_sample.shape
    past = cache_ckv.shape[2]

    cond = jnp.concatenate([c_ctx[None, :], c, jnp.zeros((SUBLANES - 1 - dec_batch, D_MODEL), F32)], axis=0)
    mods = _ada_all_layers(cond, w_ada, b_ada)

    xp = x_prompt.reshape(batch * seq, D_MODEL)
    xs = x_sample.reshape(dec_batch * dec_seq, D_MODEL)
    states = []
    j_f = j_p = j_a = 0
    for i in range(DEPTH):
        mod_p = mods[i, 0:1].reshape(1, 1, N_MOD * D_MODEL)
        mod_s = mods[i, 1:1 + dec_batch].reshape(dec_batch, 1, N_MOD * D_MODEL)
        kind = i % N_MIXERS
        if kind == 0:
            w = w_fnet[j_f].astype(BF16)
            xp = _fourier_mix(xp, mod_p, seq, w)
            xs = _fourier_mix(xs, mod_s, dec_seq, w)
            j_f += 1
        elif kind == 1:
            w = w_pool[j_p].astype(BF16)
            ps = pool_scale[j_p].reshape(1, D_MODEL)
            xp = _pool_mix(xp, mod_p, seq, w, ps)
            xs = _pool_mix(xs, mod_s, dec_seq, w, ps)
            j_p += 1
        else:
            prm = _mla_weights(w_dq[j_a], g_q[j_a], w_uq[j_a], w_dkv[j_a], g_kv[j_a], w_ukv[j_a], w_o[j_a])
            qn, qr, kn, kr, v, ckv_p, kpe_p = _mla_project(xp, mod_p, seq, prm, state=True)
            xp = _mla_attend(xp, mod_p, seq, qn, qr, kn, kr, v, prm["w_o"])
            states.append((ckv_p.reshape(batch, seq, KV_RANK), kpe_p.reshape(batch, seq, QK_ROPE)))
            qn, qr, kn, kr, v = _mla_project(xs, mod_s, dec_seq, prm, cs=_rope_table(dec_seq))
            knc, vc = _ctx_expand(cache_ckv[:, j_a].reshape(dec_batch * past, KV_RANK), prm["w_ukv"])
            kpe_c = cache_kpe[:, j_a].reshape(dec_batch * past, QK_ROPE)
            krc = jnp.concatenate([kpe_c, jnp.zeros_like(kpe_c)], axis=1).astype(BF16)
            xs = _mla_attend(xs, mod_s, dec_seq, qn, qr, kn, kr, v, prm["w_o"], ctx=(knc, krc, vc))
            j_a += 1
        ffn_w = _ffn_weights(w_ffn_in[i], w_ffn_conv[i], b_ffn_conv[i], w_ffn_out[i])
        g_fin = g_final.reshape(1, D_MODEL) if i == DEPTH - 1 else None
        xp = _conv_ffn(xp, mod_p, seq, *ffn_w, g_final=g_fin)
        xs = _conv_ffn(xs, mod_s, dec_seq, *ffn_w, g_final=g_fin)

    state_ckv = jnp.stack([s[0] for s in states], axis=1)
    state_kpe = jnp.stack([s[1] for s in states], axis=1)
    return (xp.reshape(batch, seq, D_MODEL), xs.reshape(dec_batch, dec_seq, D_MODEL), state_ckv, state_kpe)
```

```python
import functools

import numpy as np
import jax
import jax.numpy as jnp
from jax import lax
from jax.experimental import pallas as pl
from jax.experimental.pallas import tpu as pltpu

F32 = jnp.float32
BF16 = jnp.bfloat16

D_MODEL = 1024
DEPTH = 4
GRID_W = 64
N_MIXERS = 3
N_FOURIER_GROUPS = 4
FOURIER_GROUP = D_MODEL // N_FOURIER_GROUPS
POOL_WINDOWS = (2, 4, 8, 16)
POOL_GROUP = D_MODEL // len(POOL_WINDOWS)
N_HEADS = 8
QK_NOPE = 128
QK_ROPE = 64
V_HEAD = 128
Q_RANK = 768
KV_RANK = 512
ROPE_AXIS = QK_ROPE // 2
ROPE_BASE = 10000.0
SM_SCALE = (QK_NOPE + QK_ROPE) ** -0.5
D_FF = 2816
N_MOD = 6
EPS = 1e-6

SUBLANES = 8
LANES = 128
BF16_SUBLANES = 16
VMEM_LIMIT_BYTES = 56 * 1024 * 1024

ROW_BLOCK = 256
FFN_CHUNK = 256
FFN_TILE = 512
MLA_TILE = 512
ATT_Q_TILE = 256
ADA_COLS = 2048
HEAD_LANES = 2 * QK_ROPE


def _cparams(n_axes):
    return pltpu.CompilerParams(dimension_semantics=("arbitrary",) * n_axes,
                                vmem_limit_bytes=VMEM_LIMIT_BYTES)


def _rms(x):
    return x * lax.rsqrt(jnp.mean(x * x, axis=-1, keepdims=True) + EPS)


def _modulate(x, mod, sub):
    shift = mod[:, (3 * sub) * D_MODEL:(3 * sub + 1) * D_MODEL]
    scale = mod[:, (3 * sub + 1) * D_MODEL:(3 * sub + 2) * D_MODEL]
    return _rms(x) * (1.0 + scale) + shift


def _gate(mod, sub):
    return mod[:, (3 * sub + 2) * D_MODEL:(3 * sub + 3) * D_MODEL]


def _dot(a, b):
    return jnp.dot(a, b, preferred_element_type=F32)


def _dot_nt(a, b):
    return lax.dot_general(a, b, (((1,), (1,)), ((), ())), preferred_element_type=F32)


def _ada_body(cond_ref, w_ref, b_ref, o_ref):
    a = jax.nn.silu(cond_ref[...]).astype(BF16)
    o_ref[...] = _dot(a, w_ref[...].astype(BF16)) + b_ref[...]


def _ada_all_layers(cond, w_ada, b_ada):
    n_out = N_MOD * D_MODEL
    return pl.pallas_call(
        _ada_body,
        grid=(DEPTH, n_out // ADA_COLS),
        in_specs=[
            pl.BlockSpec((SUBLANES, D_MODEL), lambda l, j: (0, 0)),
            pl.BlockSpec((None, D_MODEL, ADA_COLS), lambda l, j: (l, 0, j)),
            pl.BlockSpec((None, 1, ADA_COLS), lambda l, j: (l, 0, j)),
        ],
        out_specs=pl.BlockSpec((None, SUBLANES, ADA_COLS), lambda l, j: (l, 0, j)),
        out_shape=jax.ShapeDtypeStruct((DEPTH, SUBLANES, n_out), F32),
        compiler_params=_cparams(2),
        name="ada_mod",
    )(cond, w_ada, b_ada.reshape(DEPTH, 1, n_out))


def _ffn_body(seq_len, n_seg, seg, final, *refs):
    if final:
        (x_ref, xp_ref, xn_ref, mod_ref, win_ref, wc_ref, bc_ref, wout_ref, gfin_ref,
         o_ref, hext_ref, ua_ref, ub_ref, acta_ref, actb_ref, acc_ref) = refs
    else:
        (x_ref, xp_ref, xn_ref, mod_ref, win_ref, wc_ref, bc_ref, wout_ref,
         o_ref, hext_ref, ua_ref, ub_ref, acta_ref, actb_ref, acc_ref) = refs
    tile = n_seg * seg
    gap = BF16_SUBLANES
    i = pl.program_id(0)
    mod = mod_ref[...]
    x = x_ref[...]

    start = i * tile
    prev_ok = (start % seq_len != 0).astype(F32)
    next_ok = ((start + tile) % seq_len != 0).astype(F32)
    zeros8 = jnp.zeros((SUBLANES, D_MODEL), F32)
    h_prev = _modulate(xp_ref[...], mod, 1) * prev_ok
    h_next = _modulate(xn_ref[...], mod, 1) * next_ok
    hext_ref[0:gap, :] = jnp.concatenate([zeros8, h_prev], axis=0).astype(BF16)
    h = _modulate(x, mod, 1).astype(BF16)
    for k in range(n_seg):
        base = gap + k * (seg + gap)
        hext_ref[base:base + seg, :] = h[k * seg:(k + 1) * seg, :]
        if k == n_seg - 1:
            tail = jnp.concatenate([h_next, zeros8], axis=0).astype(BF16)
        else:
            tail = jnp.zeros((gap, D_MODEL), BF16)
        hext_ref[base + seg:base + seg + gap, :] = tail
    acc_ref[...] = jnp.zeros_like(acc_ref)

    def up(c, u_ref):
        u_ref[...] = _dot(hext_ref[...], win_ref[c])

    def down(c, u_ref, act_ref):
        wc = wc_ref[c]
        bc = bc_ref[c]
        for k in range(n_seg):
            b = gap + k * (seg + gap)
            conv = (u_ref[b - 1:b - 1 + seg, :] * wc[0:1, :]
                    + u_ref[b:b + seg, :] * wc[1:2, :]
                    + u_ref[b + 1:b + 1 + seg, :] * wc[2:3, :] + bc)
            a = jax.nn.silu(conv[:, :FFN_CHUNK]) * conv[:, FFN_CHUNK:]
            act_ref[k * seg:(k + 1) * seg, :] = a.astype(BF16)
        acc_ref[...] += _dot(act_ref[...], wout_ref[c])

    n_chunks = D_FF // FFN_CHUNK
    assert n_chunks % 2 == 1
    up(0, ua_ref)

    def pair(j, carry):
        c0 = 2 * j
        up(c0 + 1, ub_ref)
        down(c0, ua_ref, acta_ref)
        up(c0 + 2, ua_ref)
        down(c0 + 1, ub_ref, actb_ref)
        return carry

    lax.fori_loop(0, (n_chunks - 1) // 2, pair, 0)
    down(n_chunks - 1, ua_ref, acta_ref)
    y = x + _gate(mod, 1) * acc_ref[...]
    if final:
        y = _rms(y) * gfin_ref[...]
    o_ref[...] = y


def _conv_ffn(x, mod, seq_len, layer, w_in, w_conv, b_conv, w_out, g_final=None):
    n = x.shape[0]
    tile = FFN_TILE
    seg = min(seq_len, tile)
    n_seg = tile // seg
    rows = BF16_SUBLANES + n_seg * (seg + BF16_SUBLANES)
    n_chunks = D_FF // FFN_CHUNK
    per_seq_mod = mod.shape[0] > 1
    final = g_final is not None
    blocks8 = n // SUBLANES
    t8 = tile // SUBLANES

    def resident(shape):
        return pl.BlockSpec((None,) + shape, lambda i: (layer,) + (0,) * len(shape),
                            pipeline_mode=pl.Buffered(1))

    in_specs = [
        pl.BlockSpec((tile, D_MODEL), lambda i: (i, 0)),
        pl.BlockSpec((SUBLANES, D_MODEL), lambda i: (jnp.maximum(i * t8 - 1, 0), 0)),
        pl.BlockSpec((SUBLANES, D_MODEL), lambda i: (jnp.minimum((i + 1) * t8, blocks8 - 1), 0)),
        pl.BlockSpec((None, 1, N_MOD * D_MODEL),
                     (lambda i: ((i * tile) // seq_len, 0, 0)) if per_seq_mod else (lambda i: (0, 0, 0))),
        resident((n_chunks, D_MODEL, 2 * FFN_CHUNK)),
        resident((n_chunks, 3, 2 * FFN_CHUNK)),
        resident((n_chunks, 1, 2 * FFN_CHUNK)),
        resident((n_chunks, FFN_CHUNK, D_MODEL)),
    ]
    args = [x, x, x, mod, w_in, w_conv, b_conv, w_out]
    if final:
        in_specs.append(pl.BlockSpec((1, D_MODEL), lambda i: (0, 0)))
        args.append(g_final)
    return pl.pallas_call(
        functools.partial(_ffn_body, seq_len, n_seg, seg, final),
        grid=(n // tile,),
        in_specs=in_specs,
        out_specs=pl.BlockSpec((tile, D_MODEL), lambda i: (i, 0)),
        out_shape=jax.ShapeDtypeStruct((n, D_MODEL), F32),
        scratch_shapes=[
            pltpu.VMEM((rows, D_MODEL), BF16),
            pltpu.VMEM((rows, 2 * FFN_CHUNK), F32),
            pltpu.VMEM((rows, 2 * FFN_CHUNK), F32),
            pltpu.VMEM((tile, FFN_CHUNK), BF16),
            pltpu.VMEM((tile, FFN_CHUNK), BF16),
            pltpu.VMEM((tile, D_MODEL), F32),
        ],
        compiler_params=_cparams(1),
        name="conv_ffn",
    )(*args)


def _fnet_body(seq_len, x_ref, mod_ref, csc_ref, csrow_ref, w_ref, o_ref, stack_ref):
    r = pl.program_id(1)
    mod = mod_ref[...]

    @pl.when(r == 0)
    def _():
        def rows(j, carry):
            r0 = pl.multiple_of(j * ROW_BLOCK, ROW_BLOCK)
            h = _modulate(x_ref[pl.ds(r0, ROW_BLOCK), :], mod, 0).astype(BF16)
            for g in range(N_FOURIER_GROUPS):
                cols = slice(g * FOURIER_GROUP, (g + 1) * FOURIER_GROUP)
                y = _dot(h[:, cols], csc_ref[...].astype(BF16))
                stack_ref[pl.ds(r0, ROW_BLOCK), cols] = y[:, :FOURIER_GROUP].astype(BF16)
                stack_ref[pl.ds(seq_len + r0, ROW_BLOCK), cols] = y[:, FOURIER_GROUP:].astype(BF16)
            return carry
        lax.fori_loop(0, seq_len // ROW_BLOCK, rows, 0)

    f = _dot(csrow_ref[...].astype(BF16), stack_ref[...]) * ((seq_len * FOURIER_GROUP) ** -0.5)
    y = _dot(f.astype(BF16), w_ref[...])
    r0 = pl.multiple_of(r * ROW_BLOCK, ROW_BLOCK)
    o_ref[...] = x_ref[pl.ds(r0, ROW_BLOCK), :] + _gate(mod, 0) * y


def _dft_tables(seq_len):
    c = np.arange(FOURIER_GROUP)
    ang_c = 2.0 * np.pi * ((c[:, None] * c[None, :]) % FOURIER_GROUP) / FOURIER_GROUP
    csc = np.concatenate([np.cos(ang_c), np.sin(ang_c)], axis=1)
    t = np.arange(seq_len)
    ang_t = 2.0 * np.pi * ((t[:, None] * t[None, :]) % seq_len) / seq_len
    csrow = np.concatenate([np.cos(ang_t), -np.sin(ang_t)], axis=1)
    return jnp.asarray(csc, F32), jnp.asarray(csrow, F32)


def _fourier_mix(x, mod, seq_len, w):
    n = x.shape[0]
    batch = n // seq_len
    n_row_blocks = seq_len // ROW_BLOCK
    csc, csrow = _dft_tables(seq_len)
    return pl.pallas_call(
        functools.partial(_fnet_body, seq_len),
        grid=(batch, n_row_blocks),
        in_specs=[
            pl.BlockSpec((seq_len, D_MODEL), lambda b, r: (b, 0)),
            pl.BlockSpec((None, 1, N_MOD * D_MODEL),
                         (lambda b, r: (b, 0, 0)) if mod.shape[0] > 1 else (lambda b, r: (0, 0, 0))),
            pl.BlockSpec((FOURIER_GROUP, 2 * FOURIER_GROUP), lambda b, r: (0, 0)),
            pl.BlockSpec((ROW_BLOCK, 2 * seq_len), lambda b, r: (r, 0)),
            pl.BlockSpec((D_MODEL, D_MODEL), lambda b, r: (0, 0)),
        ],
        out_specs=pl.BlockSpec((ROW_BLOCK, D_MODEL), lambda b, r: (b * n_row_blocks + r, 0)),
        out_shape=jax.ShapeDtypeStruct((n, D_MODEL), F32),
        scratch_shapes=[pltpu.VMEM((2 * seq_len, D_MODEL), BF16)],
        compiler_params=_cparams(2),
        name="fourier_mix",
    )(x, mod, csc, csrow, w)


def _pool_body(seq_len, x_ref, mod_ref, w_ref, ps_ref, o_ref, hpad_ref):
    mod = mod_ref[...]
    pad = max(POOL_WINDOWS) // 2
    zeros = jnp.zeros((pad, D_MODEL), F32)
    hpad_ref[0:pad, :] = zeros
    hpad_ref[pad + seq_len:pad + seq_len + pad, :] = zeros
    n_blocks = seq_len // ROW_BLOCK
    for j in range(n_blocks):
        rows = slice(j * ROW_BLOCK, (j + 1) * ROW_BLOCK)
        hpad_ref[pad + j * ROW_BLOCK:pad + (j + 1) * ROW_BLOCK, :] = _modulate(x_ref[rows, :], mod, 0)
    gate = _gate(mod, 0)
    for j in range(n_blocks):
        rows = slice(j * ROW_BLOCK, (j + 1) * ROW_BLOCK)
        base = pad + j * ROW_BLOCK
        t = j * ROW_BLOCK + lax.broadcasted_iota(jnp.int32, (ROW_BLOCK, POOL_GROUP), 0)
        for g, w in enumerate(POOL_WINDOWS):
            cols = slice(g * POOL_GROUP, (g + 1) * POOL_GROUP)
            s = hpad_ref[base - w // 2:base - w // 2 + ROW_BLOCK, cols]
            for off in range(-w // 2 + 1, w // 2):
                s = s + hpad_ref[base + off:base + off + ROW_BLOCK, cols]
            cnt = jnp.minimum(t - w // 2 + w, seq_len) - jnp.maximum(t - w // 2, 0)
            pooled = s / cnt.astype(F32) - hpad_ref[base:base + ROW_BLOCK, cols]
            y = _dot(pooled.astype(BF16), w_ref[g]) * ps_ref[:, cols]
            o_ref[rows, cols] = x_ref[rows, cols] + gate[:, cols] * y


def _pool_mix(x, mod, seq_len, w_pool, pool_scale):
    n = x.shape[0]
    pad = max(POOL_WINDOWS) // 2
    return pl.pallas_call(
        functools.partial(_pool_body, seq_len),
        grid=(n // seq_len,),
        in_specs=[
            pl.BlockSpec((seq_len, D_MODEL), lambda b: (b, 0)),
            pl.BlockSpec((None, 1, N_MOD * D_MODEL),
                         (lambda b: (b, 0, 0)) if mod.shape[0] > 1 else (lambda b: (0, 0, 0))),
            pl.BlockSpec((len(POOL_WINDOWS), POOL_GROUP, POOL_GROUP), lambda b: (0, 0, 0)),
            pl.BlockSpec((1, D_MODEL), lambda b: (0, 0)),
        ],
        out_specs=pl.BlockSpec((seq_len, D_MODEL), lambda b: (b, 0)),
        out_shape=jax.ShapeDtypeStruct((n, D_MODEL), F32),
        scratch_shapes=[pltpu.VMEM((seq_len + 2 * pad, D_MODEL), F32)],
        compiler_params=_cparams(1),
        name="pool_mix",
    )(x, mod, w_pool, pool_scale)


def _rope_pair(a, cs):
    t = a * cs
    return t + pltpu.roll(t, QK_ROPE, 1)


def _mla_proj_body(rope, state, *refs):
    refs = list(refs)
    x_ref, mod_ref = refs[:2]
    del refs[:2]
    cs_ref = refs.pop(0) if rope else None
    wdq_ref, gq_ref, wuq_ref, wdkv_ref, gkv_ref, wukv_ref = refs[:6]
    del refs[:6]
    qn_ref, qr_ref, kn_ref, kr_ref, v_ref = refs[:5]
    del refs[:5]
    h = _modulate(x_ref[...], mod_ref[...], 0).astype(BF16)
    cq = (_rms(_dot(h, wdq_ref[...])) * gq_ref[...]).astype(BF16)
    q = _dot(cq, wuq_ref[...])
    n_nope = N_HEADS * QK_NOPE
    qn_ref[...] = q[:, :n_nope].astype(BF16)
    for hd in range(N_HEADS):
        a = q[:, n_nope + hd * HEAD_LANES:n_nope + (hd + 1) * HEAD_LANES]
        if rope:
            a = _rope_pair(a, cs_ref[...])
        qr_ref[:, hd * HEAD_LANES:(hd + 1) * HEAD_LANES] = a.astype(BF16)
    ckv = _dot(h, wdkv_ref[...])
    c = _rms(ckv[:, :KV_RANK]) * gkv_ref[...]
    kp = ckv[:, KV_RANK:]
    if state:
        ckv_out_ref, kpe_out_ref = refs
        ckv_out_ref[...] = c
        kpe_out_ref[...] = kp[:, :QK_ROPE]
    if rope:
        kp = _rope_pair(kp, cs_ref[...])
    lane = lax.broadcasted_iota(jnp.int32, kp.shape, 1)
    kr_ref[...] = jnp.where(lane < QK_ROPE, kp, 0.0).astype(BF16)
    kv = _dot(c.astype(BF16), wukv_ref[...])
    kn_ref[...] = kv[:, :n_nope].astype(BF16)
    v_ref[...] = kv[:, n_nope:].astype(BF16)


def _mla_project(x, mod, seq_len, prm, cs=None, state=False):
    n = x.shape[0]
    tile = min(MLA_TILE, seq_len)
    tiles_per_seq = seq_len // tile
    rope = cs is not None
    wide = N_HEADS * QK_NOPE

    def const(shape):
        return pl.BlockSpec(shape, lambda i: (0,) * len(shape))

    def rows(width):
        return pl.BlockSpec((tile, width), lambda i: (i, 0))

    in_specs = [rows(D_MODEL),
                pl.BlockSpec((None, 1, N_MOD * D_MODEL),
                             (lambda i: (i // tiles_per_seq, 0, 0)) if mod.shape[0] > 1
                             else (lambda i: (0, 0, 0)))]
    args = [x, mod]
    if rope:
        in_specs.append(pl.BlockSpec((tile, HEAD_LANES), lambda i: (i % tiles_per_seq, 0)))
        args.append(cs)
    for name in ("w_dq", "g_q", "w_uq", "w_dkv", "g_kv", "w_ukv"):
        in_specs.append(const(prm[name].shape))
        args.append(prm[name])
    out_specs = [rows(wide), rows(wide), rows(wide), rows(HEAD_LANES), rows(wide)]
    out_shape = [jax.ShapeDtypeStruct((n, wide), BF16), jax.ShapeDtypeStruct((n, wide), BF16),
                 jax.ShapeDtypeStruct((n, wide), BF16), jax.ShapeDtypeStruct((n, HEAD_LANES), BF16),
                 jax.ShapeDtypeStruct((n, wide), BF16)]
    if state:
        out_specs += [rows(KV_RANK), rows(QK_ROPE)]
        out_shape += [jax.ShapeDtypeStruct((n, KV_RANK), F32), jax.ShapeDtypeStruct((n, QK_ROPE), F32)]
    return pl.pallas_call(
        functools.partial(_mla_proj_body, rope, state),
        grid=(n // tile,),
        in_specs=in_specs,
        out_specs=out_specs,
        out_shape=out_shape,
        compiler_params=_cparams(1),
        name="mla_project",
    )(*args)


def _ctx_expand_body(c_ref, w_ref, kn_ref, v_ref):
    kv = _dot(c_ref[...].astype(BF16), w_ref[...])
    n_nope = N_HEADS * QK_NOPE
    kn_ref[...] = kv[:, :n_nope].astype(BF16)
    v_ref[...] = kv[:, n_nope:].astype(BF16)


def _ctx_expand(c_kv, w_ukv):
    n = c_kv.shape[0]
    wide = N_HEADS * QK_NOPE
    return pl.pallas_call(
        _ctx_expand_body,
        grid=(n // ROW_BLOCK,),
        in_specs=[pl.BlockSpec((ROW_BLOCK, KV_RANK), lambda i: (i, 0)),
                  pl.BlockSpec(w_ukv.shape, lambda i: (0, 0))],
        out_specs=[pl.BlockSpec((ROW_BLOCK, wide), lambda i: (i, 0))] * 2,
        out_shape=[jax.ShapeDtypeStruct((n, wide), BF16)] * 2,
        compiler_params=_cparams(1),
        name="mla_ctx_expand",
    )(c_kv, w_ukv)


def _attn_body(has_ctx, *refs):
    refs = list(refs)
    qn_ref, qr_ref, kn_ref, kr_ref, v_ref = refs[:5]
    del refs[:5]
    if has_ctx:
        knc_ref, krc_ref, vc_ref = refs[:3]
        del refs[:3]
    x_ref, mod_ref, wo_ref, o_ref, heads_ref = refs
    for hd in range(N_HEADS):
        nope = slice(hd * QK_NOPE, (hd + 1) * QK_NOPE)
        pe = slice(hd * HEAD_LANES, (hd + 1) * HEAD_LANES)
        vcols = slice(hd * V_HEAD, (hd + 1) * V_HEAD)
        qn = qn_ref[:, nope]
        qr = qr_ref[:, pe]
        s = (_dot_nt(qn, kn_ref[:, nope]) + _dot_nt(qr, kr_ref[...])) * SM_SCALE
        m = jnp.max(s, axis=-1, keepdims=True)
        if has_ctx:
            sc = (_dot_nt(qn, knc_ref[:, nope]) + _dot_nt(qr, krc_ref[...])) * SM_SCALE
            m = jnp.maximum(m, jnp.max(sc, axis=-1, keepdims=True))
        p = jnp.exp(s - m)
        l = jnp.sum(p, axis=-1, keepdims=True)
        o = _dot(p.astype(BF16), v_ref[:, vcols])
        if has_ctx:
            pc = jnp.exp(sc - m)
            l = l + jnp.sum(pc, axis=-1, keepdims=True)
            o = o + _dot(pc.astype(BF16), vc_ref[:, vcols])
        heads_ref[:, vcols] = (o / l).astype(BF16)
    y = _dot(heads_ref[...], wo_ref[...])
    o_ref[...] = x_ref[...] + _gate(mod_ref[...], 0) * y


def _mla_attend(x, mod, seq_len, qn, qr, kn, kr, v, w_o, ctx=None):
    n = x.shape[0]
    tq = ATT_Q_TILE
    q_tiles = seq_len // tq
    wide = N_HEADS * QK_NOPE
    has_ctx = ctx is not None

    def q_rows(width):
        return pl.BlockSpec((tq, width), lambda b, r: (b * q_tiles + r, 0))

    def seq_rows(length, width):
        return pl.BlockSpec((length, width), lambda b, r: (b, 0))

    in_specs = [q_rows(wide), q_rows(wide), seq_rows(seq_len, wide), seq_rows(seq_len, HEAD_LANES),
                seq_rows(seq_len, wide)]
    args = [qn, qr, kn, kr, v]
    if has_ctx:
        knc, krc, vc = ctx
        past = knc.shape[0] // (n // seq_len)
        in_specs += [seq_rows(past, wide), seq_rows(past, HEAD_LANES), seq_rows(past, wide)]
        args += [knc, krc, vc]
    in_specs += [q_rows(D_MODEL),
                 pl.BlockSpec((None, 1, N_MOD * D_MODEL),
                              (lambda b, r: (b, 0, 0)) if mod.shape[0] > 1 else (lambda b, r: (0, 0, 0))),
                 pl.BlockSpec(w_o.shape, lambda b, r: (0, 0))]
    args += [x, mod, w_o]
    return pl.pallas_call(
        functools.partial(_attn_body, has_ctx),
        grid=(n // seq_len, q_tiles),
        in_specs=in_specs,
        out_specs=q_rows(D_MODEL),
        out_shape=jax.ShapeDtypeStruct((n, D_MODEL), F32),
        scratch_shapes=[pltpu.VMEM((tq, wide), BF16)],
        compiler_params=_cparams(2),
        name="mla_attend",
    )(*args)


def _rope_table(n_tok):
    rows = n_tok // GRID_W
    row = np.repeat(np.arange(rows), GRID_W).astype(np.float32)
    col = np.tile(np.arange(GRID_W), rows).astype(np.float32)
    inv = (1.0 / (ROPE_BASE ** (np.arange(0, ROPE_AXIS, 2, dtype=np.float32) / ROPE_AXIS))).astype(np.float32)
    ang_r = row[:, None] * inv
    ang_c = col[:, None] * inv
    cos = np.concatenate([np.cos(ang_r), np.cos(ang_r), np.cos(ang_c), np.cos(ang_c)], axis=1)
    sin = np.concatenate([np.sin(ang_r), np.sin(ang_r), np.sin(ang_c), np.sin(ang_c)], axis=1)
    return jnp.asarray(np.concatenate([cos, sin], axis=1), F32)


def _swap_cols(w):
    half = ROPE_AXIS // 2
    return jnp.concatenate([-w[..., half:2 * half], w[..., 0:half],
                            -w[..., 3 * half:4 * half], w[..., 2 * half:3 * half]], axis=-1)


def _mla_weights(w_dq, g_q, w_uq, w_dkv, g_kv, w_ukv, w_o):
    wq = w_uq.reshape(Q_RANK, N_HEADS, QK_NOPE + QK_ROPE)
    wq_pe = wq[:, :, QK_NOPE:]
    wq_pairs = jnp.concatenate([wq_pe, _swap_cols(wq_pe)], axis=-1)
    w_uq2 = jnp.concatenate([wq[:, :, :QK_NOPE].reshape(Q_RANK, -1), wq_pairs.reshape(Q_RANK, -1)], axis=1)
    w_pe = w_dkv[:, KV_RANK:]
    w_dkv2 = jnp.concatenate([w_dkv[:, :KV_RANK], w_pe, _swap_cols(w_pe)], axis=1)
    wkv = w_ukv.reshape(KV_RANK, N_HEADS, QK_NOPE + V_HEAD)
    w_ukv2 = jnp.concatenate([wkv[:, :, :QK_NOPE].reshape(KV_RANK, -1),
                              wkv[:, :, QK_NOPE:].reshape(KV_RANK, -1)], axis=1)
    return {"w_dq": w_dq.astype(BF16), "g_q": g_q.reshape(1, Q_RANK), "w_uq": w_uq2.astype(BF16),
            "w_dkv": w_dkv2.astype(BF16), "g_kv": g_kv.reshape(1, KV_RANK), "w_ukv": w_ukv2.astype(BF16),
            "w_o": w_o.astype(BF16)}


def _ffn_prep_body(gate_ref, val_ref, wout_ref, win_o_ref, wout_o_ref):
    win_o_ref[:, :FFN_CHUNK] = gate_ref[...].astype(BF16)
    win_o_ref[:, FFN_CHUNK:] = val_ref[...].astype(BF16)
    wout_o_ref[...] = wout_ref[...].astype(BF16)


def _ffn_weights(w_in, w_conv, b_conv, w_out):
    n_chunks = D_FF // FFN_CHUNK
    w_in2, w_out2 = pl.pallas_call(
        _ffn_prep_body,
        grid=(DEPTH, n_chunks),
        in_specs=[
            pl.BlockSpec((None, D_MODEL, FFN_CHUNK), lambda l, c: (l, 0, c)),
            pl.BlockSpec((None, D_MODEL, FFN_CHUNK), lambda l, c: (l, 0, n_chunks + c)),
            pl.BlockSpec((None, FFN_CHUNK, D_MODEL), lambda l, c: (l, c, 0)),
        ],
        out_specs=[
            pl.BlockSpec((None, None, D_MODEL, 2 * FFN_CHUNK), lambda l, c: (l, c, 0, 0)),
            pl.BlockSpec((None, None, FFN_CHUNK, D_MODEL), lambda l, c: (l, c, 0, 0)),
        ],
        out_shape=[jax.ShapeDtypeStruct((DEPTH, n_chunks, D_MODEL, 2 * FFN_CHUNK), BF16),
                   jax.ShapeDtypeStruct((DEPTH, n_chunks, FFN_CHUNK, D_MODEL), BF16)],
        compiler_params=_cparams(2),
        name="ffn_weight_prep",
    )(w_in, w_in, w_out)

    def pair(a):
        gate = a[..., :D_FF].reshape(DEPTH, -1, n_chunks, FFN_CHUNK)
        val = a[..., D_FF:].reshape(DEPTH, -1, n_chunks, FFN_CHUNK)
        return jnp.moveaxis(jnp.concatenate([gate, val], axis=-1), 2, 1)

    return w_in2, pair(w_conv), pair(b_conv[:, None, :]), w_out2


def kernel(x_prompt, x_sample, cache_ckv, cache_kpe, c, c_ctx, w_ada, b_ada, w_fnet, w_pool, pool_scale,
           w_dq, g_q, w_uq, w_dkv, g_kv, w_ukv, w_o, w_ffn_in, w_ffn_conv, b_ffn_conv, w_ffn_out, g_final):
    batch, seq, _ = x_prompt.shape
    dec_batch, dec_seq, _ = x_sample.shape
    past = cache_ckv.shape[2]

    cond = jnp.concatenate([c_ctx[None, :], c, jnp.zeros((SUBLANES - 1 - dec_batch, D_MODEL), F32)], axis=0)
    mods = _ada_all_layers(cond, w_ada, b_ada)
    ffn_w = _ffn_weights(w_ffn_in, w_ffn_conv, b_ffn_conv, w_ffn_out)

    xp = x_prompt.reshape(batch * seq, D_MODEL)
    xs = x_sample.reshape(dec_batch * dec_seq, D_MODEL)
    states = []
    j_f = j_p = j_a = 0
    for i in range(DEPTH):
        mod_p = mods[i, 0:1].reshape(1, 1, N_MOD * D_MODEL)
        mod_s = mods[i, 1:1 + dec_batch].reshape(dec_batch, 1, N_MOD * D_MODEL)
        kind = i % N_MIXERS
        if kind == 0:
            w = w_fnet[j_f].astype(BF16)
            xp = _fourier_mix(xp, mod_p, seq, w)
            xs = _fourier_mix(xs, mod_s, dec_seq, w)
            j_f += 1
        elif kind == 1:
            w = w_pool[j_p].astype(BF16)
            ps = pool_scale[j_p].reshape(1, D_MODEL)
            xp = _pool_mix(xp, mod_p, seq, w, ps)
            xs = _pool_mix(xs, mod_s, dec_seq, w, ps)
            j_p += 1
        else:
            prm = _mla_weights(w_dq[j_a], g_q[j_a], w_uq[j_a], w_dkv[j_a], g_kv[j_a], w_ukv[j_a], w_o[j_a])
            qn, qr, kn, kr, v, ckv_p, kpe_p = _mla_project(xp, mod_p, seq, prm, state=True)
            xp = _mla_attend(xp, mod_p, seq, qn, qr, kn, kr, v, prm["w_o"])
            states.append((ckv_p.reshape(batch, seq, KV_RANK), kpe_p.reshape(batch, seq, QK_ROPE)))
            qn, qr, kn, kr, v = _mla_project(xs, mod_s, dec_seq, prm, cs=_rope_table(dec_seq))
            knc, vc = _ctx_expand(cache_ckv[:, j_a].reshape(dec_batch * past, KV_RANK), prm["w_ukv"])
            kpe_c = cache_kpe[:, j_a].reshape(dec_batch * past, QK_ROPE)
            krc = jnp.concatenate([kpe_c, jnp.zeros_like(kpe_c)], axis=1).astype(BF16)
            xs = _mla_attend(xs, mod_s, dec_seq, qn, qr, kn, kr, v, prm["w_o"], ctx=(knc, krc, vc))
            j_a += 1
        g_fin = g_final.reshape(1, D_MODEL) if i == DEPTH - 1 else None
        xp = _conv_ffn(xp, mod_p, seq, i, *ffn_w, g_final=g_fin)
        xs = _conv_ffn(xs, mod_s, dec_seq, i, *ffn_w, g_final=g_fin)

    state_ckv = jnp.stack([s[0] for s in states], axis=1)
    state_kpe = jnp.stack([s[1] for s in states], axis=1)
    return (xp.reshape(batch, seq, D_MODEL), xs.reshape(dec_batch, dec_seq, D_MODEL), state_ckv, state_kpe)
```

```python
import functools

import numpy as np
import jax
import jax.numpy as jnp
from jax import lax
from jax.experimental import pallas as pl
from jax.experimental.pallas import tpu as pltpu

F32 = jnp.float32
BF16 = jnp.bfloat16

D_MODEL = 1024
DEPTH = 4
GRID_W = 64
N_MIXERS = 3
N_FOURIER_GROUPS = 4
FOURIER_GROUP = D_MODEL // N_FOURIER_GROUPS
POOL_WINDOWS = (2, 4, 8, 16)
POOL_GROUP = D_MODEL // len(POOL_WINDOWS)
N_HEADS = 8
QK_NOPE = 128
QK_ROPE = 64
V_HEAD = 128
Q_RANK = 768
KV_RANK = 512
ROPE_AXIS = QK_ROPE // 2
ROPE_BASE = 10000.0
SM_SCALE = (QK_NOPE + QK_ROPE) ** -0.5
D_FF = 2816
N_MOD = 6
EPS = 1e-6

SUBLANES = 8
LANES = 128
BF16_SUBLANES = 16
VMEM_LIMIT_BYTES = 56 * 1024 * 1024

ROW_BLOCK = 256
FFN_CHUNK = 512
FFN_TILE = 512
MLA_TILE = 512
ATT_Q_TILE = 256
ADA_COLS = 2048
HEAD_LANES = 2 * QK_ROPE
QK_TILE = QK_NOPE + HEAD_LANES


def _cparams(n_axes):
    return pltpu.CompilerParams(dimension_semantics=("arbitrary",) * n_axes,
                                vmem_limit_bytes=VMEM_LIMIT_BYTES)


def _rms(x):
    return x * lax.rsqrt(jnp.mean(x * x, axis=-1, keepdims=True) + EPS)


def _modulate(x, mod, sub):
    shift = mod[:, (3 * sub) * D_MODEL:(3 * sub + 1) * D_MODEL]
    scale = mod[:, (3 * sub + 1) * D_MODEL:(3 * sub + 2) * D_MODEL]
    return _rms(x) * (1.0 + scale) + shift


def _gate(mod, sub):
    return mod[:, (3 * sub + 2) * D_MODEL:(3 * sub + 3) * D_MODEL]


def _dot(a, b):
    return jnp.dot(a, b, preferred_element_type=F32)


def _dot_nt(a, b):
    return lax.dot_general(a, b, (((1,), (1,)), ((), ())), preferred_element_type=F32)


def _ada_body(cond_ref, w_ref, b_ref, o_ref):
    a = jax.nn.silu(cond_ref[...]).astype(BF16)
    o_ref[...] = _dot(a, w_ref[...].astype(BF16)) + b_ref[...]


def _ada_all_layers(cond, w_ada, b_ada):
    n_out = N_MOD * D_MODEL
    return pl.pallas_call(
        _ada_body,
        grid=(DEPTH, n_out // ADA_COLS),
        in_specs=[
            pl.BlockSpec((SUBLANES, D_MODEL), lambda l, j: (0, 0)),
            pl.BlockSpec((None, D_MODEL, ADA_COLS), lambda l, j: (l, 0, j)),
            pl.BlockSpec((None, 1, ADA_COLS), lambda l, j: (l, 0, j)),
        ],
        out_specs=pl.BlockSpec((None, SUBLANES, ADA_COLS), lambda l, j: (l, 0, j)),
        out_shape=jax.ShapeDtypeStruct((DEPTH, SUBLANES, n_out), F32),
        compiler_params=_cparams(2),
        name="ada_mod",
    )(cond, w_ada, b_ada.reshape(DEPTH, 1, n_out))


def _ffn_chunks():
    edges = list(range(0, D_FF, FFN_CHUNK)) + [D_FF]
    return list(zip(edges[:-1], edges[1:]))


def _ffn_body(seq_len, n_seg, seg, final, *refs):
    refs = list(refs)
    x_ref, xp_ref, xn_ref, mod_ref, win_ref, wc_ref, bc_ref, wout_ref = refs[:8]
    del refs[:8]
    gfin_ref = refs.pop(0) if final else None
    o_ref, hext_ref, uga_ref, uva_ref, ugb_ref, uvb_ref, acta_ref, actb_ref, acc_ref = refs
    tile = n_seg * seg
    gap = BF16_SUBLANES
    i = pl.program_id(0)
    mod = mod_ref[...]
    x = x_ref[...]

    start = i * tile
    prev_ok = (start % seq_len != 0).astype(F32)
    next_ok = ((start + tile) % seq_len != 0).astype(F32)
    zeros8 = jnp.zeros((SUBLANES, D_MODEL), F32)
    h_prev = _modulate(xp_ref[...], mod, 1) * prev_ok
    h_next = _modulate(xn_ref[...], mod, 1) * next_ok
    hext_ref[0:gap, :] = jnp.concatenate([zeros8, h_prev], axis=0).astype(BF16)
    h = _modulate(x, mod, 1).astype(BF16)
    for k in range(n_seg):
        base = gap + k * (seg + gap)
        hext_ref[base:base + seg, :] = h[k * seg:(k + 1) * seg, :]
        if k == n_seg - 1:
            tail = jnp.concatenate([h_next, zeros8], axis=0).astype(BF16)
        else:
            tail = jnp.zeros((gap, D_MODEL), BF16)
        hext_ref[base + seg:base + seg + gap, :] = tail
    acc_ref[...] = jnp.zeros_like(acc_ref)

    def up(cols, ug_ref, uv_ref):
        lo, hi = cols
        ug_ref[:, :hi - lo] = _dot(hext_ref[...], win_ref[:, lo:hi])
        uv_ref[:, :hi - lo] = _dot(hext_ref[...], win_ref[:, D_FF + lo:D_FF + hi])

    def conv(u_ref, b, lo, hi):
        wc = wc_ref[:, lo:hi]
        return (u_ref[b - 1:b - 1 + seg, :hi - lo] * wc[0:1, :] + u_ref[b:b + seg, :hi - lo] * wc[1:2, :]
                + u_ref[b + 1:b + 1 + seg, :hi - lo] * wc[2:3, :] + bc_ref[:, lo:hi])

    def down(cols, ug_ref, uv_ref, act_ref):
        lo, hi = cols
        for k in range(n_seg):
            b = gap + k * (seg + gap)
            a = jax.nn.silu(conv(ug_ref, b, lo, hi)) * conv(uv_ref, b, D_FF + lo, D_FF + hi)
            act_ref[k * seg:(k + 1) * seg, :hi - lo] = a.astype(BF16)
        acc_ref[...] += _dot(act_ref[:, :hi - lo], wout_ref[lo:hi, :])

    chunks = _ffn_chunks()
    u_bufs = ((uga_ref, uva_ref), (ugb_ref, uvb_ref))
    act_bufs = (acta_ref, actb_ref)
    up(chunks[0], *u_bufs[0])
    for c, cols in enumerate(chunks):
        if c + 1 < len(chunks):
            up(chunks[c + 1], *u_bufs[(c + 1) % 2])
        down(cols, *u_bufs[c % 2], act_bufs[c % 2])
    y = x + _gate(mod, 1) * acc_ref[...]
    if final:
        y = _rms(y) * gfin_ref[...]
    o_ref[...] = y


def _conv_ffn(x, mod, seq_len, layer, w_in, w_conv, b_conv, w_out, g_final=None):
    n = x.shape[0]
    tile = FFN_TILE
    seg = min(seq_len, tile)
    n_seg = tile // seg
    rows = BF16_SUBLANES + n_seg * (seg + BF16_SUBLANES)
    per_seq_mod = mod.shape[0] > 1
    final = g_final is not None
    blocks8 = n // SUBLANES
    t8 = tile // SUBLANES

    def resident(a):
        shape = a.shape[1:]
        return pl.BlockSpec((None,) + shape, lambda i: (layer,) + (0,) * len(shape),
                            pipeline_mode=pl.Buffered(1))

    in_specs = [
        pl.BlockSpec((tile, D_MODEL), lambda i: (i, 0)),
        pl.BlockSpec((SUBLANES, D_MODEL), lambda i: (jnp.maximum(i * t8 - 1, 0), 0)),
        pl.BlockSpec((SUBLANES, D_MODEL), lambda i: (jnp.minimum((i + 1) * t8, blocks8 - 1), 0)),
        pl.BlockSpec((None, 1, N_MOD * D_MODEL),
                     (lambda i: ((i * tile) // seq_len, 0, 0)) if per_seq_mod else (lambda i: (0, 0, 0))),
        resident(w_in), resident(w_conv), resident(b_conv), resident(w_out),
    ]
    args = [x, x, x, mod, w_in, w_conv, b_conv, w_out]
    if final:
        in_specs.append(pl.BlockSpec((1, D_MODEL), lambda i: (0, 0)))
        args.append(g_final)
    return pl.pallas_call(
        functools.partial(_ffn_body, seq_len, n_seg, seg, final),
        grid=(n // tile,),
        in_specs=in_specs,
        out_specs=pl.BlockSpec((tile, D_MODEL), lambda i: (i, 0)),
        out_shape=jax.ShapeDtypeStruct((n, D_MODEL), F32),
        scratch_shapes=[pltpu.VMEM((rows, D_MODEL), BF16)]
        + [pltpu.VMEM((rows, FFN_CHUNK), F32)] * 4
        + [pltpu.VMEM((tile, FFN_CHUNK), BF16)] * 2
        + [pltpu.VMEM((tile, D_MODEL), F32)],
        compiler_params=_cparams(1),
        name="conv_ffn",
    )(*args)


def _fnet_body(seq_len, x_ref, mod_ref, csc_ref, csrow_ref, w_ref, o_ref, stack_ref):
    r = pl.program_id(1)
    mod = mod_ref[...]

    @pl.when(r == 0)
    def _():
        def rows(j, carry):
            r0 = pl.multiple_of(j * ROW_BLOCK, ROW_BLOCK)
            h = _modulate(x_ref[pl.ds(r0, ROW_BLOCK), :], mod, 0).astype(BF16)
            for g in range(N_FOURIER_GROUPS):
                cols = slice(g * FOURIER_GROUP, (g + 1) * FOURIER_GROUP)
                y = _dot(h[:, cols], csc_ref[...].astype(BF16))
                stack_ref[pl.ds(r0, ROW_BLOCK), cols] = y[:, :FOURIER_GROUP].astype(BF16)
                stack_ref[pl.ds(seq_len + r0, ROW_BLOCK), cols] = y[:, FOURIER_GROUP:].astype(BF16)
            return carry
        lax.fori_loop(0, seq_len // ROW_BLOCK, rows, 0)

    f = _dot(csrow_ref[...].astype(BF16), stack_ref[...]) * ((seq_len * FOURIER_GROUP) ** -0.5)
    y = _dot(f.astype(BF16), w_ref[...])
    r0 = pl.multiple_of(r * ROW_BLOCK, ROW_BLOCK)
    o_ref[...] = x_ref[pl.ds(r0, ROW_BLOCK), :] + _gate(mod, 0) * y


def _dft_tables(seq_len):
    c = np.arange(FOURIER_GROUP)
    ang_c = 2.0 * np.pi * ((c[:, None] * c[None, :]) % FOURIER_GROUP) / FOURIER_GROUP
    csc = np.concatenate([np.cos(ang_c), np.sin(ang_c)], axis=1)
    t = np.arange(seq_len)
    ang_t = 2.0 * np.pi * ((t[:, None] * t[None, :]) % seq_len) / seq_len
    csrow = np.concatenate([np.cos(ang_t), -np.sin(ang_t)], axis=1)
    return jnp.asarray(csc, F32), jnp.asarray(csrow, F32)


def _fourier_mix(x, mod, seq_len, w):
    n = x.shape[0]
    batch = n // seq_len
    n_row_blocks = seq_len // ROW_BLOCK
    csc, csrow = _dft_tables(seq_len)
    return pl.pallas_call(
        functools.partial(_fnet_body, seq_len),
        grid=(batch, n_row_blocks),
        in_specs=[
            pl.BlockSpec((seq_len, D_MODEL), lambda b, r: (b, 0)),
            pl.BlockSpec((None, 1, N_MOD * D_MODEL),
                         (lambda b, r: (b, 0, 0)) if mod.shape[0] > 1 else (lambda b, r: (0, 0, 0))),
            pl.BlockSpec((FOURIER_GROUP, 2 * FOURIER_GROUP), lambda b, r: (0, 0)),
            pl.BlockSpec((ROW_BLOCK, 2 * seq_len), lambda b, r: (r, 0)),
            pl.BlockSpec((D_MODEL, D_MODEL), lambda b, r: (0, 0)),
        ],
        out_specs=pl.BlockSpec((ROW_BLOCK, D_MODEL), lambda b, r: (b * n_row_blocks + r, 0)),
        out_shape=jax.ShapeDtypeStruct((n, D_MODEL), F32),
        scratch_shapes=[pltpu.VMEM((2 * seq_len, D_MODEL), BF16)],
        compiler_params=_cparams(2),
        name="fourier_mix",
    )(x, mod, csc, csrow, w)


def _pool_body(seq_len, x_ref, mod_ref, w_ref, ps_ref, o_ref, hpad_ref):
    mod = mod_ref[...]
    pad = max(POOL_WINDOWS) // 2
    zeros = jnp.zeros((pad, D_MODEL), F32)
    hpad_ref[0:pad, :] = zeros
    hpad_ref[pad + seq_len:pad + seq_len + pad, :] = zeros
    n_blocks = seq_len // ROW_BLOCK
    for j in range(n_blocks):
        rows = slice(j * ROW_BLOCK, (j + 1) * ROW_BLOCK)
        hpad_ref[pad + j * ROW_BLOCK:pad + (j + 1) * ROW_BLOCK, :] = _modulate(x_ref[rows, :], mod, 0)
    gate = _gate(mod, 0)
    for j in range(n_blocks):
        rows = slice(j * ROW_BLOCK, (j + 1) * ROW_BLOCK)
        base = pad + j * ROW_BLOCK
        t = j * ROW_BLOCK + lax.broadcasted_iota(jnp.int32, (ROW_BLOCK, POOL_GROUP), 0)
        for g, w in enumerate(POOL_WINDOWS):
            cols = slice(g * POOL_GROUP, (g + 1) * POOL_GROUP)
            s = hpad_ref[base - w // 2:base - w // 2 + ROW_BLOCK, cols]
            for off in range(-w // 2 + 1, w // 2):
                s = s + hpad_ref[base + off:base + off + ROW_BLOCK, cols]
            cnt = jnp.minimum(t - w // 2 + w, seq_len) - jnp.maximum(t - w // 2, 0)
            pooled = s / cnt.astype(F32) - hpad_ref[base:base + ROW_BLOCK, cols]
            y = _dot(pooled.astype(BF16), w_ref[g]) * ps_ref[:, cols]
            o_ref[rows, cols] = x_ref[rows, cols] + gate[:, cols] * y


def _pool_mix(x, mod, seq_len, w_pool, pool_scale):
    n = x.shape[0]
    pad = max(POOL_WINDOWS) // 2
    return pl.pallas_call(
        functools.partial(_pool_body, seq_len),
        grid=(n // seq_len,),
        in_specs=[
            pl.BlockSpec((seq_len, D_MODEL), lambda b: (b, 0)),
            pl.BlockSpec((None, 1, N_MOD * D_MODEL),
                         (lambda b: (b, 0, 0)) if mod.shape[0] > 1 else (lambda b: (0, 0, 0))),
            pl.BlockSpec((len(POOL_WINDOWS), POOL_GROUP, POOL_GROUP), lambda b: (0, 0, 0)),
            pl.BlockSpec((1, D_MODEL), lambda b: (0, 0)),
        ],
        out_specs=pl.BlockSpec((seq_len, D_MODEL), lambda b: (b, 0)),
        out_shape=jax.ShapeDtypeStruct((n, D_MODEL), F32),
        scratch_shapes=[pltpu.VMEM((seq_len + 2 * pad, D_MODEL), F32)],
        compiler_params=_cparams(1),
        name="pool_mix",
    )(x, mod, w_pool, pool_scale)


def _rope_pair(a, cs):
    t = a * cs
    return t + pltpu.roll(t, QK_ROPE, 1)


def _store_keys(k_ref, k_nope, k_rope):
    for hd in range(N_HEADS):
        k_ref[:, hd * QK_TILE:hd * QK_TILE + QK_NOPE] = k_nope[:, hd * QK_NOPE:(hd + 1) * QK_NOPE]
        k_ref[:, hd * QK_TILE + QK_NOPE:(hd + 1) * QK_TILE] = k_rope


def _mla_proj_body(rope, state, *refs):
    refs = list(refs)
    x_ref, mod_ref = refs[:2]
    del refs[:2]
    cs_ref = refs.pop(0) if rope else None
    wdq_ref, gq_ref, wuq_ref, wdkv_ref, gkv_ref, wukv_ref = refs[:6]
    del refs[:6]
    q_ref, k_ref, v_ref = refs[:3]
    del refs[:3]
    h = _modulate(x_ref[...], mod_ref[...], 0).astype(BF16)
    cq = (_rms(_dot(h, wdq_ref[...])) * gq_ref[...]).astype(BF16)
    q = _dot(cq, wuq_ref[...]) * SM_SCALE
    n_nope = N_HEADS * QK_NOPE
    for hd in range(N_HEADS):
        a = q[:, n_nope + hd * HEAD_LANES:n_nope + (hd + 1) * HEAD_LANES]
        if rope:
            a = _rope_pair(a, cs_ref[...])
        q_ref[:, hd * QK_TILE:hd * QK_TILE + QK_NOPE] = q[:, hd * QK_NOPE:(hd + 1) * QK_NOPE].astype(BF16)
        q_ref[:, hd * QK_TILE + QK_NOPE:(hd + 1) * QK_TILE] = a.astype(BF16)
    ckv = _dot(h, wdkv_ref[...])
    c = _rms(ckv[:, :KV_RANK]) * gkv_ref[...]
    kp = ckv[:, KV_RANK:]
    if state:
        ckv_out_ref, kpe_out_ref = refs
        ckv_out_ref[...] = c
        kpe_out_ref[...] = kp[:, :QK_ROPE]
    if rope:
        kp = _rope_pair(kp, cs_ref[...])
    lane = lax.broadcasted_iota(jnp.int32, kp.shape, 1)
    kr = jnp.where(lane < QK_ROPE, kp, 0.0).astype(BF16)
    kv = _dot(c.astype(BF16), wukv_ref[...])
    _store_keys(k_ref, kv[:, :n_nope].astype(BF16), kr)
    v_ref[...] = kv[:, n_nope:].astype(BF16)


def _mla_project(x, mod, seq_len, prm, cs=None, state=False):
    n = x.shape[0]
    tile = min(MLA_TILE, seq_len)
    tiles_per_seq = seq_len // tile
    rope = cs is not None
    wide = N_HEADS * QK_NOPE

    def const(shape):
        return pl.BlockSpec(shape, lambda i: (0,) * len(shape))

    def rows(width):
        return pl.BlockSpec((tile, width), lambda i: (i, 0))

    in_specs = [rows(D_MODEL),
                pl.BlockSpec((None, 1, N_MOD * D_MODEL),
                             (lambda i: (i // tiles_per_seq, 0, 0)) if mod.shape[0] > 1
                             else (lambda i: (0, 0, 0)))]
    args = [x, mod]
    if rope:
        in_specs.append(pl.BlockSpec((tile, HEAD_LANES), lambda i: (i % tiles_per_seq, 0)))
        args.append(cs)
    for name in ("w_dq", "g_q", "w_uq", "w_dkv", "g_kv", "w_ukv"):
        in_specs.append(const(prm[name].shape))
        args.append(prm[name])
    out_specs = [rows(N_HEADS * QK_TILE), rows(N_HEADS * QK_TILE), rows(wide)]
    out_shape = [jax.ShapeDtypeStruct((n, N_HEADS * QK_TILE), BF16),
                 jax.ShapeDtypeStruct((n, N_HEADS * QK_TILE), BF16),
                 jax.ShapeDtypeStruct((n, wide), BF16)]
    if state:
        out_specs += [rows(KV_RANK), rows(QK_ROPE)]
        out_shape += [jax.ShapeDtypeStruct((n, KV_RANK), F32), jax.ShapeDtypeStruct((n, QK_ROPE), F32)]
    return pl.pallas_call(
        functools.partial(_mla_proj_body, rope, state),
        grid=(n // tile,),
        in_specs=in_specs,
        out_specs=out_specs,
        out_shape=out_shape,
        compiler_params=_cparams(1),
        name="mla_project",
    )(*args)


def _ctx_expand_body(c_ref, kr_ref, w_ref, k_ref, v_ref):
    kv = _dot(c_ref[...].astype(BF16), w_ref[...])
    n_nope = N_HEADS * QK_NOPE
    _store_keys(k_ref, kv[:, :n_nope].astype(BF16), kr_ref[...])
    v_ref[...] = kv[:, n_nope:].astype(BF16)


def _ctx_expand(c_kv, k_rope, w_ukv):
    n = c_kv.shape[0]
    wide = N_HEADS * QK_NOPE
    return pl.pallas_call(
        _ctx_expand_body,
        grid=(n // ROW_BLOCK,),
        in_specs=[pl.BlockSpec((ROW_BLOCK, KV_RANK), lambda i: (i, 0)),
                  pl.BlockSpec((ROW_BLOCK, HEAD_LANES), lambda i: (i, 0)),
                  pl.BlockSpec(w_ukv.shape, lambda i: (0, 0))],
        out_specs=[pl.BlockSpec((ROW_BLOCK, N_HEADS * QK_TILE), lambda i: (i, 0)),
                   pl.BlockSpec((ROW_BLOCK, wide), lambda i: (i, 0))],
        out_shape=[jax.ShapeDtypeStruct((n, N_HEADS * QK_TILE), BF16), jax.ShapeDtypeStruct((n, wide), BF16)],
        compiler_params=_cparams(1),
        name="mla_ctx_expand",
    )(c_kv, k_rope, w_ukv)


def _attn_body(has_ctx, *refs):
    refs = list(refs)
    q_ref, k_ref, v_ref = refs[:3]
    del refs[:3]
    if has_ctx:
        kc_ref, vc_ref = refs[:2]
        del refs[:2]
    x_ref, mod_ref, wo_ref, o_ref, heads_ref = refs
    for hd in range(N_HEADS):
        qk = slice(hd * QK_TILE, (hd + 1) * QK_TILE)
        vcols = slice(hd * V_HEAD, (hd + 1) * V_HEAD)
        q = q_ref[:, qk]
        s = _dot_nt(q, k_ref[:, qk])
        m = jnp.max(s, axis=-1, keepdims=True)
        if has_ctx:
            sc = _dot_nt(q, kc_ref[:, qk])
            m = jnp.maximum(m, jnp.max(sc, axis=-1, keepdims=True))
        p = jnp.exp(s - m)
        l = jnp.sum(p, axis=-1, keepdims=True)
        o = _dot(p.astype(BF16), v_ref[:, vcols])
        if has_ctx:
            pc = jnp.exp(sc - m)
            l = l + jnp.sum(pc, axis=-1, keepdims=True)
            o = o + _dot(pc.astype(BF16), vc_ref[:, vcols])
        heads_ref[:, vcols] = (o / l).astype(BF16)
    y = _dot(heads_ref[...], wo_ref[...])
    o_ref[...] = x_ref[...] + _gate(mod_ref[...], 0) * y


def _mla_attend(x, mod, seq_len, q, k, v, w_o, ctx=None):
    n = x.shape[0]
    tq = ATT_Q_TILE
    q_tiles = seq_len // tq
    wide = N_HEADS * QK_NOPE
    has_ctx = ctx is not None

    def q_rows(width):
        return pl.BlockSpec((tq, width), lambda b, r: (b * q_tiles + r, 0))

    def seq_rows(length, width):
        return pl.BlockSpec((length, width), lambda b, r: (b, 0))

    in_specs = [q_rows(N_HEADS * QK_TILE), seq_rows(seq_len, N_HEADS * QK_TILE), seq_rows(seq_len, wide)]
    args = [q, k, v]
    if has_ctx:
        kc, vc = ctx
        past = kc.shape[0] // (n // seq_len)
        in_specs += [seq_rows(past, N_HEADS * QK_TILE), seq_rows(past, wide)]
        args += [kc, vc]
    in_specs += [q_rows(D_MODEL),
                 pl.BlockSpec((None, 1, N_MOD * D_MODEL),
                              (lambda b, r: (b, 0, 0)) if mod.shape[0] > 1 else (lambda b, r: (0, 0, 0))),
                 pl.BlockSpec(w_o.shape, lambda b, r: (0, 0))]
    args += [x, mod, w_o]
    return pl.pallas_call(
        functools.partial(_attn_body, has_ctx),
        grid=(n // seq_len, q_tiles),
        in_specs=in_specs,
        out_specs=q_rows(D_MODEL),
        out_shape=jax.ShapeDtypeStruct((n, D_MODEL), F32),
        scratch_shapes=[pltpu.VMEM((tq, wide), BF16)],
        compiler_params=_cparams(2),
        name="mla_attend",
    )(*args)


def _rope_table(n_tok):
    rows = n_tok // GRID_W
    row = np.repeat(np.arange(rows), GRID_W).astype(np.float32)
    col = np.tile(np.arange(GRID_W), rows).astype(np.float32)
    inv = (1.0 / (ROPE_BASE ** (np.arange(0, ROPE_AXIS, 2, dtype=np.float32) / ROPE_AXIS))).astype(np.float32)
    ang_r = row[:, None] * inv
    ang_c = col[:, None] * inv
    cos = np.concatenate([np.cos(ang_r), np.cos(ang_r), np.cos(ang_c), np.cos(ang_c)], axis=1)
    sin = np.concatenate([np.sin(ang_r), np.sin(ang_r), np.sin(ang_c), np.sin(ang_c)], axis=1)
    return jnp.asarray(np.concatenate([cos, sin], axis=1), F32)


def _swap_cols(w):
    half = ROPE_AXIS // 2
    return jnp.concatenate([-w[..., half:2 * half], w[..., 0:half],
                            -w[..., 3 * half:4 * half], w[..., 2 * half:3 * half]], axis=-1)


def _mla_weights(w_dq, g_q, w_uq, w_dkv, g_kv, w_ukv, w_o):
    wq = w_uq.reshape(Q_RANK, N_HEADS, QK_NOPE + QK_ROPE)
    wq_pe = wq[:, :, QK_NOPE:]
    wq_pairs = jnp.concatenate([wq_pe, _swap_cols(wq_pe)], axis=-1)
    w_uq2 = jnp.concatenate([wq[:, :, :QK_NOPE].reshape(Q_RANK, -1), wq_pairs.reshape(Q_RANK, -1)], axis=1)
    w_pe = w_dkv[:, KV_RANK:]
    w_dkv2 = jnp.concatenate([w_dkv[:, :KV_RANK], w_pe, _swap_cols(w_pe)], axis=1)
    wkv = w_ukv.reshape(KV_RANK, N_HEADS, QK_NOPE + V_HEAD)
    w_ukv2 = jnp.concatenate([wkv[:, :, :QK_NOPE].reshape(KV_RANK, -1),
                              wkv[:, :, QK_NOPE:].reshape(KV_RANK, -1)], axis=1)
    return {"w_dq": w_dq.astype(BF16), "g_q": g_q.reshape(1, Q_RANK), "w_uq": w_uq2.astype(BF16),
            "w_dkv": w_dkv2.astype(BF16), "g_kv": g_kv.reshape(1, KV_RANK), "w_ukv": w_ukv2.astype(BF16),
            "w_o": w_o.astype(BF16)}


def _cast_body(a_ref, b_ref, ao_ref, bo_ref):
    ao_ref[...] = a_ref[...].astype(BF16)
    bo_ref[...] = b_ref[...].astype(BF16)


def _ffn_weights(w_in, w_conv, b_conv, w_out):
    steps = D_FF // ROW_BLOCK
    cols_in = 2 * D_FF // steps
    w_in2, w_out2 = pl.pallas_call(
        _cast_body,
        grid=(DEPTH, steps),
        in_specs=[pl.BlockSpec((None, D_MODEL, cols_in), lambda l, c: (l, 0, c)),
                  pl.BlockSpec((None, ROW_BLOCK, D_MODEL), lambda l, c: (l, c, 0))],
        out_specs=[pl.BlockSpec((None, D_MODEL, cols_in), lambda l, c: (l, 0, c)),
                   pl.BlockSpec((None, ROW_BLOCK, D_MODEL), lambda l, c: (l, c, 0))],
        out_shape=[jax.ShapeDtypeStruct(w_in.shape, BF16), jax.ShapeDtypeStruct(w_out.shape, BF16)],
        compiler_params=_cparams(2),
        name="ffn_weight_cast",
    )(w_in, w_out)
    return w_in2, w_conv, b_conv[:, None, :], w_out2


def kernel(x_prompt, x_sample, cache_ckv, cache_kpe, c, c_ctx, w_ada, b_ada, w_fnet, w_pool, pool_scale,
           w_dq, g_q, w_uq, w_dkv, g_kv, w_ukv, w_o, w_ffn_in, w_ffn_conv, b_ffn_conv, w_ffn_out, g_final):
    batch, seq, _ = x_prompt.shape
    dec_batch, dec_seq, _ = x_sample.shape
    past = cache_ckv.shape[2]

    cond = jnp.concatenate([c_ctx[None, :], c, jnp.zeros((SUBLANES - 1 - dec_batch, D_MODEL), F32)], axis=0)
    mods = _ada_all_layers(cond, w_ada, b_ada)
    ffn_w = _ffn_weights(w_ffn_in, w_ffn_conv, b_ffn_conv, w_ffn_out)

    xp = x_prompt.reshape(batch * seq, D_MODEL)
    xs = x_sample.reshape(dec_batch * dec_seq, D_MODEL)
    states = []
    j_f = j_p = j_a = 0
    for i in range(DEPTH):
        mod_p = mods[i, 0:1].reshape(1, 1, N_MOD * D_MODEL)
        mod_s = mods[i, 1:1 + dec_batch].reshape(dec_batch, 1, N_MOD * D_MODEL)
        kind = i % N_MIXERS
        if kind == 0:
            w = w_fnet[j_f].astype(BF16)
            xp = _fourier_mix(xp, mod_p, seq, w)
            xs = _fourier_mix(xs, mod_s, dec_seq, w)
            j_f += 1
        elif kind == 1:
            w = w_pool[j_p].astype(BF16)
            ps = pool_scale[j_p].reshape(1, D_MODEL)
            xp = _pool_mix(xp, mod_p, seq, w, ps)
            xs = _pool_mix(xs, mod_s, dec_seq, w, ps)
            j_p += 1
        else:
            prm = _mla_weights(w_dq[j_a], g_q[j_a], w_uq[j_a], w_dkv[j_a], g_kv[j_a], w_ukv[j_a], w_o[j_a])
            q, k, v, ckv_p, kpe_p = _mla_project(xp, mod_p, seq, prm, state=True)
            xp = _mla_attend(xp, mod_p, seq, q, k, v, prm["w_o"])
            states.append((ckv_p.reshape(batch, seq, KV_RANK), kpe_p.reshape(batch, seq, QK_ROPE)))
            q, k, v = _mla_project(xs, mod_s, dec_seq, prm, cs=_rope_table(dec_seq))
            kpe_c = cache_kpe[:, j_a].reshape(dec_batch * past, QK_ROPE)
            krc = jnp.concatenate([kpe_c, jnp.zeros_like(kpe_c)], axis=1).astype(BF16)
            kc, vc = _ctx_expand(cache_ckv[:, j_a].reshape(dec_batch * past, KV_RANK), krc, prm["w_ukv"])
            xs = _mla_attend(xs, mod_s, dec_seq, q, k, v, prm["w_o"], ctx=(kc, vc))
            j_a += 1
        g_fin = g_final.reshape(1, D_MODEL) if i == DEPTH - 1 else None
        xp = _conv_ffn(xp, mod_p, seq, i, *ffn_w, g_final=g_fin)
        xs = _conv_ffn(xs, mod_s, dec_seq, i, *ffn_w, g_final=g_fin)

    state_ckv = jnp.stack([s[0] for s in states], axis=1)
    state_kpe = jnp.stack([s[1] for s in states], axis=1)
    return (xp.reshape(batch, seq, D_MODEL), xs.reshape(dec_batch, dec_seq, D_MODEL), state_ckv, state_kpe)
```

```python
import functools

import numpy as np
import jax
import jax.numpy as jnp
from jax import lax
from jax.experimental import pallas as pl
from jax.experimental.pallas import tpu as pltpu

F32 = jnp.float32
BF16 = jnp.bfloat16

D_MODEL = 1024
DEPTH = 4
GRID_W = 64
N_MIXERS = 3
N_FOURIER_GROUPS = 4
FOURIER_GROUP = D_MODEL // N_FOURIER_GROUPS
POOL_WINDOWS = (2, 4, 8, 16)
POOL_GROUP = D_MODEL // len(POOL_WINDOWS)
N_HEADS = 8
QK_NOPE = 128
QK_ROPE = 64
V_HEAD = 128
Q_RANK = 768
KV_RANK = 512
ROPE_AXIS = QK_ROPE // 2
ROPE_BASE = 10000.0
SM_SCALE = (QK_NOPE + QK_ROPE) ** -0.5
D_FF = 2816
N_MOD = 6
EPS = 1e-6

SUBLANES = 8
LANES = 128
BF16_SUBLANES = 16
VMEM_LIMIT_BYTES = 56 * 1024 * 1024

ROW_BLOCK = 256
FFN_CHUNK = 512
FFN_TILE = 512
MLA_TILE = 512
ATT_Q_TILE = 256
ADA_COLS = 2048
HEAD_LANES = 2 * QK_ROPE
QK_TILE = QK_NOPE + HEAD_LANES


def _cparams(n_axes):
    return pltpu.CompilerParams(dimension_semantics=("arbitrary",) * n_axes,
                                vmem_limit_bytes=VMEM_LIMIT_BYTES)


def _rms(x):
    return x * lax.rsqrt(jnp.mean(x * x, axis=-1, keepdims=True) + EPS)


def _modulate(x, mod, sub):
    shift = mod[:, (3 * sub) * D_MODEL:(3 * sub + 1) * D_MODEL]
    scale = mod[:, (3 * sub + 1) * D_MODEL:(3 * sub + 2) * D_MODEL]
    return _rms(x) * (1.0 + scale) + shift


def _gate(mod, sub):
    return mod[:, (3 * sub + 2) * D_MODEL:(3 * sub + 3) * D_MODEL]


def _dot(a, b):
    return jnp.dot(a, b, preferred_element_type=F32)


def _dot_nt(a, b):
    return lax.dot_general(a, b, (((1,), (1,)), ((), ())), preferred_element_type=F32)


def _ada_body(cond_ref, w_ref, b_ref, o_ref):
    a = jax.nn.silu(cond_ref[...]).astype(BF16)
    o_ref[...] = _dot(a, w_ref[...].astype(BF16)) + b_ref[...]


def _ada_all_layers(cond, w_ada, b_ada):
    n_out = N_MOD * D_MODEL
    return pl.pallas_call(
        _ada_body,
        grid=(DEPTH, n_out // ADA_COLS),
        in_specs=[
            pl.BlockSpec((SUBLANES, D_MODEL), lambda l, j: (0, 0)),
            pl.BlockSpec((None, D_MODEL, ADA_COLS), lambda l, j: (l, 0, j)),
            pl.BlockSpec((None, 1, ADA_COLS), lambda l, j: (l, 0, j)),
        ],
        out_specs=pl.BlockSpec((None, SUBLANES, ADA_COLS), lambda l, j: (l, 0, j)),
        out_shape=jax.ShapeDtypeStruct((DEPTH, SUBLANES, n_out), F32),
        compiler_params=_cparams(2),
        name="ada_mod",
    )(cond, w_ada, b_ada.reshape(DEPTH, 1, n_out))


def _ffn_chunks():
    edges = list(range(0, D_FF, FFN_CHUNK)) + [D_FF]
    return list(zip(edges[:-1], edges[1:]))


def _ffn_body(seq_len, n_seg, seg, final, cast_next, *refs):
    refs = list(refs)
    x_ref, xp_ref, xn_ref, mod_ref, win_ref, wc_ref, bc_ref, wout_ref = refs[:8]
    del refs[:8]
    gfin_ref = refs.pop(0) if final else None
    if cast_next:
        next_in_ref, next_out_ref, o_ref, next_in_bf_ref, next_out_bf_ref = refs[:5]
        del refs[:5]
        next_in_bf_ref[...] = next_in_ref[...].astype(BF16)
        next_out_bf_ref[...] = next_out_ref[...].astype(BF16)
    else:
        o_ref = refs.pop(0)
    hext_ref, uga_ref, uva_ref, ugb_ref, uvb_ref, acta_ref, actb_ref, acc_ref = refs
    tile = n_seg * seg
    gap = BF16_SUBLANES
    i = pl.program_id(0)
    mod = mod_ref[...]
    x = x_ref[...]

    start = i * tile
    prev_ok = (start % seq_len != 0).astype(F32)
    next_ok = ((start + tile) % seq_len != 0).astype(F32)
    zeros8 = jnp.zeros((SUBLANES, D_MODEL), F32)
    h_prev = _modulate(xp_ref[...], mod, 1) * prev_ok
    h_next = _modulate(xn_ref[...], mod, 1) * next_ok
    hext_ref[0:gap, :] = jnp.concatenate([zeros8, h_prev], axis=0).astype(BF16)
    h = _modulate(x, mod, 1).astype(BF16)
    for k in range(n_seg):
        base = gap + k * (seg + gap)
        hext_ref[base:base + seg, :] = h[k * seg:(k + 1) * seg, :]
        if k == n_seg - 1:
            tail = jnp.concatenate([h_next, zeros8], axis=0).astype(BF16)
        else:
            tail = jnp.zeros((gap, D_MODEL), BF16)
        hext_ref[base + seg:base + seg + gap, :] = tail
    acc_ref[...] = jnp.zeros_like(acc_ref)

    def up(cols, ug_ref, uv_ref):
        lo, hi = cols
        ug_ref[:, :hi - lo] = _dot(hext_ref[...], win_ref[:, lo:hi])
        uv_ref[:, :hi - lo] = _dot(hext_ref[...], win_ref[:, D_FF + lo:D_FF + hi])

    def conv(u_ref, b, lo, hi):
        wc = wc_ref[:, lo:hi]
        return (u_ref[b - 1:b - 1 + seg, :hi - lo] * wc[0:1, :] + u_ref[b:b + seg, :hi - lo] * wc[1:2, :]
                + u_ref[b + 1:b + 1 + seg, :hi - lo] * wc[2:3, :] + bc_ref[:, lo:hi])

    def down(cols, ug_ref, uv_ref, act_ref):
        lo, hi = cols
        for k in range(n_seg):
            b = gap + k * (seg + gap)
            a = jax.nn.silu(conv(ug_ref, b, lo, hi)) * conv(uv_ref, b, D_FF + lo, D_FF + hi)
            act_ref[k * seg:(k + 1) * seg, :hi - lo] = a.astype(BF16)
        acc_ref[...] += _dot(act_ref[:, :hi - lo], wout_ref[lo:hi, :])

    chunks = _ffn_chunks()
    u_bufs = ((uga_ref, uva_ref), (ugb_ref, uvb_ref))
    act_bufs = (acta_ref, actb_ref)
    up(chunks[0], *u_bufs[0])
    for c, cols in enumerate(chunks):
        if c + 1 < len(chunks):
            up(chunks[c + 1], *u_bufs[(c + 1) % 2])
        down(cols, *u_bufs[c % 2], act_bufs[c % 2])
    y = x + _gate(mod, 1) * acc_ref[...]
    if final:
        y = _rms(y) * gfin_ref[...]
    o_ref[...] = y


def _conv_ffn(x, mod, seq_len, layer, w_in, w_conv, b_conv, w_out, g_final=None, cast_next=None):
    n = x.shape[0]
    tile = FFN_TILE
    seg = min(seq_len, tile)
    n_seg = tile // seg
    rows = BF16_SUBLANES + n_seg * (seg + BF16_SUBLANES)
    per_seq_mod = mod.shape[0] > 1
    final = g_final is not None
    blocks8 = n // SUBLANES
    t8 = tile // SUBLANES

    def resident(a):
        return pl.BlockSpec(a.shape, lambda i: (0,) * a.ndim, pipeline_mode=pl.Buffered(1))

    def resident_layer(a):
        shape = a.shape[1:]
        return pl.BlockSpec((None,) + shape, lambda i: (layer,) + (0,) * len(shape),
                            pipeline_mode=pl.Buffered(1))

    in_specs = [
        pl.BlockSpec((tile, D_MODEL), lambda i: (i, 0)),
        pl.BlockSpec((SUBLANES, D_MODEL), lambda i: (jnp.maximum(i * t8 - 1, 0), 0)),
        pl.BlockSpec((SUBLANES, D_MODEL), lambda i: (jnp.minimum((i + 1) * t8, blocks8 - 1), 0)),
        pl.BlockSpec((None, 1, N_MOD * D_MODEL),
                     (lambda i: ((i * tile) // seq_len, 0, 0)) if per_seq_mod else (lambda i: (0, 0, 0))),
        resident(w_in), resident_layer(w_conv), resident_layer(b_conv), resident(w_out),
    ]
    args = [x, x, x, mod, w_in, w_conv, b_conv, w_out]
    if final:
        in_specs.append(pl.BlockSpec((1, D_MODEL), lambda i: (0, 0)))
        args.append(g_final)
    out_specs = [pl.BlockSpec((tile, D_MODEL), lambda i: (i, 0))]
    out_shape = [jax.ShapeDtypeStruct((n, D_MODEL), F32)]
    if cast_next is not None:
        steps = n // tile
        for a in cast_next:
            rows_step = a.shape[1] // steps
            assert rows_step * steps == a.shape[1] and rows_step % BF16_SUBLANES == 0
            in_specs.append(pl.BlockSpec((None, rows_step, a.shape[2]), lambda i: (layer + 1, i, 0)))
            out_specs.append(pl.BlockSpec((rows_step, a.shape[2]), lambda i: (i, 0)))
            out_shape.append(jax.ShapeDtypeStruct(a.shape[1:], BF16))
            args.append(a)
    return pl.pallas_call(
        functools.partial(_ffn_body, seq_len, n_seg, seg, final, cast_next is not None),
        grid=(n // tile,),
        in_specs=in_specs,
        out_specs=out_specs,
        out_shape=out_shape,
        scratch_shapes=[pltpu.VMEM((rows, D_MODEL), BF16)]
        + [pltpu.VMEM((rows, FFN_CHUNK), F32)] * 4
        + [pltpu.VMEM((tile, FFN_CHUNK), BF16)] * 2
        + [pltpu.VMEM((tile, D_MODEL), F32)],
        compiler_params=_cparams(1),
        name="conv_ffn",
    )(*args)


def _fnet_body(seq_len, x_ref, mod_ref, csc_ref, csrow_ref, w_ref, o_ref, stack_ref):
    r = pl.program_id(1)
    mod = mod_ref[...]

    @pl.when(r == 0)
    def _():
        def rows(j, carry):
            r0 = pl.multiple_of(j * ROW_BLOCK, ROW_BLOCK)
            h = _modulate(x_ref[pl.ds(r0, ROW_BLOCK), :], mod, 0).astype(BF16)
            for g in range(N_FOURIER_GROUPS):
                cols = slice(g * FOURIER_GROUP, (g + 1) * FOURIER_GROUP)
                y = _dot(h[:, cols], csc_ref[...].astype(BF16))
                stack_ref[pl.ds(r0, ROW_BLOCK), cols] = y[:, :FOURIER_GROUP].astype(BF16)
                stack_ref[pl.ds(seq_len + r0, ROW_BLOCK), cols] = y[:, FOURIER_GROUP:].astype(BF16)
            return carry
        lax.fori_loop(0, seq_len // ROW_BLOCK, rows, 0)

    f = _dot(csrow_ref[...].astype(BF16), stack_ref[...]) * ((seq_len * FOURIER_GROUP) ** -0.5)
    y = _dot(f.astype(BF16), w_ref[...])
    r0 = pl.multiple_of(r * ROW_BLOCK, ROW_BLOCK)
    o_ref[...] = x_ref[pl.ds(r0, ROW_BLOCK), :] + _gate(mod, 0) * y


def _dft_tables(seq_len):
    c = np.arange(FOURIER_GROUP)
    ang_c = 2.0 * np.pi * ((c[:, None] * c[None, :]) % FOURIER_GROUP) / FOURIER_GROUP
    csc = np.concatenate([np.cos(ang_c), np.sin(ang_c)], axis=1)
    t = np.arange(seq_len)
    ang_t = 2.0 * np.pi * ((t[:, None] * t[None, :]) % seq_len) / seq_len
    csrow = np.concatenate([np.cos(ang_t), -np.sin(ang_t)], axis=1)
    return jnp.asarray(csc, F32), jnp.asarray(csrow, F32)


def _fourier_mix(x, mod, seq_len, w):
    n = x.shape[0]
    batch = n // seq_len
    n_row_blocks = seq_len // ROW_BLOCK
    csc, csrow = _dft_tables(seq_len)
    return pl.pallas_call(
        functools.partial(_fnet_body, seq_len),
        grid=(batch, n_row_blocks),
        in_specs=[
            pl.BlockSpec((seq_len, D_MODEL), lambda b, r: (b, 0)),
            pl.BlockSpec((None, 1, N_MOD * D_MODEL),
                         (lambda b, r: (b, 0, 0)) if mod.shape[0] > 1 else (lambda b, r: (0, 0, 0))),
            pl.BlockSpec((FOURIER_GROUP, 2 * FOURIER_GROUP), lambda b, r: (0, 0)),
            pl.BlockSpec((ROW_BLOCK, 2 * seq_len), lambda b, r: (r, 0)),
            pl.BlockSpec((D_MODEL, D_MODEL), lambda b, r: (0, 0)),
        ],
        out_specs=pl.BlockSpec((ROW_BLOCK, D_MODEL), lambda b, r: (b * n_row_blocks + r, 0)),
        out_shape=jax.ShapeDtypeStruct((n, D_MODEL), F32),
        scratch_shapes=[pltpu.VMEM((2 * seq_len, D_MODEL), BF16)],
        compiler_params=_cparams(2),
        name="fourier_mix",
    )(x, mod, csc, csrow, w)


def _pool_body(seq_len, x_ref, mod_ref, w_ref, ps_ref, o_ref, hpad_ref):
    mod = mod_ref[...]
    pad = max(POOL_WINDOWS) // 2
    zeros = jnp.zeros((pad, D_MODEL), F32)
    hpad_ref[0:pad, :] = zeros
    hpad_ref[pad + seq_len:pad + seq_len + pad, :] = zeros
    n_blocks = seq_len // ROW_BLOCK
    for j in range(n_blocks):
        rows = slice(j * ROW_BLOCK, (j + 1) * ROW_BLOCK)
        hpad_ref[pad + j * ROW_BLOCK:pad + (j + 1) * ROW_BLOCK, :] = _modulate(x_ref[rows, :], mod, 0)
    gate = _gate(mod, 0)
    for j in range(n_blocks):
        rows = slice(j * ROW_BLOCK, (j + 1) * ROW_BLOCK)
        base = pad + j * ROW_BLOCK
        t = j * ROW_BLOCK + lax.broadcasted_iota(jnp.int32, (ROW_BLOCK, POOL_GROUP), 0)
        for g, w in enumerate(POOL_WINDOWS):
            cols = slice(g * POOL_GROUP, (g + 1) * POOL_GROUP)
            s = hpad_ref[base - w // 2:base - w // 2 + ROW_BLOCK, cols]
            for off in range(-w // 2 + 1, w // 2):
                s = s + hpad_ref[base + off:base + off + ROW_BLOCK, cols]
            cnt = jnp.minimum(t - w // 2 + w, seq_len) - jnp.maximum(t - w // 2, 0)
            pooled = s / cnt.astype(F32) - hpad_ref[base:base + ROW_BLOCK, cols]
            y = _dot(pooled.astype(BF16), w_ref[g]) * ps_ref[:, cols]
            o_ref[rows, cols] = x_ref[rows, cols] + gate[:, cols] * y


def _pool_mix(x, mod, seq_len, w_pool, pool_scale):
    n = x.shape[0]
    pad = max(POOL_WINDOWS) // 2
    return pl.pallas_call(
        functools.partial(_pool_body, seq_len),
        grid=(n // seq_len,),
        in_specs=[
            pl.BlockSpec((seq_len, D_MODEL), lambda b: (b, 0)),
            pl.BlockSpec((None, 1, N_MOD * D_MODEL),
                         (lambda b: (b, 0, 0)) if mod.shape[0] > 1 else (lambda b: (0, 0, 0))),
            pl.BlockSpec((len(POOL_WINDOWS), POOL_GROUP, POOL_GROUP), lambda b: (0, 0, 0)),
            pl.BlockSpec((1, D_MODEL), lambda b: (0, 0)),
        ],
        out_specs=pl.BlockSpec((seq_len, D_MODEL), lambda b: (b, 0)),
        out_shape=jax.ShapeDtypeStruct((n, D_MODEL), F32),
        scratch_shapes=[pltpu.VMEM((seq_len + 2 * pad, D_MODEL), F32)],
        compiler_params=_cparams(1),
        name="pool_mix",
    )(x, mod, w_pool, pool_scale)


def _rope_pair(a, cs):
    t = a * cs
    return t + pltpu.roll(t, QK_ROPE, 1)


def _store_keys(k_ref, k_nope, k_rope):
    for hd in range(N_HEADS):
        k_ref[:, hd * QK_TILE:hd * QK_TILE + QK_NOPE] = k_nope[:, hd * QK_NOPE:(hd + 1) * QK_NOPE]
        k_ref[:, hd * QK_TILE + QK_NOPE:(hd + 1) * QK_TILE] = k_rope


def _mla_proj_body(rope, state, *refs):
    refs = list(refs)
    x_ref, mod_ref = refs[:2]
    del refs[:2]
    cs_ref = refs.pop(0) if rope else None
    wdq_ref, gq_ref, wuq_ref, wdkv_ref, gkv_ref, wukv_ref = refs[:6]
    del refs[:6]
    q_ref, k_ref, v_ref = refs[:3]
    del refs[:3]
    h = _modulate(x_ref[...], mod_ref[...], 0).astype(BF16)
    cq = (_rms(_dot(h, wdq_ref[...])) * gq_ref[...]).astype(BF16)
    q = _dot(cq, wuq_ref[...]) * SM_SCALE
    n_nope = N_HEADS * QK_NOPE
    for hd in range(N_HEADS):
        a = q[:, n_nope + hd * HEAD_LANES:n_nope + (hd + 1) * HEAD_LANES]
        if rope:
            a = _rope_pair(a, cs_ref[...])
        q_ref[:, hd * QK_TILE:hd * QK_TILE + QK_NOPE] = q[:, hd * QK_NOPE:(hd + 1) * QK_NOPE].astype(BF16)
        q_ref[:, hd * QK_TILE + QK_NOPE:(hd + 1) * QK_TILE] = a.astype(BF16)
    ckv = _dot(h, wdkv_ref[...])
    c = _rms(ckv[:, :KV_RANK]) * gkv_ref[...]
    kp = ckv[:, KV_RANK:]
    if state:
        ckv_out_ref, kpe_out_ref = refs
        ckv_out_ref[...] = c
        kpe_out_ref[...] = kp[:, :QK_ROPE]
    if rope:
        kp = _rope_pair(kp, cs_ref[...])
    lane = lax.broadcasted_iota(jnp.int32, kp.shape, 1)
    kr = jnp.where(lane < QK_ROPE, kp, 0.0).astype(BF16)
    kv = _dot(c.astype(BF16), wukv_ref[...])
    _store_keys(k_ref, kv[:, :n_nope].astype(BF16), kr)
    v_ref[...] = kv[:, n_nope:].astype(BF16)


def _mla_project(x, mod, seq_len, prm, cs=None, state=False):
    n = x.shape[0]
    rope = cs is not None
    tile = min(MLA_TILE, seq_len) if (rope or mod.shape[0] > 1) else MLA_TILE
    tiles_per_seq = max(seq_len // tile, 1)
    wide = N_HEADS * QK_NOPE

    def const(shape):
        return pl.BlockSpec(shape, lambda i: (0,) * len(shape))

    def rows(width):
        return pl.BlockSpec((tile, width), lambda i: (i, 0))

    in_specs = [rows(D_MODEL),
                pl.BlockSpec((None, 1, N_MOD * D_MODEL),
                             (lambda i: (i // tiles_per_seq, 0, 0)) if mod.shape[0] > 1
                             else (lambda i: (0, 0, 0)))]
    args = [x, mod]
    if rope:
        in_specs.append(pl.BlockSpec((tile, HEAD_LANES), lambda i: (i % tiles_per_seq, 0)))
        args.append(cs)
    for name in ("w_dq", "g_q", "w_uq", "w_dkv", "g_kv", "w_ukv"):
        in_specs.append(const(prm[name].shape))
        args.append(prm[name])
    out_specs = [rows(N_HEADS * QK_TILE), rows(N_HEADS * QK_TILE), rows(wide)]
    out_shape = [jax.ShapeDtypeStruct((n, N_HEADS * QK_TILE), BF16),
                 jax.ShapeDtypeStruct((n, N_HEADS * QK_TILE), BF16),
                 jax.ShapeDtypeStruct((n, wide), BF16)]
    if state:
        out_specs += [rows(KV_RANK), rows(QK_ROPE)]
        out_shape += [jax.ShapeDtypeStruct((n, KV_RANK), F32), jax.ShapeDtypeStruct((n, QK_ROPE), F32)]
    return pl.pallas_call(
        functools.partial(_mla_proj_body, rope, state),
        grid=(n // tile,),
        in_specs=in_specs,
        out_specs=out_specs,
        out_shape=out_shape,
        compiler_params=_cparams(1),
        name="mla_project",
    )(*args)


def _ctx_expand_body(c_ref, kr_ref, w_ref, k_ref, v_ref):
    kv = _dot(c_ref[...].astype(BF16), w_ref[...])
    n_nope = N_HEADS * QK_NOPE
    _store_keys(k_ref, kv[:, :n_nope].astype(BF16), kr_ref[...])
    v_ref[...] = kv[:, n_nope:].astype(BF16)


def _ctx_expand(c_kv, k_rope, w_ukv):
    n = c_kv.shape[0]
    wide = N_HEADS * QK_NOPE
    return pl.pallas_call(
        _ctx_expand_body,
        grid=(n // ROW_BLOCK,),
        in_specs=[pl.BlockSpec((ROW_BLOCK, KV_RANK), lambda i: (i, 0)),
                  pl.BlockSpec((ROW_BLOCK, HEAD_LANES), lambda i: (i, 0)),
                  pl.BlockSpec(w_ukv.shape, lambda i: (0, 0))],
        out_specs=[pl.BlockSpec((ROW_BLOCK, N_HEADS * QK_TILE), lambda i: (i, 0)),
                   pl.BlockSpec((ROW_BLOCK, wide), lambda i: (i, 0))],
        out_shape=[jax.ShapeDtypeStruct((n, N_HEADS * QK_TILE), BF16), jax.ShapeDtypeStruct((n, wide), BF16)],
        compiler_params=_cparams(1),
        name="mla_ctx_expand",
    )(c_kv, k_rope, w_ukv)


def _attn_body(has_ctx, *refs):
    refs = list(refs)
    q_ref, k_ref, v_ref = refs[:3]
    del refs[:3]
    if has_ctx:
        kc_ref, vc_ref = refs[:2]
        del refs[:2]
    x_ref, mod_ref, wo_ref, o_ref, heads_ref = refs
    for hd in range(N_HEADS):
        qk = slice(hd * QK_TILE, (hd + 1) * QK_TILE)
        vcols = slice(hd * V_HEAD, (hd + 1) * V_HEAD)
        q = q_ref[:, qk]
        s = _dot_nt(q, k_ref[:, qk])
        m = jnp.max(s, axis=-1, keepdims=True)
        if has_ctx:
            sc = _dot_nt(q, kc_ref[:, qk])
            m = jnp.maximum(m, jnp.max(sc, axis=-1, keepdims=True))
        p = jnp.exp(s - m)
        l = jnp.sum(p, axis=-1, keepdims=True)
        o = _dot(p.astype(BF16), v_ref[:, vcols])
        if has_ctx:
            pc = jnp.exp(sc - m)
            l = l + jnp.sum(pc, axis=-1, keepdims=True)
            o = o + _dot(pc.astype(BF16), vc_ref[:, vcols])
        heads_ref[:, vcols] = (o / l).astype(BF16)
    y = _dot(heads_ref[...], wo_ref[...])
    o_ref[...] = x_ref[...] + _gate(mod_ref[...], 0) * y


def _mla_attend(x, mod, seq_len, q, k, v, w_o, ctx=None):
    n = x.shape[0]
    tq = min(ATT_Q_TILE, seq_len)
    q_tiles = seq_len // tq
    wide = N_HEADS * QK_NOPE
    has_ctx = ctx is not None

    def q_rows(width):
        return pl.BlockSpec((tq, width), lambda b, r: (b * q_tiles + r, 0))

    def seq_rows(length, width):
        return pl.BlockSpec((length, width), lambda b, r: (b, 0))

    in_specs = [q_rows(N_HEADS * QK_TILE), seq_rows(seq_len, N_HEADS * QK_TILE), seq_rows(seq_len, wide)]
    args = [q, k, v]
    if has_ctx:
        kc, vc = ctx
        past = kc.shape[0] // (n // seq_len)
        in_specs += [seq_rows(past, N_HEADS * QK_TILE), seq_rows(past, wide)]
        args += [kc, vc]
    in_specs += [q_rows(D_MODEL),
                 pl.BlockSpec((None, 1, N_MOD * D_MODEL),
                              (lambda b, r: (b, 0, 0)) if mod.shape[0] > 1 else (lambda b, r: (0, 0, 0))),
                 pl.BlockSpec(w_o.shape, lambda b, r: (0, 0))]
    args += [x, mod, w_o]
    return pl.pallas_call(
        functools.partial(_attn_body, has_ctx),
        grid=(n // seq_len, q_tiles),
        in_specs=in_specs,
        out_specs=q_rows(D_MODEL),
        out_shape=jax.ShapeDtypeStruct((n, D_MODEL), F32),
        scratch_shapes=[pltpu.VMEM((tq, wide), BF16)],
        compiler_params=_cparams(2),
        name="mla_attend",
    )(*args)


def _rope_table(n_tok):
    rows = n_tok // GRID_W
    row = np.repeat(np.arange(rows), GRID_W).astype(np.float32)
    col = np.tile(np.arange(GRID_W), rows).astype(np.float32)
    inv = (1.0 / (ROPE_BASE ** (np.arange(0, ROPE_AXIS, 2, dtype=np.float32) / ROPE_AXIS))).astype(np.float32)
    ang_r = row[:, None] * inv
    ang_c = col[:, None] * inv
    cos = np.concatenate([np.cos(ang_r), np.cos(ang_r), np.cos(ang_c), np.cos(ang_c)], axis=1)
    sin = np.concatenate([np.sin(ang_r), np.sin(ang_r), np.sin(ang_c), np.sin(ang_c)], axis=1)
    return jnp.asarray(np.concatenate([cos, sin], axis=1), F32)


def _swap_cols(w):
    half = ROPE_AXIS // 2
    return jnp.concatenate([-w[..., half:2 * half], w[..., 0:half],
                            -w[..., 3 * half:4 * half], w[..., 2 * half:3 * half]], axis=-1)


def _mla_weights(w_dq, g_q, w_uq, w_dkv, g_kv, w_ukv, w_o):
    wq = w_uq.reshape(Q_RANK, N_HEADS, QK_NOPE + QK_ROPE)
    wq_pe = wq[:, :, QK_NOPE:]
    wq_pairs = jnp.concatenate([wq_pe, _swap_cols(wq_pe)], axis=-1)
    w_uq2 = jnp.concatenate([wq[:, :, :QK_NOPE].reshape(Q_RANK, -1), wq_pairs.reshape(Q_RANK, -1)], axis=1)
    w_pe = w_dkv[:, KV_RANK:]
    w_dkv2 = jnp.concatenate([w_dkv[:, :KV_RANK], w_pe, _swap_cols(w_pe)], axis=1)
    wkv = w_ukv.reshape(KV_RANK, N_HEADS, QK_NOPE + V_HEAD)
    w_ukv2 = jnp.concatenate([wkv[:, :, :QK_NOPE].reshape(KV_RANK, -1),
                              wkv[:, :, QK_NOPE:].reshape(KV_RANK, -1)], axis=1)
    return {"w_dq": w_dq.astype(BF16), "g_q": g_q.reshape(1, Q_RANK), "w_uq": w_uq2.astype(BF16),
            "w_dkv": w_dkv2.astype(BF16), "g_kv": g_kv.reshape(1, KV_RANK), "w_ukv": w_ukv2.astype(BF16),
            "w_o": w_o.astype(BF16)}


def _cast_body(a_ref, b_ref, ao_ref, bo_ref):
    ao_ref[...] = a_ref[...].astype(BF16)
    bo_ref[...] = b_ref[...].astype(BF16)


def _ffn_first_weights(w_in, w_out):
    steps = D_FF // ROW_BLOCK
    cols_in = 2 * D_FF // steps
    return pl.pallas_call(
        _cast_body,
        grid=(steps,),
        in_specs=[pl.BlockSpec((None, D_MODEL, cols_in), lambda c: (0, 0, c)),
                  pl.BlockSpec((None, ROW_BLOCK, D_MODEL), lambda c: (0, c, 0))],
        out_specs=[pl.BlockSpec((D_MODEL, cols_in), lambda c: (0, c)),
                   pl.BlockSpec((ROW_BLOCK, D_MODEL), lambda c: (c, 0))],
        out_shape=[jax.ShapeDtypeStruct(w_in.shape[1:], BF16), jax.ShapeDtypeStruct(w_out.shape[1:], BF16)],
        compiler_params=_cparams(1),
        name="ffn_weight_cast",
    )(w_in, w_out)


def kernel(x_prompt, x_sample, cache_ckv, cache_kpe, c, c_ctx, w_ada, b_ada, w_fnet, w_pool, pool_scale,
           w_dq, g_q, w_uq, w_dkv, g_kv, w_ukv, w_o, w_ffn_in, w_ffn_conv, b_ffn_conv, w_ffn_out, g_final):
    batch, seq, _ = x_prompt.shape
    dec_batch, dec_seq, _ = x_sample.shape
    past = cache_ckv.shape[2]

    cond = jnp.concatenate([c_ctx[None, :], c, jnp.zeros((SUBLANES - 1 - dec_batch, D_MODEL), F32)], axis=0)
    mods = _ada_all_layers(cond, w_ada, b_ada)
    w_in_bf, w_out_bf = _ffn_first_weights(w_ffn_in, w_ffn_out)
    b_conv = b_ffn_conv[:, None, :]

    xp = x_prompt.reshape(batch * seq, D_MODEL)
    xs = x_sample.reshape(dec_batch * dec_seq, D_MODEL)
    states = []
    j_f = j_p = j_a = 0
    for i in range(DEPTH):
        mod_p = mods[i, 0:1].reshape(1, 1, N_MOD * D_MODEL)
        mod_s = mods[i, 1:1 + dec_batch].reshape(dec_batch, 1, N_MOD * D_MODEL)
        kind = i % N_MIXERS
        if kind == 0:
            w = w_fnet[j_f].astype(BF16)
            xp = _fourier_mix(xp, mod_p, seq, w)
            xs = _fourier_mix(xs, mod_s, dec_seq, w)
            j_f += 1
        elif kind == 1:
            w = w_pool[j_p].astype(BF16)
            ps = pool_scale[j_p].reshape(1, D_MODEL)
            xp = _pool_mix(xp, mod_p, seq, w, ps)
            xs = _pool_mix(xs, mod_s, dec_seq, w, ps)
            j_p += 1
        else:
            prm = _mla_weights(w_dq[j_a], g_q[j_a], w_uq[j_a], w_dkv[j_a], g_kv[j_a], w_ukv[j_a], w_o[j_a])
            q, k, v, ckv_p, kpe_p = _mla_project(xp, mod_p, seq, prm, state=True)
            xp = _mla_attend(xp, mod_p, seq, q, k, v, prm["w_o"])
            states.append((ckv_p.reshape(batch, seq, KV_RANK), kpe_p.reshape(batch, seq, QK_ROPE)))
            q, k, v = _mla_project(xs, mod_s, dec_seq, prm, cs=_rope_table(dec_seq))
            kpe_c = cache_kpe[:, j_a].reshape(dec_batch * past, QK_ROPE)
            krc = jnp.concatenate([kpe_c, jnp.zeros_like(kpe_c)], axis=1).astype(BF16)
            kc, vc = _ctx_expand(cache_ckv[:, j_a].reshape(dec_batch * past, KV_RANK), krc, prm["w_ukv"])
            xs = _mla_attend(xs, mod_s, dec_seq, q, k, v, prm["w_o"], ctx=(kc, vc))
            j_a += 1
        g_fin = g_final.reshape(1, D_MODEL) if i == DEPTH - 1 else None
        ffn_w = (w_in_bf, w_ffn_conv, b_conv, w_out_bf)
        if i + 1 < DEPTH:
            xp, w_in_bf, w_out_bf = _conv_ffn(xp, mod_p, seq, i, *ffn_w, cast_next=(w_ffn_in, w_ffn_out))
        else:
            xp, = _conv_ffn(xp, mod_p, seq, i, *ffn_w, g_final=g_fin)
        xs, = _conv_ffn(xs, mod_s, dec_seq, i, *ffn_w, g_final=g_fin)

    state_ckv = jnp.stack([s[0] for s in states], axis=1)
    state_kpe = jnp.stack([s[1] for s in states], axis=1)
    return (xp.reshape(batch, seq, D_MODEL), xs.reshape(dec_batch, dec_seq, D_MODEL), state_ckv, state_kpe)
```

```python
import functools

import numpy as np
import jax
import jax.numpy as jnp
from jax import lax
from jax.experimental import pallas as pl
from jax.experimental.pallas import tpu as pltpu

F32 = jnp.float32
BF16 = jnp.bfloat16

D_MODEL = 1024
DEPTH = 4
GRID_W = 64
N_MIXERS = 3
N_FOURIER_GROUPS = 4
FOURIER_GROUP = D_MODEL // N_FOURIER_GROUPS
POOL_WINDOWS = (2, 4, 8, 16)
POOL_GROUP = D_MODEL // len(POOL_WINDOWS)
N_HEADS = 8
QK_NOPE = 128
QK_ROPE = 64
V_HEAD = 128
Q_RANK = 768
KV_RANK = 512
ROPE_AXIS = QK_ROPE // 2
ROPE_BASE = 10000.0
SM_SCALE = (QK_NOPE + QK_ROPE) ** -0.5
D_FF = 2816
N_MOD = 6
EPS = 1e-6

SUBLANES = 8
LANES = 128
BF16_SUBLANES = 16
VMEM_LIMIT_BYTES = 56 * 1024 * 1024

ROW_BLOCK = 256
FFN_CHUNK = 512
FFN_TILE = 512
MLA_TILE = 512
ATT_Q_TILE = 256
ADA_COLS = 2048
POOL_HALO = 128
POOL_SEQS = 4
HEAD_LANES = 2 * QK_ROPE
QK_TILE = QK_NOPE + HEAD_LANES


def _cparams(n_axes):
    return pltpu.CompilerParams(dimension_semantics=("arbitrary",) * n_axes,
                                vmem_limit_bytes=VMEM_LIMIT_BYTES)


def _rms(x):
    return x * lax.rsqrt(jnp.mean(x * x, axis=-1, keepdims=True) + EPS)


def _modulate(x, mod, sub):
    shift = mod[:, (3 * sub) * D_MODEL:(3 * sub + 1) * D_MODEL]
    scale = mod[:, (3 * sub + 1) * D_MODEL:(3 * sub + 2) * D_MODEL]
    return _rms(x) * (1.0 + scale) + shift


def _gate(mod, sub):
    return mod[:, (3 * sub + 2) * D_MODEL:(3 * sub + 3) * D_MODEL]


def _dot(a, b):
    return jnp.dot(a, b, preferred_element_type=F32)


def _dot_nt(a, b):
    return lax.dot_general(a, b, (((1,), (1,)), ((), ())), preferred_element_type=F32)


def _ada_body(cond_ref, w_ref, b_ref, o_ref):
    a = jax.nn.silu(cond_ref[...]).astype(BF16)
    o_ref[...] = _dot(a, w_ref[...].astype(BF16)) + b_ref[...]


def _ada_all_layers(cond, w_ada, b_ada):
    n_out = N_MOD * D_MODEL
    return pl.pallas_call(
        _ada_body,
        grid=(DEPTH, n_out // ADA_COLS),
        in_specs=[
            pl.BlockSpec((SUBLANES, D_MODEL), lambda l, j: (0, 0)),
            pl.BlockSpec((None, D_MODEL, ADA_COLS), lambda l, j: (l, 0, j)),
            pl.BlockSpec((None, 1, ADA_COLS), lambda l, j: (l, 0, j)),
        ],
        out_specs=pl.BlockSpec((None, SUBLANES, ADA_COLS), lambda l, j: (l, 0, j)),
        out_shape=jax.ShapeDtypeStruct((DEPTH, SUBLANES, n_out), F32),
        compiler_params=_cparams(2),
        name="ada_mod",
    )(cond, w_ada, b_ada.reshape(DEPTH, 1, n_out))


def _ffn_chunks():
    edges = list(range(0, D_FF, FFN_CHUNK)) + [D_FF]
    return list(zip(edges[:-1], edges[1:]))


def _ffn_body(seq_len, n_seg, seg, final, cast_next, *refs):
    refs = list(refs)
    x_ref, xp_ref, xn_ref, mod_ref, win_ref, wc_ref, bc_ref, wout_ref = refs[:8]
    del refs[:8]
    gfin_ref = refs.pop(0) if final else None
    if cast_next:
        next_in_ref, next_out_ref, o_ref, next_in_bf_ref, next_out_bf_ref = refs[:5]
        del refs[:5]
        next_in_bf_ref[...] = next_in_ref[...].astype(BF16)
        next_out_bf_ref[...] = next_out_ref[...].astype(BF16)
    else:
        o_ref = refs.pop(0)
    hext_ref, uga_ref, uva_ref, ugb_ref, uvb_ref, acta_ref, actb_ref, acc_ref = refs
    tile = n_seg * seg
    gap = BF16_SUBLANES
    i = pl.program_id(0)
    mod = mod_ref[...]
    x = x_ref[...]

    start = i * tile
    prev_ok = (start % seq_len != 0).astype(F32)
    next_ok = ((start + tile) % seq_len != 0).astype(F32)
    zeros8 = jnp.zeros((SUBLANES, D_MODEL), F32)
    h_prev = _modulate(xp_ref[...], mod, 1) * prev_ok
    h_next = _modulate(xn_ref[...], mod, 1) * next_ok
    hext_ref[0:gap, :] = jnp.concatenate([zeros8, h_prev], axis=0).astype(BF16)
    h = _modulate(x, mod, 1).astype(BF16)
    for k in range(n_seg):
        base = gap + k * (seg + gap)
        hext_ref[base:base + seg, :] = h[k * seg:(k + 1) * seg, :]
        if k == n_seg - 1:
            tail = jnp.concatenate([h_next, zeros8], axis=0).astype(BF16)
        else:
            tail = jnp.zeros((gap, D_MODEL), BF16)
        hext_ref[base + seg:base + seg + gap, :] = tail
    acc_ref[...] = jnp.zeros_like(acc_ref)

    def up(cols, ug_ref, uv_ref):
        lo, hi = cols
        ug_ref[:, :hi - lo] = _dot(hext_ref[...], win_ref[:, lo:hi])
        uv_ref[:, :hi - lo] = _dot(hext_ref[...], win_ref[:, D_FF + lo:D_FF + hi])

    def conv(u_ref, b, lo, hi):
        wc = wc_ref[:, lo:hi]
        return (u_ref[b - 1:b - 1 + seg, :hi - lo] * wc[0:1, :] + u_ref[b:b + seg, :hi - lo] * wc[1:2, :]
                + u_ref[b + 1:b + 1 + seg, :hi - lo] * wc[2:3, :] + bc_ref[:, lo:hi])

    def down(cols, ug_ref, uv_ref, act_ref):
        lo, hi = cols
        for k in range(n_seg):
            b = gap + k * (seg + gap)
            a = jax.nn.silu(conv(ug_ref, b, lo, hi)) * conv(uv_ref, b, D_FF + lo, D_FF + hi)
            act_ref[k * seg:(k + 1) * seg, :hi - lo] = a.astype(BF16)
        acc_ref[...] += _dot(act_ref[:, :hi - lo], wout_ref[lo:hi, :])

    chunks = _ffn_chunks()
    u_bufs = ((uga_ref, uva_ref), (ugb_ref, uvb_ref))
    act_bufs = (acta_ref, actb_ref)
    up(chunks[0], *u_bufs[0])
    for c, cols in enumerate(chunks):
        if c + 1 < len(chunks):
            up(chunks[c + 1], *u_bufs[(c + 1) % 2])
        down(cols, *u_bufs[c % 2], act_bufs[c % 2])
    y = x + _gate(mod, 1) * acc_ref[...]
    if final:
        y = _rms(y) * gfin_ref[...]
    o_ref[...] = y


def _conv_ffn(x, mod, seq_len, layer, w_in, w_conv, b_conv, w_out, g_final=None, cast_next=None):
    n = x.shape[0]
    tile = FFN_TILE
    seg = min(seq_len, tile)
    n_seg = tile // seg
    rows = BF16_SUBLANES + n_seg * (seg + BF16_SUBLANES)
    per_seq_mod = mod.shape[0] > 1
    final = g_final is not None
    blocks8 = n // SUBLANES
    t8 = tile // SUBLANES

    def resident(a):
        return pl.BlockSpec(a.shape, lambda i: (0,) * a.ndim, pipeline_mode=pl.Buffered(1))

    def resident_layer(a):
        shape = a.shape[1:]
        return pl.BlockSpec((None,) + shape, lambda i: (layer,) + (0,) * len(shape),
                            pipeline_mode=pl.Buffered(1))

    in_specs = [
        pl.BlockSpec((tile, D_MODEL), lambda i: (i, 0)),
        pl.BlockSpec((SUBLANES, D_MODEL), lambda i: (jnp.maximum(i * t8 - 1, 0), 0)),
        pl.BlockSpec((SUBLANES, D_MODEL), lambda i: (jnp.minimum((i + 1) * t8, blocks8 - 1), 0)),
        pl.BlockSpec((None, 1, N_MOD * D_MODEL),
                     (lambda i: ((i * tile) // seq_len, 0, 0)) if per_seq_mod else (lambda i: (0, 0, 0))),
        resident(w_in), resident_layer(w_conv), resident_layer(b_conv), resident(w_out),
    ]
    args = [x, x, x, mod, w_in, w_conv, b_conv, w_out]
    if final:
        in_specs.append(pl.BlockSpec((1, D_MODEL), lambda i: (0, 0)))
        args.append(g_final)
    out_specs = [pl.BlockSpec((tile, D_MODEL), lambda i: (i, 0))]
    out_shape = [jax.ShapeDtypeStruct((n, D_MODEL), F32)]
    if cast_next is not None:
        steps = n // tile
        for a in cast_next:
            rows_step = a.shape[1] // steps
            assert rows_step * steps == a.shape[1] and rows_step % BF16_SUBLANES == 0
            in_specs.append(pl.BlockSpec((None, rows_step, a.shape[2]), lambda i: (layer + 1, i, 0)))
            out_specs.append(pl.BlockSpec((rows_step, a.shape[2]), lambda i: (i, 0)))
            out_shape.append(jax.ShapeDtypeStruct(a.shape[1:], BF16))
            args.append(a)
    return pl.pallas_call(
        functools.partial(_ffn_body, seq_len, n_seg, seg, final, cast_next is not None),
        grid=(n // tile,),
        in_specs=in_specs,
        out_specs=out_specs,
        out_shape=out_shape,
        scratch_shapes=[pltpu.VMEM((rows, D_MODEL), BF16)]
        + [pltpu.VMEM((rows, FFN_CHUNK), F32)] * 4
        + [pltpu.VMEM((tile, FFN_CHUNK), BF16)] * 2
        + [pltpu.VMEM((tile, D_MODEL), F32)],
        compiler_params=_cparams(1),
        name="conv_ffn",
    )(*args)


def _fnet_body(seq_len, x_ref, mod_ref, csc_ref, csrow_ref, w_ref, o_ref, stack_ref):
    r = pl.program_id(1)
    mod = mod_ref[...]

    @pl.when(r == 0)
    def _():
        def rows(j, carry):
            r0 = pl.multiple_of(j * ROW_BLOCK, ROW_BLOCK)
            h = _modulate(x_ref[pl.ds(r0, ROW_BLOCK), :], mod, 0).astype(BF16)
            for g in range(N_FOURIER_GROUPS):
                cols = slice(g * FOURIER_GROUP, (g + 1) * FOURIER_GROUP)
                y = _dot(h[:, cols], csc_ref[...].astype(BF16))
                stack_ref[pl.ds(r0, ROW_BLOCK), cols] = y[:, :FOURIER_GROUP].astype(BF16)
                stack_ref[pl.ds(seq_len + r0, ROW_BLOCK), cols] = y[:, FOURIER_GROUP:].astype(BF16)
            return carry
        lax.fori_loop(0, seq_len // ROW_BLOCK, rows, 0)

    f = _dot(csrow_ref[...].astype(BF16), stack_ref[...]) * ((seq_len * FOURIER_GROUP) ** -0.5)
    y = _dot(f.astype(BF16), w_ref[...])
    r0 = pl.multiple_of(r * ROW_BLOCK, ROW_BLOCK)
    o_ref[...] = x_ref[pl.ds(r0, ROW_BLOCK), :] + _gate(mod, 0) * y


def _fnet_split_body(seq_len, x_ref, mod_ref, csc_ref, even_ref, odd_ref, perm_ref, w_ref, o_ref,
                     plus_ref, minus_ref):
    r = pl.program_id(1)
    mod = mod_ref[...]
    half = seq_len // 2

    @pl.when(r == 0)
    def _():
        def rows(j, carry):
            r0 = pl.multiple_of(j * ROW_BLOCK, ROW_BLOCK)
            h_lo = _modulate(x_ref[pl.ds(r0, ROW_BLOCK), :], mod, 0).astype(BF16)
            h_hi = _modulate(x_ref[pl.ds(half + r0, ROW_BLOCK), :], mod, 0).astype(BF16)
            table = csc_ref[...].astype(BF16)
            for g in range(N_FOURIER_GROUPS):
                cols = slice(g * FOURIER_GROUP, (g + 1) * FOURIER_GROUP)
                y_lo = _dot(h_lo[:, cols], table)
                y_hi = _dot(h_hi[:, cols], table)
                for ref, y in ((plus_ref, y_lo + y_hi), (minus_ref, y_lo - y_hi)):
                    ref[pl.ds(r0, ROW_BLOCK), cols] = y[:, :FOURIER_GROUP].astype(BF16)
                    ref[pl.ds(half + r0, ROW_BLOCK), cols] = y[:, FOURIER_GROUP:].astype(BF16)
            return carry
        lax.fori_loop(0, half // ROW_BLOCK, rows, 0)

    scale = (seq_len * FOURIER_GROUP) ** -0.5
    even = _dot(even_ref[...].astype(BF16), plus_ref[...]) * scale
    odd = _dot(odd_ref[...].astype(BF16), minus_ref[...]) * scale
    f = _dot(perm_ref[...], jnp.concatenate([even, odd], axis=0).astype(BF16))
    y = _dot(f.astype(BF16), w_ref[...])
    r0 = pl.multiple_of(r * ROW_BLOCK, ROW_BLOCK)
    o_ref[...] = x_ref[pl.ds(r0, ROW_BLOCK), :] + _gate(mod, 0) * y


def _channel_table():
    c = np.arange(FOURIER_GROUP)
    ang = 2.0 * np.pi * ((c[:, None] * c[None, :]) % FOURIER_GROUP) / FOURIER_GROUP
    return jnp.asarray(np.concatenate([np.cos(ang), np.sin(ang)], axis=1), F32)


def _position_table(freqs, positions, seq_len):
    ang = 2.0 * np.pi * ((freqs[:, None] * positions[None, :]) % seq_len) / seq_len
    return jnp.asarray(np.concatenate([np.cos(ang), -np.sin(ang)], axis=1), F32)


def _fourier_mix(x, mod, seq_len, w):
    n = x.shape[0]
    batch = n // seq_len
    n_row_blocks = seq_len // ROW_BLOCK
    mod_spec = pl.BlockSpec((None, 1, N_MOD * D_MODEL),
                            (lambda b, r: (b, 0, 0)) if mod.shape[0] > 1 else (lambda b, r: (0, 0, 0)))
    common = dict(
        grid=(batch, n_row_blocks),
        out_specs=pl.BlockSpec((ROW_BLOCK, D_MODEL), lambda b, r: (b * n_row_blocks + r, 0)),
        out_shape=jax.ShapeDtypeStruct((n, D_MODEL), F32),
        compiler_params=_cparams(2),
        name="fourier_mix",
    )
    x_spec = pl.BlockSpec((seq_len, D_MODEL), lambda b, r: (b, 0))
    csc_spec = pl.BlockSpec((FOURIER_GROUP, 2 * FOURIER_GROUP), lambda b, r: (0, 0))
    w_spec = pl.BlockSpec((D_MODEL, D_MODEL), lambda b, r: (0, 0))
    t = np.arange(seq_len)
    if n_row_blocks == 1:
        return pl.pallas_call(
            functools.partial(_fnet_body, seq_len),
            in_specs=[x_spec, mod_spec, csc_spec,
                      pl.BlockSpec((ROW_BLOCK, 2 * seq_len), lambda b, r: (r, 0)), w_spec],
            scratch_shapes=[pltpu.VMEM((2 * seq_len, D_MODEL), BF16)],
            **common,
        )(x, mod, _channel_table(), _position_table(t, t, seq_len), w)
    half_block = ROW_BLOCK // 2
    u = np.arange(seq_len // 2)
    i = np.arange(half_block)
    perm = np.zeros((ROW_BLOCK, ROW_BLOCK), np.float32)
    perm[2 * i, i] = 1.0
    perm[2 * i + 1, half_block + i] = 1.0
    table_spec = pl.BlockSpec((half_block, seq_len), lambda b, r: (r, 0))
    return pl.pallas_call(
        functools.partial(_fnet_split_body, seq_len),
        in_specs=[x_spec, mod_spec, csc_spec, table_spec, table_spec,
                  pl.BlockSpec((ROW_BLOCK, ROW_BLOCK), lambda b, r: (0, 0)), w_spec],
        scratch_shapes=[pltpu.VMEM((seq_len, D_MODEL), BF16)] * 2,
        **common,
    )(x, mod, _channel_table(), _position_table(2 * u, u, seq_len), _position_table(2 * u + 1, u, seq_len),
      jnp.asarray(perm, BF16), w)


def _pool_bands():
    i = np.arange(ROW_BLOCK)[:, None]
    c = np.arange(2 * ROW_BLOCK)[None, :] - POOL_HALO
    bands = [((c >= i - w // 2) & (c < i - w // 2 + w)) for w in POOL_WINDOWS]
    return jnp.asarray(np.stack(bands), BF16)


def _pool_body(seq_len, n_seq, x_ref, mod_ref, band_ref, w_ref, ps_ref, o_ref, h_ref, hb_ref):
    r = pl.program_id(1)
    mod = mod_ref[...]
    n_blocks = seq_len // ROW_BLOCK
    padded = seq_len + 2 * POOL_HALO

    @pl.when(r == 0)
    def _():
        zeros = jnp.zeros((POOL_HALO, D_MODEL), BF16)
        for s in range(n_seq):
            hb_ref[s * padded:s * padded + POOL_HALO, :] = zeros
            hb_ref[(s + 1) * padded - POOL_HALO:(s + 1) * padded, :] = zeros
            for j in range(n_blocks):
                rows = slice(s * seq_len + j * ROW_BLOCK, s * seq_len + (j + 1) * ROW_BLOCK)
                h = _modulate(x_ref[rows, :], mod, 0)
                h_ref[rows, :] = h
                start = s * padded + POOL_HALO + j * ROW_BLOCK
                hb_ref[start:start + ROW_BLOCK, :] = h.astype(BF16)

    gate = _gate(mod, 0)
    t = r * ROW_BLOCK + lax.broadcasted_iota(jnp.int32, (ROW_BLOCK, POOL_GROUP), 0)
    for s in range(n_seq):
        rows = pl.ds(pl.multiple_of(s * seq_len + r * ROW_BLOCK, ROW_BLOCK), ROW_BLOCK)
        slab = pl.ds(pl.multiple_of(s * padded + r * ROW_BLOCK, ROW_BLOCK), 2 * ROW_BLOCK)
        for g, w in enumerate(POOL_WINDOWS):
            cols = slice(g * POOL_GROUP, (g + 1) * POOL_GROUP)
            total = _dot(band_ref[g], hb_ref[slab, cols])
            cnt = jnp.minimum(t - w // 2 + w, seq_len) - jnp.maximum(t - w // 2, 0)
            pooled = total / cnt.astype(F32) - h_ref[rows, cols]
            y = _dot(pooled.astype(BF16), w_ref[g]) * ps_ref[:, cols]
            o_ref[s * ROW_BLOCK:(s + 1) * ROW_BLOCK, cols] = x_ref[rows, cols] + gate[:, cols] * y


def _pool_mix(x, mod, seq_len, w_pool, pool_scale):
    n = x.shape[0]
    batch = n // seq_len
    n_blocks = seq_len // ROW_BLOCK
    assert POOL_HALO >= max(POOL_WINDOWS) // 2 and 2 * POOL_HALO == ROW_BLOCK
    n_seq = POOL_SEQS if (n_blocks == 1 and mod.shape[0] == 1 and batch % POOL_SEQS == 0) else 1
    return pl.pallas_call(
        functools.partial(_pool_body, seq_len, n_seq),
        grid=(batch // n_seq, n_blocks),
        in_specs=[
            pl.BlockSpec((n_seq * seq_len, D_MODEL), lambda b, r: (b, 0)),
            pl.BlockSpec((None, 1, N_MOD * D_MODEL),
                         (lambda b, r: (b, 0, 0)) if mod.shape[0] > 1 else (lambda b, r: (0, 0, 0))),
            pl.BlockSpec((len(POOL_WINDOWS), ROW_BLOCK, 2 * ROW_BLOCK), lambda b, r: (0, 0, 0)),
            pl.BlockSpec((len(POOL_WINDOWS), POOL_GROUP, POOL_GROUP), lambda b, r: (0, 0, 0)),
            pl.BlockSpec((1, D_MODEL), lambda b, r: (0, 0)),
        ],
        out_specs=pl.BlockSpec((n_seq * ROW_BLOCK, D_MODEL), lambda b, r: (b * n_blocks + r, 0)),
        out_shape=jax.ShapeDtypeStruct((n, D_MODEL), F32),
        scratch_shapes=[pltpu.VMEM((n_seq * seq_len, D_MODEL), F32),
                        pltpu.VMEM((n_seq * (seq_len + 2 * POOL_HALO), D_MODEL), BF16)],
        compiler_params=_cparams(2),
        name="pool_mix",
    )(x, mod, _pool_bands(), w_pool, pool_scale)


def _rope_pair(a, cs):
    t = a * cs
    return t + pltpu.roll(t, QK_ROPE, 1)


def _store_keys(k_ref, k_nope, k_rope):
    for hd in range(N_HEADS):
        k_ref[:, hd * QK_TILE:hd * QK_TILE + QK_NOPE] = k_nope[:, hd * QK_NOPE:(hd + 1) * QK_NOPE]
        k_ref[:, hd * QK_TILE + QK_NOPE:(hd + 1) * QK_TILE] = k_rope


def _mla_proj_body(rope, state, *refs):
    refs = list(refs)
    x_ref, mod_ref = refs[:2]
    del refs[:2]
    cs_ref = refs.pop(0) if rope else None
    wdq_ref, gq_ref, wuq_ref, wdkv_ref, gkv_ref, wukv_ref = refs[:6]
    del refs[:6]
    q_ref, k_ref, v_ref = refs[:3]
    del refs[:3]
    h = _modulate(x_ref[...], mod_ref[...], 0).astype(BF16)
    cq = (_rms(_dot(h, wdq_ref[...])) * gq_ref[...]).astype(BF16)
    q = _dot(cq, wuq_ref[...]) * SM_SCALE
    n_nope = N_HEADS * QK_NOPE
    for hd in range(N_HEADS):
        a = q[:, n_nope + hd * HEAD_LANES:n_nope + (hd + 1) * HEAD_LANES]
        if rope:
            a = _rope_pair(a, cs_ref[...])
        q_ref[:, hd * QK_TILE:hd * QK_TILE + QK_NOPE] = q[:, hd * QK_NOPE:(hd + 1) * QK_NOPE].astype(BF16)
        q_ref[:, hd * QK_TILE + QK_NOPE:(hd + 1) * QK_TILE] = a.astype(BF16)
    ckv = _dot(h, wdkv_ref[...])
    c = _rms(ckv[:, :KV_RANK]) * gkv_ref[...]
    kp = ckv[:, KV_RANK:]
    if state:
        ckv_out_ref, kpe_out_ref = refs
        ckv_out_ref[...] = c
        kpe_out_ref[...] = kp[:, :QK_ROPE]
    if rope:
        kp = _rope_pair(kp, cs_ref[...])
    lane = lax.broadcasted_iota(jnp.int32, kp.shape, 1)
    kr = jnp.where(lane < QK_ROPE, kp, 0.0).astype(BF16)
    kv = _dot(c.astype(BF16), wukv_ref[...])
    _store_keys(k_ref, kv[:, :n_nope].astype(BF16), kr)
    v_ref[...] = kv[:, n_nope:].astype(BF16)


def _mla_project(x, mod, seq_len, prm, cs=None, state=False):
    n = x.shape[0]
    rope = cs is not None
    tile = min(MLA_TILE, seq_len) if (rope or mod.shape[0] > 1) else MLA_TILE
    tiles_per_seq = max(seq_len // tile, 1)
    wide = N_HEADS * QK_NOPE

    def const(shape):
        return pl.BlockSpec(shape, lambda i: (0,) * len(shape))

    def rows(width):
        return pl.BlockSpec((tile, width), lambda i: (i, 0))

    in_specs = [rows(D_MODEL),
                pl.BlockSpec((None, 1, N_MOD * D_MODEL),
                             (lambda i: (i // tiles_per_seq, 0, 0)) if mod.shape[0] > 1
                             else (lambda i: (0, 0, 0)))]
    args = [x, mod]
    if rope:
        in_specs.append(pl.BlockSpec((tile, HEAD_LANES), lambda i: (i % tiles_per_seq, 0)))
        args.append(cs)
    for name in ("w_dq", "g_q", "w_uq", "w_dkv", "g_kv", "w_ukv"):
        in_specs.append(const(prm[name].shape))
        args.append(prm[name])
    out_specs = [rows(N_HEADS * QK_TILE), rows(N_HEADS * QK_TILE), rows(wide)]
    out_shape = [jax.ShapeDtypeStruct((n, N_HEADS * QK_TILE), BF16),
                 jax.ShapeDtypeStruct((n, N_HEADS * QK_TILE), BF16),
                 jax.ShapeDtypeStruct((n, wide), BF16)]
    if state:
        out_specs += [rows(KV_RANK), rows(QK_ROPE)]
        out_shape += [jax.ShapeDtypeStruct((n, KV_RANK), F32), jax.ShapeDtypeStruct((n, QK_ROPE), F32)]
    return pl.pallas_call(
        functools.partial(_mla_proj_body, rope, state),
        grid=(n // tile,),
        in_specs=in_specs,
        out_specs=out_specs,
        out_shape=out_shape,
        compiler_params=_cparams(1),
        name="mla_project",
    )(*args)


def _ctx_expand_body(c_ref, kr_ref, w_ref, k_ref, v_ref):
    kv = _dot(c_ref[...].astype(BF16), w_ref[...])
    n_nope = N_HEADS * QK_NOPE
    _store_keys(k_ref, kv[:, :n_nope].astype(BF16), kr_ref[...])
    v_ref[...] = kv[:, n_nope:].astype(BF16)


def _ctx_expand(c_kv, k_rope, w_ukv):
    n = c_kv.shape[0]
    wide = N_HEADS * QK_NOPE
    return pl.pallas_call(
        _ctx_expand_body,
        grid=(n // ROW_BLOCK,),
        in_specs=[pl.BlockSpec((ROW_BLOCK, KV_RANK), lambda i: (i, 0)),
                  pl.BlockSpec((ROW_BLOCK, HEAD_LANES), lambda i: (i, 0)),
                  pl.BlockSpec(w_ukv.shape, lambda i: (0, 0))],
        out_specs=[pl.BlockSpec((ROW_BLOCK, N_HEADS * QK_TILE), lambda i: (i, 0)),
                   pl.BlockSpec((ROW_BLOCK, wide), lambda i: (i, 0))],
        out_shape=[jax.ShapeDtypeStruct((n, N_HEADS * QK_TILE), BF16), jax.ShapeDtypeStruct((n, wide), BF16)],
        compiler_params=_cparams(1),
        name="mla_ctx_expand",
    )(c_kv, k_rope, w_ukv)


def _attn_body(has_ctx, *refs):
    refs = list(refs)
    q_ref, k_ref, v_ref = refs[:3]
    del refs[:3]
    if has_ctx:
        kc_ref, vc_ref = refs[:2]
        del refs[:2]
    x_ref, mod_ref, wo_ref, o_ref, heads_ref = refs
    for hd in range(N_HEADS):
        qk = slice(hd * QK_TILE, (hd + 1) * QK_TILE)
        vcols = slice(hd * V_HEAD, (hd + 1) * V_HEAD)
        q = q_ref[:, qk]
        s = _dot_nt(q, k_ref[:, qk])
        m = jnp.max(s, axis=-1, keepdims=True)
        if has_ctx:
            sc = _dot_nt(q, kc_ref[:, qk])
            m = jnp.maximum(m, jnp.max(sc, axis=-1, keepdims=True))
        p = jnp.exp(s - m)
        l = jnp.sum(p, axis=-1, keepdims=True)
        o = _dot(p.astype(BF16), v_ref[:, vcols])
        if has_ctx:
            pc = jnp.exp(sc - m)
            l = l + jnp.sum(pc, axis=-1, keepdims=True)
            o = o + _dot(pc.astype(BF16), vc_ref[:, vcols])
        heads_ref[:, vcols] = (o / l).astype(BF16)
    y = _dot(heads_ref[...], wo_ref[...])
    o_ref[...] = x_ref[...] + _gate(mod_ref[...], 0) * y


def _mla_attend(x, mod, seq_len, q, k, v, w_o, ctx=None):
    n = x.shape[0]
    tq = min(ATT_Q_TILE, seq_len)
    q_tiles = seq_len // tq
    wide = N_HEADS * QK_NOPE
    has_ctx = ctx is not None

    def q_rows(width):
        return pl.BlockSpec((tq, width), lambda b, r: (b * q_tiles + r, 0))

    def seq_rows(length, width):
        return pl.BlockSpec((length, width), lambda b, r: (b, 0))

    in_specs = [q_rows(N_HEADS * QK_TILE), seq_rows(seq_len, N_HEADS * QK_TILE), seq_rows(seq_len, wide)]
    args = [q, k, v]
    if has_ctx:
        kc, vc = ctx
        past = kc.shape[0] // (n // seq_len)
        in_specs += [seq_rows(past, N_HEADS * QK_TILE), seq_rows(past, wide)]
        args += [kc, vc]
    in_specs += [q_rows(D_MODEL),
                 pl.BlockSpec((None, 1, N_MOD * D_MODEL),
                              (lambda b, r: (b, 0, 0)) if mod.shape[0] > 1 else (lambda b, r: (0, 0, 0))),
                 pl.BlockSpec(w_o.shape, lambda b, r: (0, 0))]
    args += [x, mod, w_o]
    return pl.pallas_call(
        functools.partial(_attn_body, has_ctx),
        grid=(n // seq_len, q_tiles),
        in_specs=in_specs,
        out_specs=q_rows(D_MODEL),
        out_shape=jax.ShapeDtypeStruct((n, D_MODEL), F32),
        scratch_shapes=[pltpu.VMEM((tq, wide), BF16)],
        compiler_params=_cparams(2),
        name="mla_attend",
    )(*args)


def _rope_table(n_tok):
    rows = n_tok // GRID_W
    row = np.repeat(np.arange(rows), GRID_W).astype(np.float32)
    col = np.tile(np.arange(GRID_W), rows).astype(np.float32)
    inv = (1.0 / (ROPE_BASE ** (np.arange(0, ROPE_AXIS, 2, dtype=np.float32) / ROPE_AXIS))).astype(np.float32)
    ang_r = row[:, None] * inv
    ang_c = col[:, None] * inv
    cos = np.concatenate([np.cos(ang_r), np.cos(ang_r), np.cos(ang_c), np.cos(ang_c)], axis=1)
    sin = np.concatenate([np.sin(ang_r), np.sin(ang_r), np.sin(ang_c), np.sin(ang_c)], axis=1)
    return jnp.asarray(np.concatenate([cos, sin], axis=1), F32)


def _swap_cols(w):
    half = ROPE_AXIS // 2
    return jnp.concatenate([-w[..., half:2 * half], w[..., 0:half],
                            -w[..., 3 * half:4 * half], w[..., 2 * half:3 * half]], axis=-1)


def _mla_weights(w_dq, g_q, w_uq, w_dkv, g_kv, w_ukv, w_o):
    wq = w_uq.reshape(Q_RANK, N_HEADS, QK_NOPE + QK_ROPE)
    wq_pe = wq[:, :, QK_NOPE:]
    wq_pairs = jnp.concatenate([wq_pe, _swap_cols(wq_pe)], axis=-1)
    w_uq2 = jnp.concatenate([wq[:, :, :QK_NOPE].reshape(Q_RANK, -1), wq_pairs.reshape(Q_RANK, -1)], axis=1)
    w_pe = w_dkv[:, KV_RANK:]
    w_dkv2 = jnp.concatenate([w_dkv[:, :KV_RANK], w_pe, _swap_cols(w_pe)], axis=1)
    wkv = w_ukv.reshape(KV_RANK, N_HEADS, QK_NOPE + V_HEAD)
    w_ukv2 = jnp.concatenate([wkv[:, :, :QK_NOPE].reshape(KV_RANK, -1),
                              wkv[:, :, QK_NOPE:].reshape(KV_RANK, -1)], axis=1)
    return {"w_dq": w_dq.astype(BF16), "g_q": g_q.reshape(1, Q_RANK), "w_uq": w_uq2.astype(BF16),
            "w_dkv": w_dkv2.astype(BF16), "g_kv": g_kv.reshape(1, KV_RANK), "w_ukv": w_ukv2.astype(BF16),
            "w_o": w_o.astype(BF16)}


def _cast_body(a_ref, b_ref, ao_ref, bo_ref):
    ao_ref[...] = a_ref[...].astype(BF16)
    bo_ref[...] = b_ref[...].astype(BF16)


def _ffn_first_weights(w_in, w_out):
    steps = D_FF // ROW_BLOCK
    cols_in = 2 * D_FF // steps
    return pl.pallas_call(
        _cast_body,
        grid=(steps,),
        in_specs=[pl.BlockSpec((None, D_MODEL, cols_in), lambda c: (0, 0, c)),
                  pl.BlockSpec((None, ROW_BLOCK, D_MODEL), lambda c: (0, c, 0))],
        out_specs=[pl.BlockSpec((D_MODEL, cols_in), lambda c: (0, c)),
                   pl.BlockSpec((ROW_BLOCK, D_MODEL), lambda c: (c, 0))],
        out_shape=[jax.ShapeDtypeStruct(w_in.shape[1:], BF16), jax.ShapeDtypeStruct(w_out.shape[1:], BF16)],
        compiler_params=_cparams(1),
        name="ffn_weight_cast",
    )(w_in, w_out)


def kernel(x_prompt, x_sample, cache_ckv, cache_kpe, c, c_ctx, w_ada, b_ada, w_fnet, w_pool, pool_scale,
           w_dq, g_q, w_uq, w_dkv, g_kv, w_ukv, w_o, w_ffn_in, w_ffn_conv, b_ffn_conv, w_ffn_out, g_final):
    batch, seq, _ = x_prompt.shape
    dec_batch, dec_seq, _ = x_sample.shape
    past = cache_ckv.shape[2]

    cond = jnp.concatenate([c_ctx[None, :], c, jnp.zeros((SUBLANES - 1 - dec_batch, D_MODEL), F32)], axis=0)
    mods = _ada_all_layers(cond, w_ada, b_ada)
    w_in_bf, w_out_bf = _ffn_first_weights(w_ffn_in, w_ffn_out)
    b_conv = b_ffn_conv[:, None, :]

    xp = x_prompt.reshape(batch * seq, D_MODEL)
    xs = x_sample.reshape(dec_batch * dec_seq, D_MODEL)
    states = []
    j_f = j_p = j_a = 0
    for i in range(DEPTH):
        mod_p = mods[i, 0:1].reshape(1, 1, N_MOD * D_MODEL)
        mod_s = mods[i, 1:1 + dec_batch].reshape(dec_batch, 1, N_MOD * D_MODEL)
        kind = i % N_MIXERS
        if kind == 0:
            w = w_fnet[j_f].astype(BF16)
            xp = _fourier_mix(xp, mod_p, seq, w)
            xs = _fourier_mix(xs, mod_s, dec_seq, w)
            j_f += 1
        elif kind == 1:
            w = w_pool[j_p].astype(BF16)
            ps = pool_scale[j_p].reshape(1, D_MODEL)
            xp = _pool_mix(xp, mod_p, seq, w, ps)
            xs = _pool_mix(xs, mod_s, dec_seq, w, ps)
            j_p += 1
        else:
            prm = _mla_weights(w_dq[j_a], g_q[j_a], w_uq[j_a], w_dkv[j_a], g_kv[j_a], w_ukv[j_a], w_o[j_a])
            q, k, v, ckv_p, kpe_p = _mla_project(xp, mod_p, seq, prm, state=True)
            xp = _mla_attend(xp, mod_p, seq, q, k, v, prm["w_o"])
            states.append((ckv_p.reshape(batch, seq, KV_RANK), kpe_p.reshape(batch, seq, QK_ROPE)))
            q, k, v = _mla_project(xs, mod_s, dec_seq, prm, cs=_rope_table(dec_seq))
            kpe_c = cache_kpe[:, j_a].reshape(dec_batch * past, QK_ROPE)
            krc = jnp.concatenate([kpe_c, jnp.zeros_like(kpe_c)], axis=1).astype(BF16)
            kc, vc = _ctx_expand(cache_ckv[:, j_a].reshape(dec_batch * past, KV_RANK), krc, prm["w_ukv"])
            xs = _mla_attend(xs, mod_s, dec_seq, q, k, v, prm["w_o"], ctx=(kc, vc))
            j_a += 1
        g_fin = g_final.reshape(1, D_MODEL) if i == DEPTH - 1 else None
        ffn_w = (w_in_bf, w_ffn_conv, b_conv, w_out_bf)
        if i + 1 < DEPTH:
            xp, w_in_bf, w_out_bf = _conv_ffn(xp, mod_p, seq, i, *ffn_w, cast_next=(w_ffn_in, w_ffn_out))
        else:
            xp, = _conv_ffn(xp, mod_p, seq, i, *ffn_w, g_final=g_fin)
        xs, = _conv_ffn(xs, mod_s, dec_seq, i, *ffn_w, g_final=g_fin)

    state_ckv = jnp.stack([s[0] for s in states], axis=1)
    state_kpe = jnp.stack([s[1] for s in states], axis=1)
    return (xp.reshape(batch, seq, D_MODEL), xs.reshape(dec_batch, dec_seq, D_MODEL), state_ckv, state_kpe)
```

```python
import functools

import numpy as np
import jax
import jax.numpy as jnp
from jax import lax
from jax.experimental import pallas as pl
from jax.experimental.pallas import tpu as pltpu

F32 = jnp.float32
BF16 = jnp.bfloat16

D_MODEL = 1024
DEPTH = 4
GRID_W = 64
N_MIXERS = 3
N_FOURIER_GROUPS = 4
FOURIER_GROUP = D_MODEL // N_FOURIER_GROUPS
POOL_WINDOWS = (2, 4, 8, 16)
POOL_GROUP = D_MODEL // len(POOL_WINDOWS)
N_HEADS = 8
QK_NOPE = 128
QK_ROPE = 64
V_HEAD = 128
Q_RANK = 768
KV_RANK = 512
ROPE_AXIS = QK_ROPE // 2
ROPE_BASE = 10000.0
SM_SCALE = (QK_NOPE + QK_ROPE) ** -0.5
D_FF = 2816
N_MOD = 6
EPS = 1e-6

SUBLANES = 8
LANES = 128
BF16_SUBLANES = 16
VMEM_LIMIT_BYTES = 56 * 1024 * 1024

ROW_BLOCK = 256
FFN_CHUNK = 512
FFN_TILE = 512
MLA_TILE = 512
ATT_Q_TILE = 256
ADA_COLS = 2048
ADA_SHARE_COLS = 1024
POOL_HALO = 128
POOL_SEQS = 4
HEAD_LANES = 2 * QK_ROPE
QK_TILE = QK_NOPE + HEAD_LANES


def _cparams(n_axes):
    return pltpu.CompilerParams(dimension_semantics=("arbitrary",) * n_axes,
                                vmem_limit_bytes=VMEM_LIMIT_BYTES)


def _rms(x):
    return x * lax.rsqrt(jnp.mean(x * x, axis=-1, keepdims=True) + EPS)


def _modulate(x, mod, sub):
    shift = mod[:, (3 * sub) * D_MODEL:(3 * sub + 1) * D_MODEL]
    scale = mod[:, (3 * sub + 1) * D_MODEL:(3 * sub + 2) * D_MODEL]
    return _rms(x) * (1.0 + scale) + shift


def _gate(mod, sub):
    return mod[:, (3 * sub + 2) * D_MODEL:(3 * sub + 3) * D_MODEL]


def _dot(a, b):
    return jnp.dot(a, b, preferred_element_type=F32)


def _dot_nt(a, b):
    return lax.dot_general(a, b, (((1,), (1,)), ((), ())), preferred_element_type=F32)


def _ada_body(cond_ref, w_ref, b_ref, o_ref):
    a = jax.nn.silu(cond_ref[...]).astype(BF16)
    o_ref[...] = _dot(a, w_ref[...].astype(BF16)) + b_ref[...]


def _ada_first_layer(cond, w_ada, b_ada):
    n_out = N_MOD * D_MODEL
    return pl.pallas_call(
        _ada_body,
        grid=(n_out // ADA_COLS,),
        in_specs=[
            pl.BlockSpec((SUBLANES, D_MODEL), lambda j: (0, 0)),
            pl.BlockSpec((None, D_MODEL, ADA_COLS), lambda j: (0, 0, j)),
            pl.BlockSpec((None, 1, ADA_COLS), lambda j: (0, 0, j)),
        ],
        out_specs=pl.BlockSpec((SUBLANES, ADA_COLS), lambda j: (0, j)),
        out_shape=jax.ShapeDtypeStruct((SUBLANES, n_out), F32),
        compiler_params=_cparams(1),
        name="ada_mod",
    )(cond, w_ada, b_ada)


def _ffn_chunks():
    edges = list(range(0, D_FF, FFN_CHUNK)) + [D_FF]
    return list(zip(edges[:-1], edges[1:]))


def _ffn_body(seq_len, n_seg, seg, final, cast_next, *refs):
    refs = list(refs)
    x_ref, xp_ref, xn_ref, mod_ref, win_ref, wc_ref, bc_ref, wout_ref = refs[:8]
    del refs[:8]
    gfin_ref = refs.pop(0) if final else None
    if cast_next:
        next_in_ref, next_out_ref, o_ref, next_in_bf_ref, next_out_bf_ref = refs[:5]
        del refs[:5]
        next_in_bf_ref[...] = next_in_ref[...].astype(BF16)
        next_out_bf_ref[...] = next_out_ref[...].astype(BF16)
    else:
        o_ref = refs.pop(0)
    hext_ref, uga_ref, uva_ref, ugb_ref, uvb_ref, acta_ref, actb_ref, acc_ref = refs
    tile = n_seg * seg
    gap = BF16_SUBLANES
    i = pl.program_id(0)
    mod = mod_ref[...]
    x = x_ref[...]

    start = i * tile
    prev_ok = (start % seq_len != 0).astype(F32)
    next_ok = ((start + tile) % seq_len != 0).astype(F32)
    zeros8 = jnp.zeros((SUBLANES, D_MODEL), F32)
    h_prev = _modulate(xp_ref[...], mod, 1) * prev_ok
    h_next = _modulate(xn_ref[...], mod, 1) * next_ok
    hext_ref[0:gap, :] = jnp.concatenate([zeros8, h_prev], axis=0).astype(BF16)
    h = _modulate(x, mod, 1).astype(BF16)
    for k in range(n_seg):
        base = gap + k * (seg + gap)
        hext_ref[base:base + seg, :] = h[k * seg:(k + 1) * seg, :]
        if k == n_seg - 1:
            tail = jnp.concatenate([h_next, zeros8], axis=0).astype(BF16)
        else:
            tail = jnp.zeros((gap, D_MODEL), BF16)
        hext_ref[base + seg:base + seg + gap, :] = tail
    acc_ref[...] = jnp.zeros_like(acc_ref)

    def up(cols, ug_ref, uv_ref):
        lo, hi = cols
        ug_ref[:, :hi - lo] = _dot(hext_ref[...], win_ref[:, lo:hi])
        uv_ref[:, :hi - lo] = _dot(hext_ref[...], win_ref[:, D_FF + lo:D_FF + hi])

    def conv(u_ref, b, lo, hi):
        wc = wc_ref[:, lo:hi]
        return (u_ref[b - 1:b - 1 + seg, :hi - lo] * wc[0:1, :] + u_ref[b:b + seg, :hi - lo] * wc[1:2, :]
                + u_ref[b + 1:b + 1 + seg, :hi - lo] * wc[2:3, :] + bc_ref[:, lo:hi])

    def down(cols, ug_ref, uv_ref, act_ref):
        lo, hi = cols
        for k in range(n_seg):
            b = gap + k * (seg + gap)
            a = jax.nn.silu(conv(ug_ref, b, lo, hi)) * conv(uv_ref, b, D_FF + lo, D_FF + hi)
            act_ref[k * seg:(k + 1) * seg, :hi - lo] = a.astype(BF16)
        acc_ref[...] += _dot(act_ref[:, :hi - lo], wout_ref[lo:hi, :])

    chunks = _ffn_chunks()
    u_bufs = ((uga_ref, uva_ref), (ugb_ref, uvb_ref))
    act_bufs = (acta_ref, actb_ref)
    up(chunks[0], *u_bufs[0])
    for c, cols in enumerate(chunks):
        if c + 1 < len(chunks):
            up(chunks[c + 1], *u_bufs[(c + 1) % 2])
        down(cols, *u_bufs[c % 2], act_bufs[c % 2])
    y = x + _gate(mod, 1) * acc_ref[...]
    if final:
        y = _rms(y) * gfin_ref[...]
    o_ref[...] = y


def _conv_ffn(x, mod, seq_len, layer, w_in, w_conv, b_conv, w_out, g_final=None, cast_next=None):
    n = x.shape[0]
    tile = FFN_TILE
    seg = min(seq_len, tile)
    n_seg = tile // seg
    rows = BF16_SUBLANES + n_seg * (seg + BF16_SUBLANES)
    per_seq_mod = mod.shape[0] > 1
    final = g_final is not None
    blocks8 = n // SUBLANES
    t8 = tile // SUBLANES

    def resident(a):
        return pl.BlockSpec(a.shape, lambda i: (0,) * a.ndim, pipeline_mode=pl.Buffered(1))

    def resident_layer(a):
        shape = a.shape[1:]
        return pl.BlockSpec((None,) + shape, lambda i: (layer,) + (0,) * len(shape),
                            pipeline_mode=pl.Buffered(1))

    in_specs = [
        pl.BlockSpec((tile, D_MODEL), lambda i: (i, 0)),
        pl.BlockSpec((SUBLANES, D_MODEL), lambda i: (jnp.maximum(i * t8 - 1, 0), 0)),
        pl.BlockSpec((SUBLANES, D_MODEL), lambda i: (jnp.minimum((i + 1) * t8, blocks8 - 1), 0)),
        pl.BlockSpec((None, 1, N_MOD * D_MODEL),
                     (lambda i: ((i * tile) // seq_len, 0, 0)) if per_seq_mod else (lambda i: (0, 0, 0))),
        resident(w_in), resident_layer(w_conv), resident_layer(b_conv), resident(w_out),
    ]
    args = [x, x, x, mod, w_in, w_conv, b_conv, w_out]
    if final:
        in_specs.append(pl.BlockSpec((1, D_MODEL), lambda i: (0, 0)))
        args.append(g_final)
    out_specs = [pl.BlockSpec((tile, D_MODEL), lambda i: (i, 0))]
    out_shape = [jax.ShapeDtypeStruct((n, D_MODEL), F32)]
    if cast_next is not None:
        steps = n // tile
        for a in cast_next:
            rows_step = a.shape[1] // steps
            assert rows_step * steps == a.shape[1] and rows_step % BF16_SUBLANES == 0
            in_specs.append(pl.BlockSpec((None, rows_step, a.shape[2]), lambda i: (layer + 1, i, 0)))
            out_specs.append(pl.BlockSpec((rows_step, a.shape[2]), lambda i: (i, 0)))
            out_shape.append(jax.ShapeDtypeStruct(a.shape[1:], BF16))
            args.append(a)
    return pl.pallas_call(
        functools.partial(_ffn_body, seq_len, n_seg, seg, final, cast_next is not None),
        grid=(n // tile,),
        in_specs=in_specs,
        out_specs=out_specs,
        out_shape=out_shape,
        scratch_shapes=[pltpu.VMEM((rows, D_MODEL), BF16)]
        + [pltpu.VMEM((rows, FFN_CHUNK), F32)] * 4
        + [pltpu.VMEM((tile, FFN_CHUNK), BF16)] * 2
        + [pltpu.VMEM((tile, D_MODEL), F32)],
        compiler_params=_cparams(1),
        name="conv_ffn",
    )(*args)


def _fnet_body(seq_len, ada, *refs):
    if ada:
        (x_ref, mod_ref, csc_ref, csrow_ref, w_ref, cond_ref, wada_ref, bada_ref,
         o_ref, mods_ref, stack_ref) = refs
        _ada_body(cond_ref, wada_ref, bada_ref, mods_ref)
    else:
        x_ref, mod_ref, csc_ref, csrow_ref, w_ref, o_ref, stack_ref = refs
    r = pl.program_id(1)
    mod = mod_ref[...]

    @pl.when(r == 0)
    def _():
        def rows(j, carry):
            r0 = pl.multiple_of(j * ROW_BLOCK, ROW_BLOCK)
            h = _modulate(x_ref[pl.ds(r0, ROW_BLOCK), :], mod, 0).astype(BF16)
            for g in range(N_FOURIER_GROUPS):
                cols = slice(g * FOURIER_GROUP, (g + 1) * FOURIER_GROUP)
                y = _dot(h[:, cols], csc_ref[...].astype(BF16))
                stack_ref[pl.ds(r0, ROW_BLOCK), cols] = y[:, :FOURIER_GROUP].astype(BF16)
                stack_ref[pl.ds(seq_len + r0, ROW_BLOCK), cols] = y[:, FOURIER_GROUP:].astype(BF16)
            return carry
        lax.fori_loop(0, seq_len // ROW_BLOCK, rows, 0)

    f = _dot(csrow_ref[...].astype(BF16), stack_ref[...]) * ((seq_len * FOURIER_GROUP) ** -0.5)
    y = _dot(f.astype(BF16), w_ref[...])
    r0 = pl.multiple_of(r * ROW_BLOCK, ROW_BLOCK)
    o_ref[...] = x_ref[pl.ds(r0, ROW_BLOCK), :] + _gate(mod, 0) * y


def _fnet_split_body(seq_len, x_ref, mod_ref, csc_ref, even_ref, odd_ref, perm_ref, w_ref, o_ref,
                     plus_ref, minus_ref):
    r = pl.program_id(1)
    mod = mod_ref[...]
    half = seq_len // 2

    @pl.when(r == 0)
    def _():
        def rows(j, carry):
            r0 = pl.multiple_of(j * ROW_BLOCK, ROW_BLOCK)
            h_lo = _modulate(x_ref[pl.ds(r0, ROW_BLOCK), :], mod, 0).astype(BF16)
            h_hi = _modulate(x_ref[pl.ds(half + r0, ROW_BLOCK), :], mod, 0).astype(BF16)
            table = csc_ref[...].astype(BF16)
            for g in range(N_FOURIER_GROUPS):
                cols = slice(g * FOURIER_GROUP, (g + 1) * FOURIER_GROUP)
                y_lo = _dot(h_lo[:, cols], table)
                y_hi = _dot(h_hi[:, cols], table)
                for ref, y in ((plus_ref, y_lo + y_hi), (minus_ref, y_lo - y_hi)):
                    ref[pl.ds(r0, ROW_BLOCK), cols] = y[:, :FOURIER_GROUP].astype(BF16)
                    ref[pl.ds(half + r0, ROW_BLOCK), cols] = y[:, FOURIER_GROUP:].astype(BF16)
            return carry
        lax.fori_loop(0, half // ROW_BLOCK, rows, 0)

    scale = (seq_len * FOURIER_GROUP) ** -0.5
    even = _dot(even_ref[...].astype(BF16), plus_ref[...]) * scale
    odd = _dot(odd_ref[...].astype(BF16), minus_ref[...]) * scale
    f = _dot(perm_ref[...], jnp.concatenate([even, odd], axis=0).astype(BF16))
    y = _dot(f.astype(BF16), w_ref[...])
    r0 = pl.multiple_of(r * ROW_BLOCK, ROW_BLOCK)
    o_ref[...] = x_ref[pl.ds(r0, ROW_BLOCK), :] + _gate(mod, 0) * y


def _channel_table():
    c = np.arange(FOURIER_GROUP)
    ang = 2.0 * np.pi * ((c[:, None] * c[None, :]) % FOURIER_GROUP) / FOURIER_GROUP
    return jnp.asarray(np.concatenate([np.cos(ang), np.sin(ang)], axis=1), F32)


def _position_table(freqs, positions, seq_len):
    ang = 2.0 * np.pi * ((freqs[:, None] * positions[None, :]) % seq_len) / seq_len
    return jnp.asarray(np.concatenate([np.cos(ang), -np.sin(ang)], axis=1), F32)


def _fourier_mix(x, mod, seq_len, w, ada=None):
    n = x.shape[0]
    batch = n // seq_len
    n_row_blocks = seq_len // ROW_BLOCK
    mod_spec = pl.BlockSpec((None, 1, N_MOD * D_MODEL),
                            (lambda b, r: (b, 0, 0)) if mod.shape[0] > 1 else (lambda b, r: (0, 0, 0)))
    common = dict(
        grid=(batch, n_row_blocks),
        out_specs=pl.BlockSpec((ROW_BLOCK, D_MODEL), lambda b, r: (b * n_row_blocks + r, 0)),
        out_shape=jax.ShapeDtypeStruct((n, D_MODEL), F32),
        compiler_params=_cparams(2),
        name="fourier_mix",
    )
    x_spec = pl.BlockSpec((seq_len, D_MODEL), lambda b, r: (b, 0))
    csc_spec = pl.BlockSpec((FOURIER_GROUP, 2 * FOURIER_GROUP), lambda b, r: (0, 0))
    w_spec = pl.BlockSpec((D_MODEL, D_MODEL), lambda b, r: (0, 0))
    t = np.arange(seq_len)
    if n_row_blocks == 1:
        in_specs = [x_spec, mod_spec, csc_spec,
                    pl.BlockSpec((ROW_BLOCK, 2 * seq_len), lambda b, r: (r, 0)), w_spec]
        args = [x, mod, _channel_table(), _position_table(t, t, seq_len), w]
        if ada is not None:
            cond, w_ada, b_ada = ada
            cols_per_layer = N_MOD * D_MODEL // ADA_SHARE_COLS
            n_share = (DEPTH - 1) * cols_per_layer
            assert batch >= n_share

            def share(b, r):
                s = jnp.minimum(b, n_share - 1)
                return s // cols_per_layer, s % cols_per_layer

            in_specs += [
                pl.BlockSpec((SUBLANES, D_MODEL), lambda b, r: (0, 0)),
                pl.BlockSpec((None, D_MODEL, ADA_SHARE_COLS), lambda b, r: (1 + share(b, r)[0], 0, share(b, r)[1])),
                pl.BlockSpec((None, 1, ADA_SHARE_COLS), lambda b, r: (1 + share(b, r)[0], 0, share(b, r)[1])),
            ]
            args += [cond, w_ada, b_ada]
            common["out_specs"] = [common["out_specs"],
                                   pl.BlockSpec((None, SUBLANES, ADA_SHARE_COLS),
                                                lambda b, r: (share(b, r)[0], 0, share(b, r)[1]))]
            common["out_shape"] = [common["out_shape"],
                                   jax.ShapeDtypeStruct((DEPTH - 1, SUBLANES, N_MOD * D_MODEL), F32)]
        return pl.pallas_call(
            functools.partial(_fnet_body, seq_len, ada is not None),
            in_specs=in_specs,
            scratch_shapes=[pltpu.VMEM((2 * seq_len, D_MODEL), BF16)],
            **common,
        )(*args)
    assert ada is None
    half_block = ROW_BLOCK // 2
    u = np.arange(seq_len // 2)
    i = np.arange(half_block)
    perm = np.zeros((ROW_BLOCK, ROW_BLOCK), np.float32)
    perm[2 * i, i] = 1.0
    perm[2 * i + 1, half_block + i] = 1.0
    table_spec = pl.BlockSpec((half_block, seq_len), lambda b, r: (r, 0))
    return pl.pallas_call(
        functools.partial(_fnet_split_body, seq_len),
        in_specs=[x_spec, mod_spec, csc_spec, table_spec, table_spec,
                  pl.BlockSpec((ROW_BLOCK, ROW_BLOCK), lambda b, r: (0, 0)), w_spec],
        scratch_shapes=[pltpu.VMEM((seq_len, D_MODEL), BF16)] * 2,
        **common,
    )(x, mod, _channel_table(), _position_table(2 * u, u, seq_len), _position_table(2 * u + 1, u, seq_len),
      jnp.asarray(perm, BF16), w)


def _pool_bands():
    i = np.arange(ROW_BLOCK)[:, None]
    c = np.arange(2 * ROW_BLOCK)[None, :] - POOL_HALO
    bands = [((c >= i - w // 2) & (c < i - w // 2 + w)) for w in POOL_WINDOWS]
    return jnp.asarray(np.stack(bands), BF16)


def _pool_body(seq_len, n_seq, x_ref, mod_ref, band_ref, w_ref, ps_ref, o_ref, h_ref, hb_ref):
    r = pl.program_id(1)
    mod = mod_ref[...]
    n_blocks = seq_len // ROW_BLOCK
    padded = seq_len + 2 * POOL_HALO

    @pl.when(r == 0)
    def _():
        zeros = jnp.zeros((POOL_HALO, D_MODEL), BF16)
        for s in range(n_seq):
            hb_ref[s * padded:s * padded + POOL_HALO, :] = zeros
            hb_ref[(s + 1) * padded - POOL_HALO:(s + 1) * padded, :] = zeros
            for j in range(n_blocks):
                rows = slice(s * seq_len + j * ROW_BLOCK, s * seq_len + (j + 1) * ROW_BLOCK)
                h = _modulate(x_ref[rows, :], mod, 0)
                h_ref[rows, :] = h
                start = s * padded + POOL_HALO + j * ROW_BLOCK
                hb_ref[start:start + ROW_BLOCK, :] = h.astype(BF16)

    gate = _gate(mod, 0)
    t = r * ROW_BLOCK + lax.broadcasted_iota(jnp.int32, (ROW_BLOCK, POOL_GROUP), 0)
    for s in range(n_seq):
        rows = pl.ds(pl.multiple_of(s * seq_len + r * ROW_BLOCK, ROW_BLOCK), ROW_BLOCK)
        slab = pl.ds(pl.multiple_of(s * padded + r * ROW_BLOCK, ROW_BLOCK), 2 * ROW_BLOCK)
        for g, w in enumerate(POOL_WINDOWS):
            cols = slice(g * POOL_GROUP, (g + 1) * POOL_GROUP)
            total = _dot(band_ref[g], hb_ref[slab, cols])
            cnt = jnp.minimum(t - w // 2 + w, seq_len) - jnp.maximum(t - w // 2, 0)
            pooled = total / cnt.astype(F32) - h_ref[rows, cols]
            y = _dot(pooled.astype(BF16), w_ref[g]) * ps_ref[:, cols]
            o_ref[s * ROW_BLOCK:(s + 1) * ROW_BLOCK, cols] = x_ref[rows, cols] + gate[:, cols] * y


def _pool_mix(x, mod, seq_len, w_pool, pool_scale):
    n = x.shape[0]
    batch = n // seq_len
    n_blocks = seq_len // ROW_BLOCK
    assert POOL_HALO >= max(POOL_WINDOWS) // 2 and 2 * POOL_HALO == ROW_BLOCK
    n_seq = POOL_SEQS if (n_blocks == 1 and mod.shape[0] == 1 and batch % POOL_SEQS == 0) else 1
    return pl.pallas_call(
        functools.partial(_pool_body, seq_len, n_seq),
        grid=(batch // n_seq, n_blocks),
        in_specs=[
            pl.BlockSpec((n_seq * seq_len, D_MODEL), lambda b, r: (b, 0)),
            pl.BlockSpec((None, 1, N_MOD * D_MODEL),
                         (lambda b, r: (b, 0, 0)) if mod.shape[0] > 1 else (lambda b, r: (0, 0, 0))),
            pl.BlockSpec((len(POOL_WINDOWS), ROW_BLOCK, 2 * ROW_BLOCK), lambda b, r: (0, 0, 0)),
            pl.BlockSpec((len(POOL_WINDOWS), POOL_GROUP, POOL_GROUP), lambda b, r: (0, 0, 0)),
            pl.BlockSpec((1, D_MODEL), lambda b, r: (0, 0)),
        ],
        out_specs=pl.BlockSpec((n_seq * ROW_BLOCK, D_MODEL), lambda b, r: (b * n_blocks + r, 0)),
        out_shape=jax.ShapeDtypeStruct((n, D_MODEL), F32),
        scratch_shapes=[pltpu.VMEM((n_seq * seq_len, D_MODEL), F32),
                        pltpu.VMEM((n_seq * (seq_len + 2 * POOL_HALO), D_MODEL), BF16)],
        compiler_params=_cparams(2),
        name="pool_mix",
    )(x, mod, _pool_bands(), w_pool, pool_scale)


def _rope_pair(a, cs):
    t = a * cs
    return t + pltpu.roll(t, QK_ROPE, 1)


def _store_keys(k_ref, k_nope, k_rope):
    for hd in range(N_HEADS):
        k_ref[:, hd * QK_TILE:hd * QK_TILE + QK_NOPE] = k_nope[:, hd * QK_NOPE:(hd + 1) * QK_NOPE]
        k_ref[:, hd * QK_TILE + QK_NOPE:(hd + 1) * QK_TILE] = k_rope


def _mla_proj_body(rope, state, *refs):
    refs = list(refs)
    x_ref, mod_ref = refs[:2]
    del refs[:2]
    cs_ref = refs.pop(0) if rope else None
    wdq_ref, gq_ref, wuq_ref, wdkv_ref, gkv_ref, wukv_ref = refs[:6]
    del refs[:6]
    q_ref, k_ref, v_ref = refs[:3]
    del refs[:3]
    h = _modulate(x_ref[...], mod_ref[...], 0).astype(BF16)
    cq = (_rms(_dot(h, wdq_ref[...])) * gq_ref[...]).astype(BF16)
    q = _dot(cq, wuq_ref[...]) * SM_SCALE
    n_nope = N_HEADS * QK_NOPE
    for hd in range(N_HEADS):
        a = q[:, n_nope + hd * HEAD_LANES:n_nope + (hd + 1) * HEAD_LANES]
        if rope:
            a = _rope_pair(a, cs_ref[...])
        q_ref[:, hd * QK_TILE:hd * QK_TILE + QK_NOPE] = q[:, hd * QK_NOPE:(hd + 1) * QK_NOPE].astype(BF16)
        q_ref[:, hd * QK_TILE + QK_NOPE:(hd + 1) * QK_TILE] = a.astype(BF16)
    ckv = _dot(h, wdkv_ref[...])
    c = _rms(ckv[:, :KV_RANK]) * gkv_ref[...]
    kp = ckv[:, KV_RANK:]
    if state:
        ckv_out_ref, kpe_out_ref = refs
        ckv_out_ref[...] = c
        kpe_out_ref[...] = kp[:, :QK_ROPE]
    if rope:
        kp = _rope_pair(kp, cs_ref[...])
    lane = lax.broadcasted_iota(jnp.int32, kp.shape, 1)
    kr = jnp.where(lane < QK_ROPE, kp, 0.0).astype(BF16)
    kv = _dot(c.astype(BF16), wukv_ref[...])
    _store_keys(k_ref, kv[:, :n_nope].astype(BF16), kr)
    v_ref[...] = kv[:, n_nope:].astype(BF16)


def _mla_project(x, mod, seq_len, prm, cs=None, state=False):
    n = x.shape[0]
    rope = cs is not None
    tile = min(MLA_TILE, seq_len) if (rope or mod.shape[0] > 1) else MLA_TILE
    tiles_per_seq = max(seq_len // tile, 1)
    wide = N_HEADS * QK_NOPE

    def const(shape):
        return pl.BlockSpec(shape, lambda i: (0,) * len(shape))

    def rows(width):
        return pl.BlockSpec((tile, width), lambda i: (i, 0))

    in_specs = [rows(D_MODEL),
                pl.BlockSpec((None, 1, N_MOD * D_MODEL),
                             (lambda i: (i // tiles_per_seq, 0, 0)) if mod.shape[0] > 1
                             else (lambda i: (0, 0, 0)))]
    args = [x, mod]
    if rope:
        in_specs.append(pl.BlockSpec((tile, HEAD_LANES), lambda i: (i % tiles_per_seq, 0)))
        args.append(cs)
    for name in ("w_dq", "g_q", "w_uq", "w_dkv", "g_kv", "w_ukv"):
        in_specs.append(const(prm[name].shape))
        args.append(prm[name])
    out_specs = [rows(N_HEADS * QK_TILE), rows(N_HEADS * QK_TILE), rows(wide)]
    out_shape = [jax.ShapeDtypeStruct((n, N_HEADS * QK_TILE), BF16),
                 jax.ShapeDtypeStruct((n, N_HEADS * QK_TILE), BF16),
                 jax.ShapeDtypeStruct((n, wide), BF16)]
    if state:
        out_specs += [rows(KV_RANK), rows(QK_ROPE)]
        out_shape += [jax.ShapeDtypeStruct((n, KV_RANK), F32), jax.ShapeDtypeStruct((n, QK_ROPE), F32)]
    return pl.pallas_call(
        functools.partial(_mla_proj_body, rope, state),
        grid=(n // tile,),
        in_specs=in_specs,
        out_specs=out_specs,
        out_shape=out_shape,
        compiler_params=_cparams(1),
        name="mla_project",
    )(*args)


def _ctx_expand_body(c_ref, kr_ref, w_ref, k_ref, v_ref):
    kv = _dot(c_ref[...].astype(BF16), w_ref[...])
    n_nope = N_HEADS * QK_NOPE
    _store_keys(k_ref, kv[:, :n_nope].astype(BF16), kr_ref[...])
    v_ref[...] = kv[:, n_nope:].astype(BF16)


def _ctx_expand(c_kv, k_rope, w_ukv):
    n = c_kv.shape[0]
    wide = N_HEADS * QK_NOPE
    return pl.pallas_call(
        _ctx_expand_body,
        grid=(n // ROW_BLOCK,),
        in_specs=[pl.BlockSpec((ROW_BLOCK, KV_RANK), lambda i: (i, 0)),
                  pl.BlockSpec((ROW_BLOCK, HEAD_LANES), lambda i: (i, 0)),
                  pl.BlockSpec(w_ukv.shape, lambda i: (0, 0))],
        out_specs=[pl.BlockSpec((ROW_BLOCK, N_HEADS * QK_TILE), lambda i: (i, 0)),
                   pl.BlockSpec((ROW_BLOCK, wide), lambda i: (i, 0))],
        out_shape=[jax.ShapeDtypeStruct((n, N_HEADS * QK_TILE), BF16), jax.ShapeDtypeStruct((n, wide), BF16)],
        compiler_params=_cparams(1),
        name="mla_ctx_expand",
    )(c_kv, k_rope, w_ukv)


def _attn_body(has_ctx, *refs):
    refs = list(refs)
    q_ref, k_ref, v_ref = refs[:3]
    del refs[:3]
    if has_ctx:
        kc_ref, vc_ref = refs[:2]
        del refs[:2]
    x_ref, mod_ref, wo_ref, o_ref, heads_ref = refs
    for hd in range(N_HEADS):
        qk = slice(hd * QK_TILE, (hd + 1) * QK_TILE)
        vcols = slice(hd * V_HEAD, (hd + 1) * V_HEAD)
        q = q_ref[:, qk]
        s = _dot_nt(q, k_ref[:, qk])
        m = jnp.max(s, axis=-1, keepdims=True)
        if has_ctx:
            sc = _dot_nt(q, kc_ref[:, qk])
            m = jnp.maximum(m, jnp.max(sc, axis=-1, keepdims=True))
        p = jnp.exp(s - m)
        l = jnp.sum(p, axis=-1, keepdims=True)
        o = _dot(p.astype(BF16), v_ref[:, vcols])
        if has_ctx:
            pc = jnp.exp(sc - m)
            l = l + jnp.sum(pc, axis=-1, keepdims=True)
            o = o + _dot(pc.astype(BF16), vc_ref[:, vcols])
        heads_ref[:, vcols] = (o / l).astype(BF16)
    y = _dot(heads_ref[...], wo_ref[...])
    o_ref[...] = x_ref[...] + _gate(mod_ref[...], 0) * y


def _mla_attend(x, mod, seq_len, q, k, v, w_o, ctx=None):
    n = x.shape[0]
    tq = min(ATT_Q_TILE, seq_len)
    q_tiles = seq_len // tq
    wide = N_HEADS * QK_NOPE
    has_ctx = ctx is not None

    def q_rows(width):
        return pl.BlockSpec((tq, width), lambda b, r: (b * q_tiles + r, 0))

    def seq_rows(length, width):
        return pl.BlockSpec((length, width), lambda b, r: (b, 0))

    in_specs = [q_rows(N_HEADS * QK_TILE), seq_rows(seq_len, N_HEADS * QK_TILE), seq_rows(seq_len, wide)]
    args = [q, k, v]
    if has_ctx:
        kc, vc = ctx
        past = kc.shape[0] // (n // seq_len)
        in_specs += [seq_rows(past, N_HEADS * QK_TILE), seq_rows(past, wide)]
        args += [kc, vc]
    in_specs += [q_rows(D_MODEL),
                 pl.BlockSpec((None, 1, N_MOD * D_MODEL),
                              (lambda b, r: (b, 0, 0)) if mod.shape[0] > 1 else (lambda b, r: (0, 0, 0))),
                 pl.BlockSpec(w_o.shape, lambda b, r: (0, 0))]
    args += [x, mod, w_o]
    return pl.pallas_call(
        functools.partial(_attn_body, has_ctx),
        grid=(n // seq_len, q_tiles),
        in_specs=in_specs,
        out_specs=q_rows(D_MODEL),
        out_shape=jax.ShapeDtypeStruct((n, D_MODEL), F32),
        scratch_shapes=[pltpu.VMEM((tq, wide), BF16)],
        compiler_params=_cparams(2),
        name="mla_attend",
    )(*args)


def _rope_table(n_tok):
    rows = n_tok // GRID_W
    row = np.repeat(np.arange(rows), GRID_W).astype(np.float32)
    col = np.tile(np.arange(GRID_W), rows).astype(np.float32)
    inv = (1.0 / (ROPE_BASE ** (np.arange(0, ROPE_AXIS, 2, dtype=np.float32) / ROPE_AXIS))).astype(np.float32)
    ang_r = row[:, None] * inv
    ang_c = col[:, None] * inv
    cos = np.concatenate([np.cos(ang_r), np.cos(ang_r), np.cos(ang_c), np.cos(ang_c)], axis=1)
    sin = np.concatenate([np.sin(ang_r), np.sin(ang_r), np.sin(ang_c), np.sin(ang_c)], axis=1)
    return jnp.asarray(np.concatenate([cos, sin], axis=1), F32)


def _swap_cols(w):
    half = ROPE_AXIS // 2
    return jnp.concatenate([-w[..., half:2 * half], w[..., 0:half],
                            -w[..., 3 * half:4 * half], w[..., 2 * half:3 * half]], axis=-1)


def _mla_weights(w_dq, g_q, w_uq, w_dkv, g_kv, w_ukv, w_o):
    wq = w_uq.reshape(Q_RANK, N_HEADS, QK_NOPE + QK_ROPE)
    wq_pe = wq[:, :, QK_NOPE:]
    wq_pairs = jnp.concatenate([wq_pe, _swap_cols(wq_pe)], axis=-1)
    w_uq2 = jnp.concatenate([wq[:, :, :QK_NOPE].reshape(Q_RANK, -1), wq_pairs.reshape(Q_RANK, -1)], axis=1)
    w_pe = w_dkv[:, KV_RANK:]
    w_dkv2 = jnp.concatenate([w_dkv[:, :KV_RANK], w_pe, _swap_cols(w_pe)], axis=1)
    wkv = w_ukv.reshape(KV_RANK, N_HEADS, QK_NOPE + V_HEAD)
    w_ukv2 = jnp.concatenate([wkv[:, :, :QK_NOPE].reshape(KV_RANK, -1),
                              wkv[:, :, QK_NOPE:].reshape(KV_RANK, -1)], axis=1)
    return {"w_dq": w_dq.astype(BF16), "g_q": g_q.reshape(1, Q_RANK), "w_uq": w_uq2.astype(BF16),
            "w_dkv": w_dkv2.astype(BF16), "g_kv": g_kv.reshape(1, KV_RANK), "w_ukv": w_ukv2.astype(BF16),
            "w_o": w_o.astype(BF16)}


def _cast_body(a_ref, b_ref, ao_ref, bo_ref):
    ao_ref[...] = a_ref[...].astype(BF16)
    bo_ref[...] = b_ref[...].astype(BF16)


def _ffn_first_weights(w_in, w_out):
    steps = D_FF // ROW_BLOCK
    cols_in = 2 * D_FF // steps
    return pl.pallas_call(
        _cast_body,
        grid=(steps,),
        in_specs=[pl.BlockSpec((None, D_MODEL, cols_in), lambda c: (0, 0, c)),
                  pl.BlockSpec((None, ROW_BLOCK, D_MODEL), lambda c: (0, c, 0))],
        out_specs=[pl.BlockSpec((D_MODEL, cols_in), lambda c: (0, c)),
                   pl.BlockSpec((ROW_BLOCK, D_MODEL), lambda c: (c, 0))],
        out_shape=[jax.ShapeDtypeStruct(w_in.shape[1:], BF16), jax.ShapeDtypeStruct(w_out.shape[1:], BF16)],
        compiler_params=_cparams(1),
        name="ffn_weight_cast",
    )(w_in, w_out)


def kernel(x_prompt, x_sample, cache_ckv, cache_kpe, c, c_ctx, w_ada, b_ada, w_fnet, w_pool, pool_scale,
           w_dq, g_q, w_uq, w_dkv, g_kv, w_ukv, w_o, w_ffn_in, w_ffn_conv, b_ffn_conv, w_ffn_out, g_final):
    batch, seq, _ = x_prompt.shape
    dec_batch, dec_seq, _ = x_sample.shape
    past = cache_ckv.shape[2]

    cond = jnp.concatenate([c_ctx[None, :], c, jnp.zeros((SUBLANES - 1 - dec_batch, D_MODEL), F32)], axis=0)
    b_ada = b_ada.reshape(DEPTH, 1, N_MOD * D_MODEL)
    mods = _ada_first_layer(cond, w_ada, b_ada)
    w_in_bf, w_out_bf = _ffn_first_weights(w_ffn_in, w_ffn_out)
    b_conv = b_ffn_conv[:, None, :]

    xp = x_prompt.reshape(batch * seq, D_MODEL)
    xs = x_sample.reshape(dec_batch * dec_seq, D_MODEL)
    states = []
    j_f = j_p = j_a = 0
    for i in range(DEPTH):
        mod_p = mods[0:1].reshape(1, 1, N_MOD * D_MODEL)
        mod_s = mods[1:1 + dec_batch].reshape(dec_batch, 1, N_MOD * D_MODEL)
        kind = i % N_MIXERS
        if kind == 0:
            w = w_fnet[j_f].astype(BF16)
            if i == 0:
                xp, later_mods = _fourier_mix(xp, mod_p, seq, w, ada=(cond, w_ada, b_ada))
            else:
                xp = _fourier_mix(xp, mod_p, seq, w)
            xs = _fourier_mix(xs, mod_s, dec_seq, w)
            j_f += 1
        elif kind == 1:
            w = w_pool[j_p].astype(BF16)
            ps = pool_scale[j_p].reshape(1, D_MODEL)
            xp = _pool_mix(xp, mod_p, seq, w, ps)
            xs = _pool_mix(xs, mod_s, dec_seq, w, ps)
            j_p += 1
        else:
            prm = _mla_weights(w_dq[j_a], g_q[j_a], w_uq[j_a], w_dkv[j_a], g_kv[j_a], w_ukv[j_a], w_o[j_a])
            q, k, v, ckv_p, kpe_p = _mla_project(xp, mod_p, seq, prm, state=True)
            xp = _mla_attend(xp, mod_p, seq, q, k, v, prm["w_o"])
            states.append((ckv_p.reshape(batch, seq, KV_RANK), kpe_p.reshape(batch, seq, QK_ROPE)))
            q, k, v = _mla_project(xs, mod_s, dec_seq, prm, cs=_rope_table(dec_seq))
            kpe_c = cache_kpe[:, j_a].reshape(dec_batch * past, QK_ROPE)
            krc = jnp.concatenate([kpe_c, jnp.zeros_like(kpe_c)], axis=1).astype(BF16)
            kc, vc = _ctx_expand(cache_ckv[:, j_a].reshape(dec_batch * past, KV_RANK), krc, prm["w_ukv"])
            xs = _mla_attend(xs, mod_s, dec_seq, q, k, v, prm["w_o"], ctx=(kc, vc))
            j_a += 1
        g_fin = g_final.reshape(1, D_MODEL) if i == DEPTH - 1 else None
        ffn_w = (w_in_bf, w_ffn_conv, b_conv, w_out_bf)
        if i + 1 < DEPTH:
            xp, w_in_bf, w_out_bf = _conv_ffn(xp, mod_p, seq, i, *ffn_w, cast_next=(w_ffn_in, w_ffn_out))
        else:
            xp, = _conv_ffn(xp, mod_p, seq, i, *ffn_w, g_final=g_fin)
        xs, = _conv_ffn(xs, mod_s, dec_seq, i, *ffn_w, g_final=g_fin)
        if i + 1 < DEPTH:
            mods = later_mods[i]

    state_ckv = jnp.stack([s[0] for s in states], axis=1)
    state_kpe = jnp.stack([s[1] for s in states], axis=1)
    return (xp.reshape(batch, seq, D_MODEL), xs.reshape(dec_batch, dec_seq, D_MODEL), state_ckv, state_kpe)
```

```python
import functools

import numpy as np
import jax
import jax.numpy as jnp
from jax import lax
from jax.experimental import pallas as pl
from jax.experimental.pallas import tpu as pltpu

F32 = jnp.float32
BF16 = jnp.bfloat16

D_MODEL = 1024
DEPTH = 4
GRID_W = 64
N_MIXERS = 3
N_FOURIER_GROUPS = 4
FOURIER_GROUP = D_MODEL // N_FOURIER_GROUPS
POOL_WINDOWS = (2, 4, 8, 16)
POOL_GROUP = D_MODEL // len(POOL_WINDOWS)
N_HEADS = 8
QK_NOPE = 128
QK_ROPE = 64
V_HEAD = 128
Q_RANK = 768
KV_RANK = 512
ROPE_AXIS = QK_ROPE // 2
ROPE_BASE = 10000.0
SM_SCALE = (QK_NOPE + QK_ROPE) ** -0.5
D_FF = 2816
N_MOD = 6
EPS = 1e-6

SUBLANES = 8
LANES = 128
BF16_SUBLANES = 16
VMEM_LIMIT_BYTES = 56 * 1024 * 1024

ROW_BLOCK = 256
FFN_CHUNK = 512
FFN_TILE = 512
MLA_TILE = 512
ATT_Q_TILE = 256
ADA_COLS = 2048
ADA_SHARE_COLS = 1024
POOL_HALO = 128
POOL_SEQS = 4
HEAD_LANES = 2 * QK_ROPE
QK_TILE = QK_NOPE + HEAD_LANES


def _cparams(n_axes):
    return pltpu.CompilerParams(dimension_semantics=("arbitrary",) * n_axes,
                                vmem_limit_bytes=VMEM_LIMIT_BYTES)


def _rms(x):
    return x * lax.rsqrt(jnp.mean(x * x, axis=-1, keepdims=True) + EPS)


def _modulate(x, mod, sub):
    shift = mod[:, (3 * sub) * D_MODEL:(3 * sub + 1) * D_MODEL]
    scale = mod[:, (3 * sub + 1) * D_MODEL:(3 * sub + 2) * D_MODEL]
    return _rms(x) * (1.0 + scale) + shift


def _gate(mod, sub):
    return mod[:, (3 * sub + 2) * D_MODEL:(3 * sub + 3) * D_MODEL]


def _dot(a, b):
    return jnp.dot(a, b, preferred_element_type=F32)


def _dot_nt(a, b):
    return lax.dot_general(a, b, (((1,), (1,)), ((), ())), preferred_element_type=F32)


def _ada_body(cond_ref, w_ref, b_ref, o_ref):
    a = jax.nn.silu(cond_ref[...]).astype(BF16)
    o_ref[...] = _dot(a, w_ref[...].astype(BF16)) + b_ref[...]


def _ada_first_layer(cond, w_ada, b_ada):
    n_out = N_MOD * D_MODEL
    return pl.pallas_call(
        _ada_body,
        grid=(n_out // ADA_COLS,),
        in_specs=[
            pl.BlockSpec((SUBLANES, D_MODEL), lambda j: (0, 0)),
            pl.BlockSpec((None, D_MODEL, ADA_COLS), lambda j: (0, 0, j)),
            pl.BlockSpec((None, 1, ADA_COLS), lambda j: (0, 0, j)),
        ],
        out_specs=pl.BlockSpec((SUBLANES, ADA_COLS), lambda j: (0, j)),
        out_shape=jax.ShapeDtypeStruct((SUBLANES, n_out), F32),
        compiler_params=_cparams(1),
        name="ada_mod",
    )(cond, w_ada, b_ada)


def _ffn_chunks():
    edges = list(range(0, D_FF, FFN_CHUNK)) + [D_FF]
    return list(zip(edges[:-1], edges[1:]))


def _ffn_body(seq_len, n_seg, seg, final, cast_next, ada_later, *refs):
    refs = list(refs)
    x_ref, xp_ref, xn_ref, mod_ref, win_ref, wc_ref, bc_ref, wout_ref = refs[:8]
    del refs[:8]
    gfin_ref = refs.pop(0) if final else None
    side_in = [refs.pop(0) for _ in range(2 * cast_next + 3 * ada_later)]
    o_ref = refs.pop(0)
    side_out = [refs.pop(0) for _ in range(2 * cast_next + ada_later)]
    if cast_next:
        side_out[0][...] = side_in[0][...].astype(BF16)
        side_out[1][...] = side_in[1][...].astype(BF16)
    if ada_later:
        _ada_body(*side_in[-3:], side_out[-1])
    hext_ref, uga_ref, uva_ref, ugb_ref, uvb_ref, acta_ref, actb_ref, acc_ref = refs
    tile = n_seg * seg
    gap = BF16_SUBLANES
    i = pl.program_id(0)
    mod = mod_ref[...]
    x = x_ref[...]

    start = i * tile
    prev_ok = (start % seq_len != 0).astype(F32)
    next_ok = ((start + tile) % seq_len != 0).astype(F32)
    zeros8 = jnp.zeros((SUBLANES, D_MODEL), F32)
    h_prev = _modulate(xp_ref[...], mod, 1) * prev_ok
    h_next = _modulate(xn_ref[...], mod, 1) * next_ok
    hext_ref[0:gap, :] = jnp.concatenate([zeros8, h_prev], axis=0).astype(BF16)
    h = _modulate(x, mod, 1).astype(BF16)
    for k in range(n_seg):
        base = gap + k * (seg + gap)
        hext_ref[base:base + seg, :] = h[k * seg:(k + 1) * seg, :]
        if k == n_seg - 1:
            tail = jnp.concatenate([h_next, zeros8], axis=0).astype(BF16)
        else:
            tail = jnp.zeros((gap, D_MODEL), BF16)
        hext_ref[base + seg:base + seg + gap, :] = tail
    acc_ref[...] = jnp.zeros_like(acc_ref)

    def up(cols, ug_ref, uv_ref):
        lo, hi = cols
        ug_ref[:, :hi - lo] = _dot(hext_ref[...], win_ref[:, lo:hi])
        uv_ref[:, :hi - lo] = _dot(hext_ref[...], win_ref[:, D_FF + lo:D_FF + hi])

    def conv(u_ref, b, lo, hi):
        wc = wc_ref[:, lo:hi]
        return (u_ref[b - 1:b - 1 + seg, :hi - lo] * wc[0:1, :] + u_ref[b:b + seg, :hi - lo] * wc[1:2, :]
                + u_ref[b + 1:b + 1 + seg, :hi - lo] * wc[2:3, :] + bc_ref[:, lo:hi])

    def down(cols, ug_ref, uv_ref, act_ref):
        lo, hi = cols
        for k in range(n_seg):
            b = gap + k * (seg + gap)
            a = jax.nn.silu(conv(ug_ref, b, lo, hi)) * conv(uv_ref, b, D_FF + lo, D_FF + hi)
            act_ref[k * seg:(k + 1) * seg, :hi - lo] = a.astype(BF16)
        acc_ref[...] += _dot(act_ref[:, :hi - lo], wout_ref[lo:hi, :])

    chunks = _ffn_chunks()
    u_bufs = ((uga_ref, uva_ref), (ugb_ref, uvb_ref))
    act_bufs = (acta_ref, actb_ref)
    up(chunks[0], *u_bufs[0])
    for c, cols in enumerate(chunks):
        if c + 1 < len(chunks):
            up(chunks[c + 1], *u_bufs[(c + 1) % 2])
        down(cols, *u_bufs[c % 2], act_bufs[c % 2])
    y = x + _gate(mod, 1) * acc_ref[...]
    if final:
        y = _rms(y) * gfin_ref[...]
    o_ref[...] = y


def _conv_ffn(x, mod, seq_len, layer, w_in, w_conv, b_conv, w_out, g_final=None, cast_next=None,
              ada_later=None):
    n = x.shape[0]
    tile = FFN_TILE
    seg = min(seq_len, tile)
    n_seg = tile // seg
    rows = BF16_SUBLANES + n_seg * (seg + BF16_SUBLANES)
    per_seq_mod = mod.shape[0] > 1
    final = g_final is not None
    blocks8 = n // SUBLANES
    t8 = tile // SUBLANES

    def resident(a):
        return pl.BlockSpec(a.shape, lambda i: (0,) * a.ndim, pipeline_mode=pl.Buffered(1))

    def resident_layer(a):
        shape = a.shape[1:]
        return pl.BlockSpec((None,) + shape, lambda i: (layer,) + (0,) * len(shape),
                            pipeline_mode=pl.Buffered(1))

    in_specs = [
        pl.BlockSpec((tile, D_MODEL), lambda i: (i, 0)),
        pl.BlockSpec((SUBLANES, D_MODEL), lambda i: (jnp.maximum(i * t8 - 1, 0), 0)),
        pl.BlockSpec((SUBLANES, D_MODEL), lambda i: (jnp.minimum((i + 1) * t8, blocks8 - 1), 0)),
        pl.BlockSpec((None, 1, N_MOD * D_MODEL),
                     (lambda i: ((i * tile) // seq_len, 0, 0)) if per_seq_mod else (lambda i: (0, 0, 0))),
        resident(w_in), resident_layer(w_conv), resident_layer(b_conv), resident(w_out),
    ]
    args = [x, x, x, mod, w_in, w_conv, b_conv, w_out]
    if final:
        in_specs.append(pl.BlockSpec((1, D_MODEL), lambda i: (0, 0)))
        args.append(g_final)
    out_specs = [pl.BlockSpec((tile, D_MODEL), lambda i: (i, 0))]
    out_shape = [jax.ShapeDtypeStruct((n, D_MODEL), F32)]
    if cast_next is not None:
        steps = n // tile
        for a in cast_next:
            rows_step = a.shape[1] // steps
            assert rows_step * steps == a.shape[1] and rows_step % BF16_SUBLANES == 0
            in_specs.append(pl.BlockSpec((None, rows_step, a.shape[2]), lambda i: (layer + 1, i, 0)))
            out_specs.append(pl.BlockSpec((rows_step, a.shape[2]), lambda i: (i, 0)))
            out_shape.append(jax.ShapeDtypeStruct(a.shape[1:], BF16))
            args.append(a)
    if ada_later is not None:
        cond, w_ada, b_ada, layers = ada_later
        blocks_per_layer = N_MOD * D_MODEL // ADA_SHARE_COLS
        n_share = len(layers) * blocks_per_layer
        assert n_share <= n // tile and list(layers) == list(range(layers[0], layers[0] + len(layers)))

        def share(i):
            s = jnp.minimum(i, n_share - 1)
            return s // blocks_per_layer, s % blocks_per_layer

        in_specs += [
            pl.BlockSpec((SUBLANES, D_MODEL), lambda i: (0, 0)),
            pl.BlockSpec((None, D_MODEL, ADA_SHARE_COLS), lambda i: (layers[0] + share(i)[0], 0, share(i)[1])),
            pl.BlockSpec((None, 1, ADA_SHARE_COLS), lambda i: (layers[0] + share(i)[0], 0, share(i)[1])),
        ]
        args += [cond, w_ada, b_ada]
        out_specs.append(pl.BlockSpec((None, SUBLANES, ADA_SHARE_COLS), lambda i: (share(i)[0], 0, share(i)[1])))
        out_shape.append(jax.ShapeDtypeStruct((len(layers), SUBLANES, N_MOD * D_MODEL), F32))
    return pl.pallas_call(
        functools.partial(_ffn_body, seq_len, n_seg, seg, final, cast_next is not None, ada_later is not None),
        grid=(n // tile,),
        in_specs=in_specs,
        out_specs=out_specs,
        out_shape=out_shape,
        scratch_shapes=[pltpu.VMEM((rows, D_MODEL), BF16)]
        + [pltpu.VMEM((rows, FFN_CHUNK), F32)] * 4
        + [pltpu.VMEM((tile, FFN_CHUNK), BF16)] * 2
        + [pltpu.VMEM((tile, D_MODEL), F32)],
        compiler_params=_cparams(1),
        name="conv_ffn",
    )(*args)


def _fnet_body(seq_len, x_ref, mod_ref, csc_ref, csrow_ref, w_ref, o_ref, stack_ref):
    r = pl.program_id(1)
    mod = mod_ref[...]

    @pl.when(r == 0)
    def _():
        def rows(j, carry):
            r0 = pl.multiple_of(j * ROW_BLOCK, ROW_BLOCK)
            h = _modulate(x_ref[pl.ds(r0, ROW_BLOCK), :], mod, 0).astype(BF16)
            for g in range(N_FOURIER_GROUPS):
                cols = slice(g * FOURIER_GROUP, (g + 1) * FOURIER_GROUP)
                y = _dot(h[:, cols], csc_ref[...].astype(BF16))
                stack_ref[pl.ds(r0, ROW_BLOCK), cols] = y[:, :FOURIER_GROUP].astype(BF16)
                stack_ref[pl.ds(seq_len + r0, ROW_BLOCK), cols] = y[:, FOURIER_GROUP:].astype(BF16)
            return carry
        lax.fori_loop(0, seq_len // ROW_BLOCK, rows, 0)

    f = _dot(csrow_ref[...].astype(BF16), stack_ref[...]) * ((seq_len * FOURIER_GROUP) ** -0.5)
    y = _dot(f.astype(BF16), w_ref[...])
    r0 = pl.multiple_of(r * ROW_BLOCK, ROW_BLOCK)
    o_ref[...] = x_ref[pl.ds(r0, ROW_BLOCK), :] + _gate(mod, 0) * y


def _fnet_split_body(seq_len, x_ref, mod_ref, csc_ref, even_ref, odd_ref, perm_ref, w_ref, o_ref,
                     plus_ref, minus_ref):
    r = pl.program_id(1)
    mod = mod_ref[...]
    half = seq_len // 2

    @pl.when(r == 0)
    def _():
        def rows(j, carry):
            r0 = pl.multiple_of(j * ROW_BLOCK, ROW_BLOCK)
            h_lo = _modulate(x_ref[pl.ds(r0, ROW_BLOCK), :], mod, 0).astype(BF16)
            h_hi = _modulate(x_ref[pl.ds(half + r0, ROW_BLOCK), :], mod, 0).astype(BF16)
            table = csc_ref[...].astype(BF16)
            for g in range(N_FOURIER_GROUPS):
                cols = slice(g * FOURIER_GROUP, (g + 1) * FOURIER_GROUP)
                y_lo = _dot(h_lo[:, cols], table)
                y_hi = _dot(h_hi[:, cols], table)
                for ref, y in ((plus_ref, y_lo + y_hi), (minus_ref, y_lo - y_hi)):
                    ref[pl.ds(r0, ROW_BLOCK), cols] = y[:, :FOURIER_GROUP].astype(BF16)
                    ref[pl.ds(half + r0, ROW_BLOCK), cols] = y[:, FOURIER_GROUP:].astype(BF16)
            return carry
        lax.fori_loop(0, half // ROW_BLOCK, rows, 0)

    scale = (seq_len * FOURIER_GROUP) ** -0.5
    even = _dot(even_ref[...].astype(BF16), plus_ref[...]) * scale
    odd = _dot(odd_ref[...].astype(BF16), minus_ref[...]) * scale
    f = _dot(perm_ref[...], jnp.concatenate([even, odd], axis=0).astype(BF16))
    y = _dot(f.astype(BF16), w_ref[...])
    r0 = pl.multiple_of(r * ROW_BLOCK, ROW_BLOCK)
    o_ref[...] = x_ref[pl.ds(r0, ROW_BLOCK), :] + _gate(mod, 0) * y


def _channel_table():
    c = np.arange(FOURIER_GROUP)
    ang = 2.0 * np.pi * ((c[:, None] * c[None, :]) % FOURIER_GROUP) / FOURIER_GROUP
    return jnp.asarray(np.concatenate([np.cos(ang), np.sin(ang)], axis=1), F32)


def _position_table(freqs, positions, seq_len):
    ang = 2.0 * np.pi * ((freqs[:, None] * positions[None, :]) % seq_len) / seq_len
    return jnp.asarray(np.concatenate([np.cos(ang), -np.sin(ang)], axis=1), F32)


def _fourier_mix(x, mod, seq_len, w):
    n = x.shape[0]
    batch = n // seq_len
    n_row_blocks = seq_len // ROW_BLOCK
    mod_spec = pl.BlockSpec((None, 1, N_MOD * D_MODEL),
                            (lambda b, r: (b, 0, 0)) if mod.shape[0] > 1 else (lambda b, r: (0, 0, 0)))
    common = dict(
        grid=(batch, n_row_blocks),
        out_specs=pl.BlockSpec((ROW_BLOCK, D_MODEL), lambda b, r: (b * n_row_blocks + r, 0)),
        out_shape=jax.ShapeDtypeStruct((n, D_MODEL), F32),
        compiler_params=_cparams(2),
        name="fourier_mix",
    )
    x_spec = pl.BlockSpec((seq_len, D_MODEL), lambda b, r: (b, 0))
    csc_spec = pl.BlockSpec((FOURIER_GROUP, 2 * FOURIER_GROUP), lambda b, r: (0, 0))
    w_spec = pl.BlockSpec((D_MODEL, D_MODEL), lambda b, r: (0, 0))
    t = np.arange(seq_len)
    if n_row_blocks == 1:
        return pl.pallas_call(
            functools.partial(_fnet_body, seq_len),
            in_specs=[x_spec, mod_spec, csc_spec,
                      pl.BlockSpec((ROW_BLOCK, 2 * seq_len), lambda b, r: (r, 0)), w_spec],
            scratch_shapes=[pltpu.VMEM((2 * seq_len, D_MODEL), BF16)],
            **common,
        )(x, mod, _channel_table(), _position_table(t, t, seq_len), w)
    half_block = ROW_BLOCK // 2
    u = np.arange(seq_len // 2)
    i = np.arange(half_block)
    perm = np.zeros((ROW_BLOCK, ROW_BLOCK), np.float32)
    perm[2 * i, i] = 1.0
    perm[2 * i + 1, half_block + i] = 1.0
    table_spec = pl.BlockSpec((half_block, seq_len), lambda b, r: (r, 0))
    return pl.pallas_call(
        functools.partial(_fnet_split_body, seq_len),
        in_specs=[x_spec, mod_spec, csc_spec, table_spec, table_spec,
                  pl.BlockSpec((ROW_BLOCK, ROW_BLOCK), lambda b, r: (0, 0)), w_spec],
        scratch_shapes=[pltpu.VMEM((seq_len, D_MODEL), BF16)] * 2,
        **common,
    )(x, mod, _channel_table(), _position_table(2 * u, u, seq_len), _position_table(2 * u + 1, u, seq_len),
      jnp.asarray(perm, BF16), w)


def _pool_bands():
    i = np.arange(ROW_BLOCK)[:, None]
    c = np.arange(2 * ROW_BLOCK)[None, :] - POOL_HALO
    bands = [((c >= i - w // 2) & (c < i - w // 2 + w)) for w in POOL_WINDOWS]
    return jnp.asarray(np.stack(bands), BF16)


def _pool_body(seq_len, n_seq, x_ref, mod_ref, band_ref, w_ref, ps_ref, o_ref, h_ref, hb_ref):
    mod = mod_ref[...]
    n_blocks = seq_len // ROW_BLOCK
    padded = seq_len + 2 * POOL_HALO
    zeros = jnp.zeros((POOL_HALO, D_MODEL), BF16)
    for s in range(n_seq):
        hb_ref[s * padded:s * padded + POOL_HALO, :] = zeros
        hb_ref[(s + 1) * padded - POOL_HALO:(s + 1) * padded, :] = zeros
        for j in range(n_blocks):
            rows = slice(s * seq_len + j * ROW_BLOCK, s * seq_len + (j + 1) * ROW_BLOCK)
            h = _modulate(x_ref[rows, :], mod, 0)
            h_ref[rows, :] = h
            start = s * padded + POOL_HALO + j * ROW_BLOCK
            hb_ref[start:start + ROW_BLOCK, :] = h.astype(BF16)

    gate = _gate(mod, 0)
    for s in range(n_seq):
        for j in range(n_blocks):
            rows = slice(s * seq_len + j * ROW_BLOCK, s * seq_len + (j + 1) * ROW_BLOCK)
            slab = slice(s * padded + j * ROW_BLOCK, s * padded + (j + 2) * ROW_BLOCK)
            t = j * ROW_BLOCK + lax.broadcasted_iota(jnp.int32, (ROW_BLOCK, POOL_GROUP), 0)
            for g, w in enumerate(POOL_WINDOWS):
                cols = slice(g * POOL_GROUP, (g + 1) * POOL_GROUP)
                total = _dot(band_ref[g], hb_ref[slab, cols])
                cnt = jnp.minimum(t - w // 2 + w, seq_len) - jnp.maximum(t - w // 2, 0)
                pooled = total / cnt.astype(F32) - h_ref[rows, cols]
                y = _dot(pooled.astype(BF16), w_ref[g]) * ps_ref[:, cols]
                o_ref[rows, cols] = x_ref[rows, cols] + gate[:, cols] * y


def _pool_mix(x, mod, seq_len, w_pool, pool_scale):
    n = x.shape[0]
    batch = n // seq_len
    assert POOL_HALO >= max(POOL_WINDOWS) // 2 and 2 * POOL_HALO == ROW_BLOCK
    n_seq = POOL_SEQS if (seq_len == ROW_BLOCK and mod.shape[0] == 1 and batch % POOL_SEQS == 0) else 1
    rows = n_seq * seq_len
    return pl.pallas_call(
        functools.partial(_pool_body, seq_len, n_seq),
        grid=(batch // n_seq,),
        in_specs=[
            pl.BlockSpec((rows, D_MODEL), lambda b: (b, 0)),
            pl.BlockSpec((None, 1, N_MOD * D_MODEL),
                         (lambda b: (b, 0, 0)) if mod.shape[0] > 1 else (lambda b: (0, 0, 0))),
            pl.BlockSpec((len(POOL_WINDOWS), ROW_BLOCK, 2 * ROW_BLOCK), lambda b: (0, 0, 0)),
            pl.BlockSpec((len(POOL_WINDOWS), POOL_GROUP, POOL_GROUP), lambda b: (0, 0, 0)),
            pl.BlockSpec((1, D_MODEL), lambda b: (0, 0)),
        ],
        out_specs=pl.BlockSpec((rows, D_MODEL), lambda b: (b, 0)),
        out_shape=jax.ShapeDtypeStruct((n, D_MODEL), F32),
        scratch_shapes=[pltpu.VMEM((rows, D_MODEL), F32),
                        pltpu.VMEM((n_seq * (seq_len + 2 * POOL_HALO), D_MODEL), BF16)],
        compiler_params=_cparams(1),
        name="pool_mix",
    )(x, mod, _pool_bands(), w_pool, pool_scale)


def _rope_pair(a, cs):
    t = a * cs
    return t + pltpu.roll(t, QK_ROPE, 1)


def _store_keys(k_ref, k_nope, k_rope):
    for hd in range(N_HEADS):
        k_ref[:, hd * QK_TILE:hd * QK_TILE + QK_NOPE] = k_nope[:, hd * QK_NOPE:(hd + 1) * QK_NOPE]
        k_ref[:, hd * QK_TILE + QK_NOPE:(hd + 1) * QK_TILE] = k_rope


def _mla_proj_body(rope, state, *refs):
    refs = list(refs)
    x_ref, mod_ref = refs[:2]
    del refs[:2]
    cs_ref = refs.pop(0) if rope else None
    wdq_ref, gq_ref, wuq_ref, wdkv_ref, gkv_ref, wukv_ref = refs[:6]
    del refs[:6]
    q_ref, k_ref, v_ref = refs[:3]
    del refs[:3]
    h = _modulate(x_ref[...], mod_ref[...], 0).astype(BF16)
    cq = (_rms(_dot(h, wdq_ref[...])) * gq_ref[...]).astype(BF16)
    q = _dot(cq, wuq_ref[...]) * SM_SCALE
    n_nope = N_HEADS * QK_NOPE
    for hd in range(N_HEADS):
        a = q[:, n_nope + hd * HEAD_LANES:n_nope + (hd + 1) * HEAD_LANES]
        if rope:
            a = _rope_pair(a, cs_ref[...])
        q_ref[:, hd * QK_TILE:hd * QK_TILE + QK_NOPE] = q[:, hd * QK_NOPE:(hd + 1) * QK_NOPE].astype(BF16)
        q_ref[:, hd * QK_TILE + QK_NOPE:(hd + 1) * QK_TILE] = a.astype(BF16)
    ckv = _dot(h, wdkv_ref[...])
    c = _rms(ckv[:, :KV_RANK]) * gkv_ref[...]
    kp = ckv[:, KV_RANK:]
    if state:
        ckv_out_ref, kpe_out_ref = refs
        ckv_out_ref[...] = c
        kpe_out_ref[...] = kp[:, :QK_ROPE]
    if rope:
        kp = _rope_pair(kp, cs_ref[...])
    lane = lax.broadcasted_iota(jnp.int32, kp.shape, 1)
    kr = jnp.where(lane < QK_ROPE, kp, 0.0).astype(BF16)
    kv = _dot(c.astype(BF16), wukv_ref[...])
    _store_keys(k_ref, kv[:, :n_nope].astype(BF16), kr)
    v_ref[...] = kv[:, n_nope:].astype(BF16)


def _mla_project(x, mod, seq_len, prm, cs=None, state=False):
    n = x.shape[0]
    rope = cs is not None
    tile = min(MLA_TILE, seq_len) if (rope or mod.shape[0] > 1) else MLA_TILE
    tiles_per_seq = max(seq_len // tile, 1)
    wide = N_HEADS * QK_NOPE

    def const(shape):
        return pl.BlockSpec(shape, lambda i: (0,) * len(shape))

    def rows(width):
        return pl.BlockSpec((tile, width), lambda i: (i, 0))

    in_specs = [rows(D_MODEL),
                pl.BlockSpec((None, 1, N_MOD * D_MODEL),
                             (lambda i: (i // tiles_per_seq, 0, 0)) if mod.shape[0] > 1
                             else (lambda i: (0, 0, 0)))]
    args = [x, mod]
    if rope:
        in_specs.append(pl.BlockSpec((tile, HEAD_LANES), lambda i: (i % tiles_per_seq, 0)))
        args.append(cs)
    for name in ("w_dq", "g_q", "w_uq", "w_dkv", "g_kv", "w_ukv"):
        in_specs.append(const(prm[name].shape))
        args.append(prm[name])
    out_specs = [rows(N_HEADS * QK_TILE), rows(N_HEADS * QK_TILE), rows(wide)]
    out_shape = [jax.ShapeDtypeStruct((n, N_HEADS * QK_TILE), BF16),
                 jax.ShapeDtypeStruct((n, N_HEADS * QK_TILE), BF16),
                 jax.ShapeDtypeStruct((n, wide), BF16)]
    if state:
        out_specs += [rows(KV_RANK), rows(QK_ROPE)]
        out_shape += [jax.ShapeDtypeStruct((n, KV_RANK), F32), jax.ShapeDtypeStruct((n, QK_ROPE), F32)]
    return pl.pallas_call(
        functools.partial(_mla_proj_body, rope, state),
        grid=(n // tile,),
        in_specs=in_specs,
        out_specs=out_specs,
        out_shape=out_shape,
        compiler_params=_cparams(1),
        name="mla_project",
    )(*args)


def _ctx_expand_body(c_ref, kr_ref, w_ref, k_ref, v_ref):
    kv = _dot(c_ref[...].astype(BF16), w_ref[...])
    n_nope = N_HEADS * QK_NOPE
    _store_keys(k_ref, kv[:, :n_nope].astype(BF16), kr_ref[...])
    v_ref[...] = kv[:, n_nope:].astype(BF16)


def _ctx_expand(c_kv, k_rope, w_ukv):
    n = c_kv.shape[0]
    wide = N_HEADS * QK_NOPE
    return pl.pallas_call(
        _ctx_expand_body,
        grid=(n // ROW_BLOCK,),
        in_specs=[pl.BlockSpec((ROW_BLOCK, KV_RANK), lambda i: (i, 0)),
                  pl.BlockSpec((ROW_BLOCK, HEAD_LANES), lambda i: (i, 0)),
                  pl.BlockSpec(w_ukv.shape, lambda i: (0, 0))],
        out_specs=[pl.BlockSpec((ROW_BLOCK, N_HEADS * QK_TILE), lambda i: (i, 0)),
                   pl.BlockSpec((ROW_BLOCK, wide), lambda i: (i, 0))],
        out_shape=[jax.ShapeDtypeStruct((n, N_HEADS * QK_TILE), BF16), jax.ShapeDtypeStruct((n, wide), BF16)],
        compiler_params=_cparams(1),
        name="mla_ctx_expand",
    )(c_kv, k_rope, w_ukv)


def _attn_body(has_ctx, *refs):
    refs = list(refs)
    q_ref, k_ref, v_ref = refs[:3]
    del refs[:3]
    if has_ctx:
        kc_ref, vc_ref = refs[:2]
        del refs[:2]
    x_ref, mod_ref, wo_ref, o_ref, heads_ref = refs
    for hd in range(N_HEADS):
        qk = slice(hd * QK_TILE, (hd + 1) * QK_TILE)
        vcols = slice(hd * V_HEAD, (hd + 1) * V_HEAD)
        q = q_ref[:, qk]
        s = _dot_nt(q, k_ref[:, qk])
        m = jnp.max(s, axis=-1, keepdims=True)
        if has_ctx:
            sc = _dot_nt(q, kc_ref[:, qk])
            m = jnp.maximum(m, jnp.max(sc, axis=-1, keepdims=True))
        p = jnp.exp(s - m)
        l = jnp.sum(p, axis=-1, keepdims=True)
        o = _dot(p.astype(BF16), v_ref[:, vcols])
        if has_ctx:
            pc = jnp.exp(sc - m)
            l = l + jnp.sum(pc, axis=-1, keepdims=True)
            o = o + _dot(pc.astype(BF16), vc_ref[:, vcols])
        heads_ref[:, vcols] = (o / l).astype(BF16)
    y = _dot(heads_ref[...], wo_ref[...])
    o_ref[...] = x_ref[...] + _gate(mod_ref[...], 0) * y


def _mla_attend(x, mod, seq_len, q, k, v, w_o, ctx=None):
    n = x.shape[0]
    tq = min(ATT_Q_TILE, seq_len)
    q_tiles = seq_len // tq
    wide = N_HEADS * QK_NOPE
    has_ctx = ctx is not None

    def q_rows(width):
        return pl.BlockSpec((tq, width), lambda b, r: (b * q_tiles + r, 0))

    def seq_rows(length, width):
        return pl.BlockSpec((length, width), lambda b, r: (b, 0))

    in_specs = [q_rows(N_HEADS * QK_TILE), seq_rows(seq_len, N_HEADS * QK_TILE), seq_rows(seq_len, wide)]
    args = [q, k, v]
    if has_ctx:
        kc, vc = ctx
        past = kc.shape[0] // (n // seq_len)
        in_specs += [seq_rows(past, N_HEADS * QK_TILE), seq_rows(past, wide)]
        args += [kc, vc]
    in_specs += [q_rows(D_MODEL),
                 pl.BlockSpec((None, 1, N_MOD * D_MODEL),
                              (lambda b, r: (b, 0, 0)) if mod.shape[0] > 1 else (lambda b, r: (0, 0, 0))),
                 pl.BlockSpec(w_o.shape, lambda b, r: (0, 0))]
    args += [x, mod, w_o]
    return pl.pallas_call(
        functools.partial(_attn_body, has_ctx),
        grid=(n // seq_len, q_tiles),
        in_specs=in_specs,
        out_specs=q_rows(D_MODEL),
        out_shape=jax.ShapeDtypeStruct((n, D_MODEL), F32),
        scratch_shapes=[pltpu.VMEM((tq, wide), BF16)],
        compiler_params=_cparams(2),
        name="mla_attend",
    )(*args)


def _rope_table(n_tok):
    rows = n_tok // GRID_W
    row = np.repeat(np.arange(rows), GRID_W).astype(np.float32)
    col = np.tile(np.arange(GRID_W), rows).astype(np.float32)
    inv = (1.0 / (ROPE_BASE ** (np.arange(0, ROPE_AXIS, 2, dtype=np.float32) / ROPE_AXIS))).astype(np.float32)
    ang_r = row[:, None] * inv
    ang_c = col[:, None] * inv
    cos = np.concatenate([np.cos(ang_r), np.cos(ang_r), np.cos(ang_c), np.cos(ang_c)], axis=1)
    sin = np.concatenate([np.sin(ang_r), np.sin(ang_r), np.sin(ang_c), np.sin(ang_c)], axis=1)
    return jnp.asarray(np.concatenate([cos, sin], axis=1), F32)


def _swap_cols(w):
    half = ROPE_AXIS // 2
    return jnp.concatenate([-w[..., half:2 * half], w[..., 0:half],
                            -w[..., 3 * half:4 * half], w[..., 2 * half:3 * half]], axis=-1)


def _mla_weights(w_dq, g_q, w_uq, w_dkv, g_kv, w_ukv, w_o):
    wq = w_uq.reshape(Q_RANK, N_HEADS, QK_NOPE + QK_ROPE)
    wq_pe = wq[:, :, QK_NOPE:]
    wq_pairs = jnp.concatenate([wq_pe, _swap_cols(wq_pe)], axis=-1)
    w_uq2 = jnp.concatenate([wq[:, :, :QK_NOPE].reshape(Q_RANK, -1), wq_pairs.reshape(Q_RANK, -1)], axis=1)
    w_pe = w_dkv[:, KV_RANK:]
    w_dkv2 = jnp.concatenate([w_dkv[:, :KV_RANK], w_pe, _swap_cols(w_pe)], axis=1)
    wkv = w_ukv.reshape(KV_RANK, N_HEADS, QK_NOPE + V_HEAD)
    w_ukv2 = jnp.concatenate([wkv[:, :, :QK_NOPE].reshape(KV_RANK, -1),
                              wkv[:, :, QK_NOPE:].reshape(KV_RANK, -1)], axis=1)
    return {"w_dq": w_dq.astype(BF16), "g_q": g_q.reshape(1, Q_RANK), "w_uq": w_uq2.astype(BF16),
            "w_dkv": w_dkv2.astype(BF16), "g_kv": g_kv.reshape(1, KV_RANK), "w_ukv": w_ukv2.astype(BF16),
            "w_o": w_o.astype(BF16)}


def _cast_body(a_ref, b_ref, ao_ref, bo_ref):
    ao_ref[...] = a_ref[...].astype(BF16)
    bo_ref[...] = b_ref[...].astype(BF16)


def _ffn_first_weights(w_in, w_out):
    steps = D_FF // ROW_BLOCK
    cols_in = 2 * D_FF // steps
    return pl.pallas_call(
        _cast_body,
        grid=(steps,),
        in_specs=[pl.BlockSpec((None, D_MODEL, cols_in), lambda c: (0, 0, c)),
                  pl.BlockSpec((None, ROW_BLOCK, D_MODEL), lambda c: (0, c, 0))],
        out_specs=[pl.BlockSpec((D_MODEL, cols_in), lambda c: (0, c)),
                   pl.BlockSpec((ROW_BLOCK, D_MODEL), lambda c: (c, 0))],
        out_shape=[jax.ShapeDtypeStruct(w_in.shape[1:], BF16), jax.ShapeDtypeStruct(w_out.shape[1:], BF16)],
        compiler_params=_cparams(1),
        name="ffn_weight_cast",
    )(w_in, w_out)


def kernel(x_prompt, x_sample, cache_ckv, cache_kpe, c, c_ctx, w_ada, b_ada, w_fnet, w_pool, pool_scale,
           w_dq, g_q, w_uq, w_dkv, g_kv, w_ukv, w_o, w_ffn_in, w_ffn_conv, b_ffn_conv, w_ffn_out, g_final):
    batch, seq, _ = x_prompt.shape
    dec_batch, dec_seq, _ = x_sample.shape
    past = cache_ckv.shape[2]

    cond = jnp.concatenate([c_ctx[None, :], c, jnp.zeros((SUBLANES - 1 - dec_batch, D_MODEL), F32)], axis=0)
    b_ada = b_ada.reshape(DEPTH, 1, N_MOD * D_MODEL)
    mods = {0: _ada_first_layer(cond, w_ada, b_ada)}
    ada_layers_per_call = (batch * seq // FFN_TILE) // (N_MOD * D_MODEL // ADA_SHARE_COLS)
    w_in_bf, w_out_bf = _ffn_first_weights(w_ffn_in, w_ffn_out)
    b_conv = b_ffn_conv[:, None, :]

    xp = x_prompt.reshape(batch * seq, D_MODEL)
    xs = x_sample.reshape(dec_batch * dec_seq, D_MODEL)
    states = []
    j_f = j_p = j_a = 0
    for i in range(DEPTH):
        mod_p = mods[i][0:1].reshape(1, 1, N_MOD * D_MODEL)
        mod_s = mods[i][1:1 + dec_batch].reshape(dec_batch, 1, N_MOD * D_MODEL)
        kind = i % N_MIXERS
        if kind == 0:
            w = w_fnet[j_f].astype(BF16)
            xp = _fourier_mix(xp, mod_p, seq, w)
            xs = _fourier_mix(xs, mod_s, dec_seq, w)
            j_f += 1
        elif kind == 1:
            w = w_pool[j_p].astype(BF16)
            ps = pool_scale[j_p].reshape(1, D_MODEL)
            xp = _pool_mix(xp, mod_p, seq, w, ps)
            xs = _pool_mix(xs, mod_s, dec_seq, w, ps)
            j_p += 1
        else:
            prm = _mla_weights(w_dq[j_a], g_q[j_a], w_uq[j_a], w_dkv[j_a], g_kv[j_a], w_ukv[j_a], w_o[j_a])
            q, k, v, ckv_p, kpe_p = _mla_project(xp, mod_p, seq, prm, state=True)
            xp = _mla_attend(xp, mod_p, seq, q, k, v, prm["w_o"])
            states.append((ckv_p.reshape(batch, seq, KV_RANK), kpe_p.reshape(batch, seq, QK_ROPE)))
            q, k, v = _mla_project(xs, mod_s, dec_seq, prm, cs=_rope_table(dec_seq))
            kpe_c = cache_kpe[:, j_a].reshape(dec_batch * past, QK_ROPE)
            krc = jnp.concatenate([kpe_c, jnp.zeros_like(kpe_c)], axis=1).astype(BF16)
            kc, vc = _ctx_expand(cache_ckv[:, j_a].reshape(dec_batch * past, KV_RANK), krc, prm["w_ukv"])
            xs = _mla_attend(xs, mod_s, dec_seq, q, k, v, prm["w_o"], ctx=(kc, vc))
            j_a += 1
        g_fin = g_final.reshape(1, D_MODEL) if i == DEPTH - 1 else None
        ffn_w = (w_in_bf, w_ffn_conv, b_conv, w_out_bf)
        if i + 1 < DEPTH:
            todo = [l for l in range(i + 1, DEPTH) if l not in mods][:ada_layers_per_call]
            assert i + 1 in mods or (todo and todo[0] == i + 1)
            ada_later = (cond, w_ada, b_ada, tuple(todo)) if todo else None
            out = _conv_ffn(xp, mod_p, seq, i, *ffn_w, cast_next=(w_ffn_in, w_ffn_out), ada_later=ada_later)
            xp, w_in_bf, w_out_bf = out[:3]
            for n_done, l in enumerate(todo):
                mods[l] = out[3][n_done]
        else:
            xp, = _conv_ffn(xp, mod_p, seq, i, *ffn_w, g_final=g_fin)
        xs, = _conv_ffn(xs, mod_s, dec_seq, i, *ffn_w, g_final=g_fin)

    state_ckv = jnp.stack([s[0] for s in states], axis=1)
    state_kpe = jnp.stack([s[1] for s in states], axis=1)
    return (xp.reshape(batch, seq, D_MODEL), xs.reshape(dec_batch, dec_seq, D_MODEL), state_ckv, state_kpe)
```

```python
import functools

import numpy as np
import jax
import jax.numpy as jnp
from jax import lax
from jax.experimental import pallas as pl
from jax.experimental.pallas import tpu as pltpu

F32 = jnp.float32
BF16 = jnp.bfloat16

D_MODEL = 1024
DEPTH = 4
GRID_W = 64
N_MIXERS = 3
N_FOURIER_GROUPS = 4
FOURIER_GROUP = D_MODEL // N_FOURIER_GROUPS
POOL_WINDOWS = (2, 4, 8, 16)
POOL_GROUP = D_MODEL // len(POOL_WINDOWS)
N_HEADS = 8
QK_NOPE = 128
QK_ROPE = 64
V_HEAD = 128
Q_RANK = 768
KV_RANK = 512
ROPE_AXIS = QK_ROPE // 2
ROPE_BASE = 10000.0
SM_SCALE = (QK_NOPE + QK_ROPE) ** -0.5
D_FF = 2816
N_MOD = 6
EPS = 1e-6

SUBLANES = 8
LANES = 128
BF16_SUBLANES = 16
VMEM_LIMIT_BYTES = 56 * 1024 * 1024

ROW_BLOCK = 256
FFN_CHUNK = 512
FFN_TILE = 512
MLA_TILE = 1024
ATT_Q_TILE = 256
ADA_COLS = 2048
ADA_SHARE_COLS = 1024
POOL_HALO = 128
POOL_SEQS = 4
FNET_SEQS = 4
FNET_STEP_BLOCKS = 2
HEAD_LANES = 2 * QK_ROPE
QK_TILE = QK_NOPE + HEAD_LANES


def _cparams(n_axes):
    return pltpu.CompilerParams(dimension_semantics=("arbitrary",) * n_axes,
                                vmem_limit_bytes=VMEM_LIMIT_BYTES)


def _rms(x):
    return x * lax.rsqrt(jnp.mean(x * x, axis=-1, keepdims=True) + EPS)


def _modulate(x, mod, sub):
    shift = mod[:, (3 * sub) * D_MODEL:(3 * sub + 1) * D_MODEL]
    scale = mod[:, (3 * sub + 1) * D_MODEL:(3 * sub + 2) * D_MODEL]
    return _rms(x) * (1.0 + scale) + shift


def _gate(mod, sub):
    return mod[:, (3 * sub + 2) * D_MODEL:(3 * sub + 3) * D_MODEL]


def _dot(a, b):
    return jnp.dot(a, b, preferred_element_type=F32)


def _dot_nt(a, b):
    return lax.dot_general(a, b, (((1,), (1,)), ((), ())), preferred_element_type=F32)


def _ada_body(cond_ref, w_ref, b_ref, o_ref):
    a = jax.nn.silu(cond_ref[...]).astype(BF16)
    o_ref[...] = _dot(a, w_ref[...].astype(BF16)) + b_ref[...]


def _ada_first_layer(cond, w_ada, b_ada):
    n_out = N_MOD * D_MODEL
    return pl.pallas_call(
        _ada_body,
        grid=(n_out // ADA_COLS,),
        in_specs=[
            pl.BlockSpec((SUBLANES, D_MODEL), lambda j: (0, 0)),
            pl.BlockSpec((None, D_MODEL, ADA_COLS), lambda j: (0, 0, j)),
            pl.BlockSpec((None, 1, ADA_COLS), lambda j: (0, 0, j)),
        ],
        out_specs=pl.BlockSpec((SUBLANES, ADA_COLS), lambda j: (0, j)),
        out_shape=jax.ShapeDtypeStruct((SUBLANES, n_out), F32),
        compiler_params=_cparams(1),
        name="ada_mod",
    )(cond, w_ada, b_ada)


def _ffn_chunks():
    edges = list(range(0, D_FF, FFN_CHUNK)) + [D_FF]
    return list(zip(edges[:-1], edges[1:]))


def _ffn_body(seq_len, n_seg, seg, final, cast_next, ada_later, *refs):
    refs = list(refs)
    x_ref, xp_ref, xn_ref, mod_ref, win_ref, wc_ref, bc_ref, wout_ref = refs[:8]
    del refs[:8]
    gfin_ref = refs.pop(0) if final else None
    side_in = [refs.pop(0) for _ in range(2 * cast_next + 3 * ada_later)]
    o_ref = refs.pop(0)
    side_out = [refs.pop(0) for _ in range(2 * cast_next + ada_later)]
    if cast_next:
        side_out[0][...] = side_in[0][...].astype(BF16)
        side_out[1][...] = side_in[1][...].astype(BF16)
    if ada_later:
        _ada_body(*side_in[-3:], side_out[-1])
    hext_ref, uga_ref, uva_ref, ugb_ref, uvb_ref, acta_ref, actb_ref, acc_ref = refs
    tile = n_seg * seg
    gap = BF16_SUBLANES
    i = pl.program_id(0)
    mod = mod_ref[...]
    x = x_ref[...]

    start = i * tile
    prev_ok = (start % seq_len != 0).astype(F32)
    next_ok = ((start + tile) % seq_len != 0).astype(F32)
    zeros8 = jnp.zeros((SUBLANES, D_MODEL), F32)
    h_prev = _modulate(xp_ref[...], mod, 1) * prev_ok
    h_next = _modulate(xn_ref[...], mod, 1) * next_ok
    hext_ref[0:gap, :] = jnp.concatenate([zeros8, h_prev], axis=0).astype(BF16)
    h = _modulate(x, mod, 1).astype(BF16)
    for k in range(n_seg):
        base = gap + k * (seg + gap)
        hext_ref[base:base + seg, :] = h[k * seg:(k + 1) * seg, :]
        if k == n_seg - 1:
            tail = jnp.concatenate([h_next, zeros8], axis=0).astype(BF16)
        else:
            tail = jnp.zeros((gap, D_MODEL), BF16)
        hext_ref[base + seg:base + seg + gap, :] = tail
    acc_ref[...] = jnp.zeros_like(acc_ref)

    def up(cols, ug_ref, uv_ref):
        lo, hi = cols
        ug_ref[:, :hi - lo] = _dot(hext_ref[...], win_ref[:, lo:hi])
        uv_ref[:, :hi - lo] = _dot(hext_ref[...], win_ref[:, D_FF + lo:D_FF + hi])

    def conv(u_ref, b, lo, hi):
        wc = wc_ref[:, lo:hi]
        return (u_ref[b - 1:b - 1 + seg, :hi - lo] * wc[0:1, :] + u_ref[b:b + seg, :hi - lo] * wc[1:2, :]
                + u_ref[b + 1:b + 1 + seg, :hi - lo] * wc[2:3, :] + bc_ref[:, lo:hi])

    def down(cols, ug_ref, uv_ref, act_ref):
        lo, hi = cols
        for k in range(n_seg):
            b = gap + k * (seg + gap)
            a = jax.nn.silu(conv(ug_ref, b, lo, hi)) * conv(uv_ref, b, D_FF + lo, D_FF + hi)
            act_ref[k * seg:(k + 1) * seg, :hi - lo] = a.astype(BF16)
        acc_ref[...] += _dot(act_ref[:, :hi - lo], wout_ref[lo:hi, :])

    chunks = _ffn_chunks()
    u_bufs = ((uga_ref, uva_ref), (ugb_ref, uvb_ref))
    act_bufs = (acta_ref, actb_ref)
    up(chunks[0], *u_bufs[0])
    for c, cols in enumerate(chunks):
        if c + 1 < len(chunks):
            up(chunks[c + 1], *u_bufs[(c + 1) % 2])
        down(cols, *u_bufs[c % 2], act_bufs[c % 2])
    y = x + _gate(mod, 1) * acc_ref[...]
    if final:
        y = _rms(y) * gfin_ref[...]
    o_ref[...] = y


def _conv_ffn(x, mod, seq_len, layer, w_in, w_conv, b_conv, w_out, g_final=None, cast_next=None,
              ada_later=None):
    n = x.shape[0]
    tile = FFN_TILE
    seg = min(seq_len, tile)
    n_seg = tile // seg
    rows = BF16_SUBLANES + n_seg * (seg + BF16_SUBLANES)
    per_seq_mod = mod.shape[0] > 1
    final = g_final is not None
    blocks8 = n // SUBLANES
    t8 = tile // SUBLANES

    def resident(a):
        return pl.BlockSpec(a.shape, lambda i: (0,) * a.ndim, pipeline_mode=pl.Buffered(1))

    def resident_layer(a):
        shape = a.shape[1:]
        return pl.BlockSpec((None,) + shape, lambda i: (layer,) + (0,) * len(shape),
                            pipeline_mode=pl.Buffered(1))

    in_specs = [
        pl.BlockSpec((tile, D_MODEL), lambda i: (i, 0)),
        pl.BlockSpec((SUBLANES, D_MODEL), lambda i: (jnp.maximum(i * t8 - 1, 0), 0)),
        pl.BlockSpec((SUBLANES, D_MODEL), lambda i: (jnp.minimum((i + 1) * t8, blocks8 - 1), 0)),
        pl.BlockSpec((None, 1, N_MOD * D_MODEL),
                     (lambda i: ((i * tile) // seq_len, 0, 0)) if per_seq_mod else (lambda i: (0, 0, 0))),
        resident(w_in), resident_layer(w_conv), resident_layer(b_conv), resident(w_out),
    ]
    args = [x, x, x, mod, w_in, w_conv, b_conv, w_out]
    if final:
        in_specs.append(pl.BlockSpec((1, D_MODEL), lambda i: (0, 0)))
        args.append(g_final)
    out_specs = [pl.BlockSpec((tile, D_MODEL), lambda i: (i, 0))]
    out_shape = [jax.ShapeDtypeStruct((n, D_MODEL), F32)]
    if cast_next is not None:
        steps = n // tile
        for a in cast_next:
            rows_step = a.shape[1] // steps
            assert rows_step * steps == a.shape[1] and rows_step % BF16_SUBLANES == 0
            in_specs.append(pl.BlockSpec((None, rows_step, a.shape[2]), lambda i: (layer + 1, i, 0)))
            out_specs.append(pl.BlockSpec((rows_step, a.shape[2]), lambda i: (i, 0)))
            out_shape.append(jax.ShapeDtypeStruct(a.shape[1:], BF16))
            args.append(a)
    if ada_later is not None:
        cond, w_ada, b_ada, layers = ada_later
        blocks_per_layer = N_MOD * D_MODEL // ADA_SHARE_COLS
        n_share = len(layers) * blocks_per_layer
        assert n_share <= n // tile and list(layers) == list(range(layers[0], layers[0] + len(layers)))

        def share(i):
            s = jnp.minimum(i, n_share - 1)
            return s // blocks_per_layer, s % blocks_per_layer

        in_specs += [
            pl.BlockSpec((SUBLANES, D_MODEL), lambda i: (0, 0)),
            pl.BlockSpec((None, D_MODEL, ADA_SHARE_COLS), lambda i: (layers[0] + share(i)[0], 0, share(i)[1])),
            pl.BlockSpec((None, 1, ADA_SHARE_COLS), lambda i: (layers[0] + share(i)[0], 0, share(i)[1])),
        ]
        args += [cond, w_ada, b_ada]
        out_specs.append(pl.BlockSpec((None, SUBLANES, ADA_SHARE_COLS), lambda i: (share(i)[0], 0, share(i)[1])))
        out_shape.append(jax.ShapeDtypeStruct((len(layers), SUBLANES, N_MOD * D_MODEL), F32))
    return pl.pallas_call(
        functools.partial(_ffn_body, seq_len, n_seg, seg, final, cast_next is not None, ada_later is not None),
        grid=(n // tile,),
        in_specs=in_specs,
        out_specs=out_specs,
        out_shape=out_shape,
        scratch_shapes=[pltpu.VMEM((rows, D_MODEL), BF16)]
        + [pltpu.VMEM((rows, FFN_CHUNK), F32)] * 4
        + [pltpu.VMEM((tile, FFN_CHUNK), BF16)] * 2
        + [pltpu.VMEM((tile, D_MODEL), F32)],
        compiler_params=_cparams(1),
        name="conv_ffn",
    )(*args)


def _fnet_body(seq_len, n_seq, x_ref, mod_ref, csc_ref, csrow_ref, w_ref, o_ref):
    mod = mod_ref[...]
    x = x_ref[...]
    h = _modulate(x, mod, 0).astype(BF16)
    table = csc_ref[...].astype(BF16)
    parts = [_dot(h[:, g * FOURIER_GROUP:(g + 1) * FOURIER_GROUP], table) for g in range(N_FOURIER_GROUPS)]
    cos_part = jnp.concatenate([p[:, :FOURIER_GROUP] for p in parts], axis=1).astype(BF16)
    sin_part = jnp.concatenate([p[:, FOURIER_GROUP:] for p in parts], axis=1).astype(BF16)
    positions = csrow_ref[...].astype(BF16)
    scale = (seq_len * FOURIER_GROUP) ** -0.5
    f = []
    for s in range(n_seq):
        rows = slice(s * seq_len, (s + 1) * seq_len)
        stack = jnp.concatenate([cos_part[rows], sin_part[rows]], axis=0)
        f.append((_dot(positions, stack) * scale).astype(BF16))
    y = _dot(jnp.concatenate(f, axis=0), w_ref[...])
    o_ref[...] = x + _gate(mod, 0) * y


def _fnet_split_body(seq_len, x_ref, mod_ref, csc_ref, even_ref, odd_ref, perm_ref, w_ref, o_ref,
                     plus_ref, minus_ref):
    r = pl.program_id(1)
    mod = mod_ref[...]
    half = seq_len // 2

    @pl.when(r == 0)
    def _():
        def rows(j, carry):
            r0 = pl.multiple_of(j * ROW_BLOCK, ROW_BLOCK)
            h_lo = _modulate(x_ref[pl.ds(r0, ROW_BLOCK), :], mod, 0).astype(BF16)
            h_hi = _modulate(x_ref[pl.ds(half + r0, ROW_BLOCK), :], mod, 0).astype(BF16)
            table = csc_ref[...].astype(BF16)
            for g in range(N_FOURIER_GROUPS):
                cols = slice(g * FOURIER_GROUP, (g + 1) * FOURIER_GROUP)
                y_lo = _dot(h_lo[:, cols], table)
                y_hi = _dot(h_hi[:, cols], table)
                for ref, y in ((plus_ref, y_lo + y_hi), (minus_ref, y_lo - y_hi)):
                    ref[pl.ds(r0, ROW_BLOCK), cols] = y[:, :FOURIER_GROUP].astype(BF16)
                    ref[pl.ds(half + r0, ROW_BLOCK), cols] = y[:, FOURIER_GROUP:].astype(BF16)
            return carry
        lax.fori_loop(0, half // ROW_BLOCK, rows, 0)

    scale = (seq_len * FOURIER_GROUP) ** -0.5
    even = (_dot(even_ref[...].astype(BF16), plus_ref[...]) * scale).astype(BF16)
    odd = (_dot(odd_ref[...].astype(BF16), minus_ref[...]) * scale).astype(BF16)
    half_block = ROW_BLOCK // 2
    f = []
    for j in range(FNET_STEP_BLOCKS):
        rows = slice(j * half_block, (j + 1) * half_block)
        f.append(_dot(perm_ref[...], jnp.concatenate([even[rows], odd[rows]], axis=0)).astype(BF16))
    y = _dot(jnp.concatenate(f, axis=0), w_ref[...])
    step_rows = FNET_STEP_BLOCKS * ROW_BLOCK
    r0 = pl.multiple_of(r * step_rows, step_rows)
    o_ref[...] = x_ref[pl.ds(r0, step_rows), :] + _gate(mod, 0) * y


def _channel_table():
    c = np.arange(FOURIER_GROUP)
    ang = 2.0 * np.pi * ((c[:, None] * c[None, :]) % FOURIER_GROUP) / FOURIER_GROUP
    return jnp.asarray(np.concatenate([np.cos(ang), np.sin(ang)], axis=1), F32)


def _position_table(freqs, positions, seq_len):
    ang = 2.0 * np.pi * ((freqs[:, None] * positions[None, :]) % seq_len) / seq_len
    return jnp.asarray(np.concatenate([np.cos(ang), -np.sin(ang)], axis=1), F32)


def _fourier_mix(x, mod, seq_len, w):
    n = x.shape[0]
    batch = n // seq_len
    n_row_blocks = seq_len // ROW_BLOCK
    mod_spec = pl.BlockSpec((None, 1, N_MOD * D_MODEL),
                            (lambda b, r: (b, 0, 0)) if mod.shape[0] > 1 else (lambda b, r: (0, 0, 0)))
    common = dict(
        grid=(batch, n_row_blocks),
        out_specs=pl.BlockSpec((ROW_BLOCK, D_MODEL), lambda b, r: (b * n_row_blocks + r, 0)),
        out_shape=jax.ShapeDtypeStruct((n, D_MODEL), F32),
        compiler_params=_cparams(2),
        name="fourier_mix",
    )
    x_spec = pl.BlockSpec((seq_len, D_MODEL), lambda b, r: (b, 0))
    csc_spec = pl.BlockSpec((FOURIER_GROUP, 2 * FOURIER_GROUP), lambda b, r: (0, 0))
    w_spec = pl.BlockSpec((D_MODEL, D_MODEL), lambda b, r: (0, 0))
    t = np.arange(seq_len)
    if n_row_blocks == 1:
        n_seq = FNET_SEQS if (mod.shape[0] == 1 and batch % FNET_SEQS == 0) else 1
        rows = pl.BlockSpec((n_seq * seq_len, D_MODEL), lambda b, r: (b, 0))
        common.update(grid=(batch // n_seq, 1), out_specs=rows)
        return pl.pallas_call(
            functools.partial(_fnet_body, seq_len, n_seq),
            in_specs=[rows, mod_spec, csc_spec, pl.BlockSpec((seq_len, 2 * seq_len), lambda b, r: (0, 0)), w_spec],
            **common,
        )(x, mod, _channel_table(), _position_table(t, t, seq_len), w)
    half_block = ROW_BLOCK // 2
    u = np.arange(seq_len // 2)
    i = np.arange(half_block)
    perm = np.zeros((ROW_BLOCK, ROW_BLOCK), np.float32)
    perm[2 * i, i] = 1.0
    perm[2 * i + 1, half_block + i] = 1.0
    step_rows = FNET_STEP_BLOCKS * ROW_BLOCK
    steps = seq_len // step_rows
    table_spec = pl.BlockSpec((step_rows // 2, seq_len), lambda b, r: (r, 0))
    common.update(grid=(batch, steps),
                  out_specs=pl.BlockSpec((step_rows, D_MODEL), lambda b, r: (b * steps + r, 0)))
    return pl.pallas_call(
        functools.partial(_fnet_split_body, seq_len),
        in_specs=[x_spec, mod_spec, csc_spec, table_spec, table_spec,
                  pl.BlockSpec((ROW_BLOCK, ROW_BLOCK), lambda b, r: (0, 0)), w_spec],
        scratch_shapes=[pltpu.VMEM((seq_len, D_MODEL), BF16)] * 2,
        **common,
    )(x, mod, _channel_table(), _position_table(2 * u, u, seq_len), _position_table(2 * u + 1, u, seq_len),
      jnp.asarray(perm, BF16), w)


def _pool_bands():
    i = np.arange(ROW_BLOCK)[:, None]
    c = np.arange(2 * ROW_BLOCK)[None, :] - POOL_HALO
    bands = [((c >= i - w // 2) & (c < i - w // 2 + w)) for w in POOL_WINDOWS]
    return jnp.asarray(np.stack(bands), BF16)


def _pool_body(seq_len, n_seq, x_ref, mod_ref, band_ref, w_ref, ps_ref, o_ref, h_ref, hb_ref):
    mod = mod_ref[...]
    n_blocks = seq_len // ROW_BLOCK
    padded = seq_len + 2 * POOL_HALO
    zeros = jnp.zeros((POOL_HALO, D_MODEL), BF16)
    for s in range(n_seq):
        hb_ref[s * padded:s * padded + POOL_HALO, :] = zeros
        hb_ref[(s + 1) * padded - POOL_HALO:(s + 1) * padded, :] = zeros
        for j in range(n_blocks):
            rows = slice(s * seq_len + j * ROW_BLOCK, s * seq_len + (j + 1) * ROW_BLOCK)
            h = _modulate(x_ref[rows, :], mod, 0)
            h_ref[rows, :] = h
            start = s * padded + POOL_HALO + j * ROW_BLOCK
            hb_ref[start:start + ROW_BLOCK, :] = h.astype(BF16)

    gate = _gate(mod, 0)
    for s in range(n_seq):
        for j in range(n_blocks):
            rows = slice(s * seq_len + j * ROW_BLOCK, s * seq_len + (j + 1) * ROW_BLOCK)
            slab = slice(s * padded + j * ROW_BLOCK, s * padded + (j + 2) * ROW_BLOCK)
            t = j * ROW_BLOCK + lax.broadcasted_iota(jnp.int32, (ROW_BLOCK, POOL_GROUP), 0)
            for g, w in enumerate(POOL_WINDOWS):
                cols = slice(g * POOL_GROUP, (g + 1) * POOL_GROUP)
                total = _dot(band_ref[g], hb_ref[slab, cols])
                cnt = jnp.minimum(t - w // 2 + w, seq_len) - jnp.maximum(t - w // 2, 0)
                pooled = total / cnt.astype(F32) - h_ref[rows, cols]
                y = _dot(pooled.astype(BF16), w_ref[g]) * ps_ref[:, cols]
                o_ref[rows, cols] = x_ref[rows, cols] + gate[:, cols] * y


def _pool_mix(x, mod, seq_len, w_pool, pool_scale):
    n = x.shape[0]
    batch = n // seq_len
    assert POOL_HALO >= max(POOL_WINDOWS) // 2 and 2 * POOL_HALO == ROW_BLOCK
    n_seq = POOL_SEQS if (seq_len == ROW_BLOCK and mod.shape[0] == 1 and batch % POOL_SEQS == 0) else 1
    rows = n_seq * seq_len
    return pl.pallas_call(
        functools.partial(_pool_body, seq_len, n_seq),
        grid=(batch // n_seq,),
        in_specs=[
            pl.BlockSpec((rows, D_MODEL), lambda b: (b, 0)),
            pl.BlockSpec((None, 1, N_MOD * D_MODEL),
                         (lambda b: (b, 0, 0)) if mod.shape[0] > 1 else (lambda b: (0, 0, 0))),
            pl.BlockSpec((len(POOL_WINDOWS), ROW_BLOCK, 2 * ROW_BLOCK), lambda b: (0, 0, 0)),
            pl.BlockSpec((len(POOL_WINDOWS), POOL_GROUP, POOL_GROUP), lambda b: (0, 0, 0)),
            pl.BlockSpec((1, D_MODEL), lambda b: (0, 0)),
        ],
        out_specs=pl.BlockSpec((rows, D_MODEL), lambda b: (b, 0)),
        out_shape=jax.ShapeDtypeStruct((n, D_MODEL), F32),
        scratch_shapes=[pltpu.VMEM((rows, D_MODEL), F32),
                        pltpu.VMEM((n_seq * (seq_len + 2 * POOL_HALO), D_MODEL), BF16)],
        compiler_params=_cparams(1),
        name="pool_mix",
    )(x, mod, _pool_bands(), w_pool, pool_scale)


def _rope_pair(a, cs):
    t = a * cs
    return t + pltpu.roll(t, QK_ROPE, 1)


def _store_keys(k_ref, k_nope, k_rope):
    for hd in range(N_HEADS):
        k_ref[:, hd * QK_TILE:hd * QK_TILE + QK_NOPE] = k_nope[:, hd * QK_NOPE:(hd + 1) * QK_NOPE]
        k_ref[:, hd * QK_TILE + QK_NOPE:(hd + 1) * QK_TILE] = k_rope


def _mla_proj_body(rope, state, *refs):
    refs = list(refs)
    x_ref, mod_ref = refs[:2]
    del refs[:2]
    cs_ref = refs.pop(0) if rope else None
    wdq_ref, gq_ref, wuq_ref, wdkv_ref, gkv_ref, wukv_ref = refs[:6]
    del refs[:6]
    q_ref, k_ref, v_ref = refs[:3]
    del refs[:3]
    h = _modulate(x_ref[...], mod_ref[...], 0).astype(BF16)
    cq = (_rms(_dot(h, wdq_ref[...])) * gq_ref[...]).astype(BF16)
    q = _dot(cq, wuq_ref[...]) * SM_SCALE
    n_nope = N_HEADS * QK_NOPE
    for hd in range(N_HEADS):
        a = q[:, n_nope + hd * HEAD_LANES:n_nope + (hd + 1) * HEAD_LANES]
        if rope:
            a = _rope_pair(a, cs_ref[...])
        q_ref[:, hd * QK_TILE:hd * QK_TILE + QK_NOPE] = q[:, hd * QK_NOPE:(hd + 1) * QK_NOPE].astype(BF16)
        q_ref[:, hd * QK_TILE + QK_NOPE:(hd + 1) * QK_TILE] = a.astype(BF16)
    ckv = _dot(h, wdkv_ref[...])
    c = _rms(ckv[:, :KV_RANK]) * gkv_ref[...]
    kp = ckv[:, KV_RANK:]
    if state:
        ckv_out_ref, kpe_out_ref = refs
        ckv_out_ref[...] = c
        kpe_out_ref[...] = kp[:, :QK_ROPE]
    if rope:
        kp = _rope_pair(kp, cs_ref[...])
    lane = lax.broadcasted_iota(jnp.int32, kp.shape, 1)
    kr = jnp.where(lane < QK_ROPE, kp, 0.0).astype(BF16)
    kv = _dot(c.astype(BF16), wukv_ref[...])
    _store_keys(k_ref, kv[:, :n_nope].astype(BF16), kr)
    v_ref[...] = kv[:, n_nope:].astype(BF16)


def _mla_project(x, mod, seq_len, prm, cs=None, state=False):
    n = x.shape[0]
    rope = cs is not None
    tile = min(MLA_TILE, seq_len) if (rope or mod.shape[0] > 1) else MLA_TILE
    tiles_per_seq = max(seq_len // tile, 1)
    wide = N_HEADS * QK_NOPE

    def const(shape):
        return pl.BlockSpec(shape, lambda i: (0,) * len(shape), pipeline_mode=pl.Buffered(1))

    def rows(width):
        return pl.BlockSpec((tile, width), lambda i: (i, 0))

    in_specs = [rows(D_MODEL),
                pl.BlockSpec((None, 1, N_MOD * D_MODEL),
                             (lambda i: (i // tiles_per_seq, 0, 0)) if mod.shape[0] > 1
                             else (lambda i: (0, 0, 0)))]
    args = [x, mod]
    if rope:
        in_specs.append(pl.BlockSpec((tile, HEAD_LANES), lambda i: (i % tiles_per_seq, 0)))
        args.append(cs)
    for name in ("w_dq", "g_q", "w_uq", "w_dkv", "g_kv", "w_ukv"):
        in_specs.append(const(prm[name].shape))
        args.append(prm[name])
    out_specs = [rows(N_HEADS * QK_TILE), rows(N_HEADS * QK_TILE), rows(wide)]
    out_shape = [jax.ShapeDtypeStruct((n, N_HEADS * QK_TILE), BF16),
                 jax.ShapeDtypeStruct((n, N_HEADS * QK_TILE), BF16),
                 jax.ShapeDtypeStruct((n, wide), BF16)]
    if state:
        out_specs += [rows(KV_RANK), rows(QK_ROPE)]
        out_shape += [jax.ShapeDtypeStruct((n, KV_RANK), F32), jax.ShapeDtypeStruct((n, QK_ROPE), F32)]
    return pl.pallas_call(
        functools.partial(_mla_proj_body, rope, state),
        grid=(n // tile,),
        in_specs=in_specs,
        out_specs=out_specs,
        out_shape=out_shape,
        compiler_params=_cparams(1),
        name="mla_project",
    )(*args)


def _ctx_expand_body(c_ref, kr_ref, w_ref, k_ref, v_ref):
    kv = _dot(c_ref[...].astype(BF16), w_ref[...])
    n_nope = N_HEADS * QK_NOPE
    _store_keys(k_ref, kv[:, :n_nope].astype(BF16), kr_ref[...])
    v_ref[...] = kv[:, n_nope:].astype(BF16)


def _ctx_expand(c_kv, k_rope, w_ukv):
    n = c_kv.shape[0]
    wide = N_HEADS * QK_NOPE
    return pl.pallas_call(
        _ctx_expand_body,
        grid=(n // ROW_BLOCK,),
        in_specs=[pl.BlockSpec((ROW_BLOCK, KV_RANK), lambda i: (i, 0)),
                  pl.BlockSpec((ROW_BLOCK, HEAD_LANES), lambda i: (i, 0)),
                  pl.BlockSpec(w_ukv.shape, lambda i: (0, 0))],
        out_specs=[pl.BlockSpec((ROW_BLOCK, N_HEADS * QK_TILE), lambda i: (i, 0)),
                   pl.BlockSpec((ROW_BLOCK, wide), lambda i: (i, 0))],
        out_shape=[jax.ShapeDtypeStruct((n, N_HEADS * QK_TILE), BF16), jax.ShapeDtypeStruct((n, wide), BF16)],
        compiler_params=_cparams(1),
        name="mla_ctx_expand",
    )(c_kv, k_rope, w_ukv)


def _attn_body(has_ctx, *refs):
    refs = list(refs)
    q_ref, k_ref, v_ref = refs[:3]
    del refs[:3]
    if has_ctx:
        kc_ref, vc_ref = refs[:2]
        del refs[:2]
    x_ref, mod_ref, wo_ref, o_ref, heads_ref = refs
    for hd in range(N_HEADS):
        qk = slice(hd * QK_TILE, (hd + 1) * QK_TILE)
        vcols = slice(hd * V_HEAD, (hd + 1) * V_HEAD)
        q = q_ref[:, qk]
        s = _dot_nt(q, k_ref[:, qk])
        m = jnp.max(s, axis=-1, keepdims=True)
        if has_ctx:
            sc = _dot_nt(q, kc_ref[:, qk])
            m = jnp.maximum(m, jnp.max(sc, axis=-1, keepdims=True))
        p = jnp.exp(s - m)
        l = jnp.sum(p, axis=-1, keepdims=True)
        o = _dot(p.astype(BF16), v_ref[:, vcols])
        if has_ctx:
            pc = jnp.exp(sc - m)
            l = l + jnp.sum(pc, axis=-1, keepdims=True)
            o = o + _dot(pc.astype(BF16), vc_ref[:, vcols])
        heads_ref[:, vcols] = (o / l).astype(BF16)
    y = _dot(heads_ref[...], wo_ref[...])
    o_ref[...] = x_ref[...] + _gate(mod_ref[...], 0) * y


def _mla_attend(x, mod, seq_len, q, k, v, w_o, ctx=None):
    n = x.shape[0]
    tq = min(ATT_Q_TILE, seq_len)
    q_tiles = seq_len // tq
    wide = N_HEADS * QK_NOPE
    has_ctx = ctx is not None

    def q_rows(width):
        return pl.BlockSpec((tq, width), lambda b, r: (b * q_tiles + r, 0))

    def seq_rows(length, width):
        return pl.BlockSpec((length, width), lambda b, r: (b, 0))

    in_specs = [q_rows(N_HEADS * QK_TILE), seq_rows(seq_len, N_HEADS * QK_TILE), seq_rows(seq_len, wide)]
    args = [q, k, v]
    if has_ctx:
        kc, vc = ctx
        past = kc.shape[0] // (n // seq_len)
        in_specs += [seq_rows(past, N_HEADS * QK_TILE), seq_rows(past, wide)]
        args += [kc, vc]
    in_specs += [q_rows(D_MODEL),
                 pl.BlockSpec((None, 1, N_MOD * D_MODEL),
                              (lambda b, r: (b, 0, 0)) if mod.shape[0] > 1 else (lambda b, r: (0, 0, 0))),
                 pl.BlockSpec(w_o.shape, lambda b, r: (0, 0))]
    args += [x, mod, w_o]
    return pl.pallas_call(
        functools.partial(_attn_body, has_ctx),
        grid=(n // seq_len, q_tiles),
        in_specs=in_specs,
        out_specs=q_rows(D_MODEL),
        out_shape=jax.ShapeDtypeStruct((n, D_MODEL), F32),
        scratch_shapes=[pltpu.VMEM((tq, wide), BF16)],
        compiler_params=_cparams(2),
        name="mla_attend",
    )(*args)


def _rope_table(n_tok):
    rows = n_tok // GRID_W
    row = np.repeat(np.arange(rows), GRID_W).astype(np.float32)
    col = np.tile(np.arange(GRID_W), rows).astype(np.float32)
    inv = (1.0 / (ROPE_BASE ** (np.arange(0, ROPE_AXIS, 2, dtype=np.float32) / ROPE_AXIS))).astype(np.float32)
    ang_r = row[:, None] * inv
    ang_c = col[:, None] * inv
    cos = np.concatenate([np.cos(ang_r), np.cos(ang_r), np.cos(ang_c), np.cos(ang_c)], axis=1)
    sin = np.concatenate([np.sin(ang_r), np.sin(ang_r), np.sin(ang_c), np.sin(ang_c)], axis=1)
    return jnp.asarray(np.concatenate([cos, sin], axis=1), F32)


def _swap_cols(w):
    half = ROPE_AXIS // 2
    return jnp.concatenate([-w[..., half:2 * half], w[..., 0:half],
                            -w[..., 3 * half:4 * half], w[..., 2 * half:3 * half]], axis=-1)


def _mla_weights(w_dq, g_q, w_uq, w_dkv, g_kv, w_ukv, w_o):
    wq = w_uq.reshape(Q_RANK, N_HEADS, QK_NOPE + QK_ROPE)
    wq_pe = wq[:, :, QK_NOPE:]
    wq_pairs = jnp.concatenate([wq_pe, _swap_cols(wq_pe)], axis=-1)
    w_uq2 = jnp.concatenate([wq[:, :, :QK_NOPE].reshape(Q_RANK, -1), wq_pairs.reshape(Q_RANK, -1)], axis=1)
    w_pe = w_dkv[:, KV_RANK:]
    w_dkv2 = jnp.concatenate([w_dkv[:, :KV_RANK], w_pe, _swap_cols(w_pe)], axis=1)
    wkv = w_ukv.reshape(KV_RANK, N_HEADS, QK_NOPE + V_HEAD)
    w_ukv2 = jnp.concatenate([wkv[:, :, :QK_NOPE].reshape(KV_RANK, -1),
                              wkv[:, :, QK_NOPE:].reshape(KV_RANK, -1)], axis=1)
    return {"w_dq": w_dq.astype(BF16), "g_q": g_q.reshape(1, Q_RANK), "w_uq": w_uq2.astype(BF16),
            "w_dkv": w_dkv2.astype(BF16), "g_kv": g_kv.reshape(1, KV_RANK), "w_ukv": w_ukv2.astype(BF16),
            "w_o": w_o.astype(BF16)}


def _cast_body(a_ref, b_ref, ao_ref, bo_ref):
    ao_ref[...] = a_ref[...].astype(BF16)
    bo_ref[...] = b_ref[...].astype(BF16)


def _ffn_first_weights(w_in, w_out):
    steps = D_FF // ROW_BLOCK
    cols_in = 2 * D_FF // steps
    return pl.pallas_call(
        _cast_body,
        grid=(steps,),
        in_specs=[pl.BlockSpec((None, D_MODEL, cols_in), lambda c: (0, 0, c)),
                  pl.BlockSpec((None, ROW_BLOCK, D_MODEL), lambda c: (0, c, 0))],
        out_specs=[pl.BlockSpec((D_MODEL, cols_in), lambda c: (0, c)),
                   pl.BlockSpec((ROW_BLOCK, D_MODEL), lambda c: (c, 0))],
        out_shape=[jax.ShapeDtypeStruct(w_in.shape[1:], BF16), jax.ShapeDtypeStruct(w_out.shape[1:], BF16)],
        compiler_params=_cparams(1),
        name="ffn_weight_cast",
    )(w_in, w_out)


def kernel(x_prompt, x_sample, cache_ckv, cache_kpe, c, c_ctx, w_ada, b_ada, w_fnet, w_pool, pool_scale,
           w_dq, g_q, w_uq, w_dkv, g_kv, w_ukv, w_o, w_ffn_in, w_ffn_conv, b_ffn_conv, w_ffn_out, g_final):
    batch, seq, _ = x_prompt.shape
    dec_batch, dec_seq, _ = x_sample.shape
    past = cache_ckv.shape[2]

    cond = jnp.concatenate([c_ctx[None, :], c, jnp.zeros((SUBLANES - 1 - dec_batch, D_MODEL), F32)], axis=0)
    b_ada = b_ada.reshape(DEPTH, 1, N_MOD * D_MODEL)
    mods = {0: _ada_first_layer(cond, w_ada, b_ada)}
    ada_layers_per_call = (batch * seq // FFN_TILE) // (N_MOD * D_MODEL // ADA_SHARE_COLS)
    w_in_bf, w_out_bf = _ffn_first_weights(w_ffn_in, w_ffn_out)
    b_conv = b_ffn_conv[:, None, :]

    xp = x_prompt.reshape(batch * seq, D_MODEL)
    xs = x_sample.reshape(dec_batch * dec_seq, D_MODEL)
    states = []
    j_f = j_p = j_a = 0
    for i in range(DEPTH):
        mod_p = mods[i][0:1].reshape(1, 1, N_MOD * D_MODEL)
        mod_s = mods[i][1:1 + dec_batch].reshape(dec_batch, 1, N_MOD * D_MODEL)
        kind = i % N_MIXERS
        if kind == 0:
            w = w_fnet[j_f].astype(BF16)
            xp = _fourier_mix(xp, mod_p, seq, w)
            xs = _fourier_mix(xs, mod_s, dec_seq, w)
            j_f += 1
        elif kind == 1:
            w = w_pool[j_p].astype(BF16)
            ps = pool_scale[j_p].reshape(1, D_MODEL)
            xp = _pool_mix(xp, mod_p, seq, w, ps)
            xs = _pool_mix(xs, mod_s, dec_seq, w, ps)
            j_p += 1
        else:
            prm = _mla_weights(w_dq[j_a], g_q[j_a], w_uq[j_a], w_dkv[j_a], g_kv[j_a], w_ukv[j_a], w_o[j_a])
            q, k, v, ckv_p, kpe_p = _mla_project(xp, mod_p, seq, prm, state=True)
            xp = _mla_attend(xp, mod_p, seq, q, k, v, prm["w_o"])
            states.append((ckv_p.reshape(batch, seq, KV_RANK), kpe_p.reshape(batch, seq, QK_ROPE)))
            q, k, v = _mla_project(xs, mod_s, dec_seq, prm, cs=_rope_table(dec_seq))
            kpe_c = cache_kpe[:, j_a].reshape(dec_batch * past, QK_ROPE)
            krc = jnp.concatenate([kpe_c, jnp.zeros_like(kpe_c)], axis=1).astype(BF16)
            kc, vc = _ctx_expand(cache_ckv[:, j_a].reshape(dec_batch * past, KV_RANK), krc, prm["w_ukv"])
            xs = _mla_attend(xs, mod_s, dec_seq, q, k, v, prm["w_o"], ctx=(kc, vc))
            j_a += 1
        g_fin = g_final.reshape(1, D_MODEL) if i == DEPTH - 1 else None
        ffn_w = (w_in_bf, w_ffn_conv, b_conv, w_out_bf)
        if i + 1 < DEPTH:
            todo = [l for l in range(i + 1, DEPTH) if l not in mods][:ada_layers_per_call]
            assert i + 1 in mods or (todo and todo[0] == i + 1)
            ada_later = (cond, w_ada, b_ada, tuple(todo)) if todo else None
            out = _conv_ffn(xp, mod_p, seq, i, *ffn_w, cast_next=(w_ffn_in, w_ffn_out), ada_later=ada_later)
            xp, w_in_bf, w_out_bf = out[:3]
            for n_done, l in enumerate(todo):
                mods[l] = out[3][n_done]
        else:
            xp, = _conv_ffn(xp, mod_p, seq, i, *ffn_w, g_final=g_fin)
        xs, = _conv_ffn(xs, mod_s, dec_seq, i, *ffn_w, g_final=g_fin)

    state_ckv = jnp.stack([s[0] for s in states], axis=1)
    state_kpe = jnp.stack([s[1] for s in states], axis=1)
    return (xp.reshape(batch, seq, D_MODEL), xs.reshape(dec_batch, dec_seq, D_MODEL), state_ckv, state_kpe)
```

```python
import functools

import numpy as np
import jax
import jax.numpy as jnp
from jax import lax
from jax.experimental import pallas as pl
from jax.experimental.pallas import tpu as pltpu

F32 = jnp.float32
BF16 = jnp.bfloat16

D_MODEL = 1024
DEPTH = 4
GRID_W = 64
N_MIXERS = 3
N_FOURIER_GROUPS = 4
FOURIER_GROUP = D_MODEL // N_FOURIER_GROUPS
POOL_WINDOWS = (2, 4, 8, 16)
POOL_GROUP = D_MODEL // len(POOL_WINDOWS)
N_HEADS = 8
QK_NOPE = 128
QK_ROPE = 64
V_HEAD = 128
Q_RANK = 768
KV_RANK = 512
ROPE_AXIS = QK_ROPE // 2
ROPE_BASE = 10000.0
SM_SCALE = (QK_NOPE + QK_ROPE) ** -0.5
D_FF = 2816
N_MOD = 6
EPS = 1e-6

SUBLANES = 8
LANES = 128
BF16_SUBLANES = 16
VMEM_LIMIT_BYTES = 56 * 1024 * 1024

ROW_BLOCK = 256
FFN_CHUNK = 512
FFN_TILE = 512
MLA_TILE = 1024
ATT_Q_TILE = 256
ADA_COLS = 2048
ADA_SHARE_COLS = 1024
POOL_HALO = 128
POOL_SEQS = 4
FNET_SEQS = 4
ATT_SEQS = 4
FNET_STEP_BLOCKS = 2
HEAD_LANES = 2 * QK_ROPE
QK_TILE = QK_NOPE + HEAD_LANES


def _cparams(n_axes):
    return pltpu.CompilerParams(dimension_semantics=("arbitrary",) * n_axes,
                                vmem_limit_bytes=VMEM_LIMIT_BYTES)


def _rms(x):
    return x * lax.rsqrt(jnp.mean(x * x, axis=-1, keepdims=True) + EPS)


def _modulate(x, mod, sub):
    shift = mod[:, (3 * sub) * D_MODEL:(3 * sub + 1) * D_MODEL]
    scale = mod[:, (3 * sub + 1) * D_MODEL:(3 * sub + 2) * D_MODEL]
    return _rms(x) * (1.0 + scale) + shift


def _gate(mod, sub):
    return mod[:, (3 * sub + 2) * D_MODEL:(3 * sub + 3) * D_MODEL]


def _dot(a, b):
    return jnp.dot(a, b, preferred_element_type=F32)


def _dot_nt(a, b):
    return lax.dot_general(a, b, (((1,), (1,)), ((), ())), preferred_element_type=F32)


def _ada_body(cond_ref, w_ref, b_ref, o_ref):
    a = jax.nn.silu(cond_ref[...]).astype(BF16)
    o_ref[...] = _dot(a, w_ref[...].astype(BF16)) + b_ref[...]


def _ada_first_layer(cond, w_ada, b_ada):
    n_out = N_MOD * D_MODEL
    return pl.pallas_call(
        _ada_body,
        grid=(n_out // ADA_COLS,),
        in_specs=[
            pl.BlockSpec((SUBLANES, D_MODEL), lambda j: (0, 0)),
            pl.BlockSpec((None, D_MODEL, ADA_COLS), lambda j: (0, 0, j)),
            pl.BlockSpec((None, 1, ADA_COLS), lambda j: (0, 0, j)),
        ],
        out_specs=pl.BlockSpec((SUBLANES, ADA_COLS), lambda j: (0, j)),
        out_shape=jax.ShapeDtypeStruct((SUBLANES, n_out), F32),
        compiler_params=_cparams(1),
        name="ada_mod",
    )(cond, w_ada, b_ada)


def _ffn_chunks():
    edges = list(range(0, D_FF, FFN_CHUNK)) + [D_FF]
    return list(zip(edges[:-1], edges[1:]))


def _ffn_body(seq_len, n_seg, seg, final, cast_next, ada_later, *refs):
    refs = list(refs)
    x_ref, xp_ref, xn_ref, mod_ref, win_ref, wc_ref, bc_ref, wout_ref = refs[:8]
    del refs[:8]
    gfin_ref = refs.pop(0) if final else None
    side_in = [refs.pop(0) for _ in range(2 * cast_next + 3 * ada_later)]
    o_ref = refs.pop(0)
    side_out = [refs.pop(0) for _ in range(2 * cast_next + ada_later)]
    if cast_next:
        side_out[0][...] = side_in[0][...].astype(BF16)
        side_out[1][...] = side_in[1][...].astype(BF16)
    if ada_later:
        _ada_body(*side_in[-3:], side_out[-1])
    hext_ref, uga_ref, uva_ref, ugb_ref, uvb_ref, acta_ref, actb_ref, acc_ref = refs
    tile = n_seg * seg
    gap = BF16_SUBLANES
    i = pl.program_id(0)
    mod = mod_ref[...]
    x = x_ref[...]

    start = i * tile
    prev_ok = (start % seq_len != 0).astype(F32)
    next_ok = ((start + tile) % seq_len != 0).astype(F32)
    zeros8 = jnp.zeros((SUBLANES, D_MODEL), F32)
    h_prev = _modulate(xp_ref[...], mod, 1) * prev_ok
    h_next = _modulate(xn_ref[...], mod, 1) * next_ok
    hext_ref[0:gap, :] = jnp.concatenate([zeros8, h_prev], axis=0).astype(BF16)
    h = _modulate(x, mod, 1).astype(BF16)
    for k in range(n_seg):
        base = gap + k * (seg + gap)
        hext_ref[base:base + seg, :] = h[k * seg:(k + 1) * seg, :]
        if k == n_seg - 1:
            tail = jnp.concatenate([h_next, zeros8], axis=0).astype(BF16)
        else:
            tail = jnp.zeros((gap, D_MODEL), BF16)
        hext_ref[base + seg:base + seg + gap, :] = tail
    acc_ref[...] = jnp.zeros_like(acc_ref)

    def up(cols, ug_ref, uv_ref):
        lo, hi = cols
        ug_ref[:, :hi - lo] = _dot(hext_ref[...], win_ref[:, lo:hi])
        uv_ref[:, :hi - lo] = _dot(hext_ref[...], win_ref[:, D_FF + lo:D_FF + hi])

    def conv(u_ref, b, lo, hi):
        wc = wc_ref[:, lo:hi]
        return (u_ref[b - 1:b - 1 + seg, :hi - lo] * wc[0:1, :] + u_ref[b:b + seg, :hi - lo] * wc[1:2, :]
                + u_ref[b + 1:b + 1 + seg, :hi - lo] * wc[2:3, :] + bc_ref[:, lo:hi])

    def down(cols, ug_ref, uv_ref, act_ref):
        lo, hi = cols
        for k in range(n_seg):
            b = gap + k * (seg + gap)
            a = jax.nn.silu(conv(ug_ref, b, lo, hi)) * conv(uv_ref, b, D_FF + lo, D_FF + hi)
            act_ref[k * seg:(k + 1) * seg, :hi - lo] = a.astype(BF16)
        acc_ref[...] += _dot(act_ref[:, :hi - lo], wout_ref[lo:hi, :])

    chunks = _ffn_chunks()
    u_bufs = ((uga_ref, uva_ref), (ugb_ref, uvb_ref))
    act_bufs = (acta_ref, actb_ref)
    up(chunks[0], *u_bufs[0])
    for c, cols in enumerate(chunks):
        if c + 1 < len(chunks):
            up(chunks[c + 1], *u_bufs[(c + 1) % 2])
        down(cols, *u_bufs[c % 2], act_bufs[c % 2])
    y = x + _gate(mod, 1) * acc_ref[...]
    if final:
        y = _rms(y) * gfin_ref[...]
    o_ref[...] = y


def _conv_ffn(x, mod, seq_len, layer, w_in, w_conv, b_conv, w_out, g_final=None, cast_next=None,
              ada_later=None):
    n = x.shape[0]
    tile = FFN_TILE
    seg = min(seq_len, tile)
    n_seg = tile // seg
    rows = BF16_SUBLANES + n_seg * (seg + BF16_SUBLANES)
    per_seq_mod = mod.shape[0] > 1
    final = g_final is not None
    blocks8 = n // SUBLANES
    t8 = tile // SUBLANES

    def resident(a):
        return pl.BlockSpec(a.shape, lambda i: (0,) * a.ndim, pipeline_mode=pl.Buffered(1))

    def resident_layer(a):
        shape = a.shape[1:]
        return pl.BlockSpec((None,) + shape, lambda i: (layer,) + (0,) * len(shape),
                            pipeline_mode=pl.Buffered(1))

    in_specs = [
        pl.BlockSpec((tile, D_MODEL), lambda i: (i, 0)),
        pl.BlockSpec((SUBLANES, D_MODEL), lambda i: (jnp.maximum(i * t8 - 1, 0), 0)),
        pl.BlockSpec((SUBLANES, D_MODEL), lambda i: (jnp.minimum((i + 1) * t8, blocks8 - 1), 0)),
        pl.BlockSpec((None, 1, N_MOD * D_MODEL),
                     (lambda i: ((i * tile) // seq_len, 0, 0)) if per_seq_mod else (lambda i: (0, 0, 0))),
        resident(w_in), resident_layer(w_conv), resident_layer(b_conv), resident(w_out),
    ]
    args = [x, x, x, mod, w_in, w_conv, b_conv, w_out]
    if final:
        in_specs.append(pl.BlockSpec((1, D_MODEL), lambda i: (0, 0)))
        args.append(g_final)
    out_specs = [pl.BlockSpec((tile, D_MODEL), lambda i: (i, 0))]
    out_shape = [jax.ShapeDtypeStruct((n, D_MODEL), F32)]
    if cast_next is not None:
        steps = n // tile
        for a in cast_next:
            rows_step = a.shape[1] // steps
            assert rows_step * steps == a.shape[1] and rows_step % BF16_SUBLANES == 0
            in_specs.append(pl.BlockSpec((None, rows_step, a.shape[2]), lambda i: (layer + 1, i, 0)))
            out_specs.append(pl.BlockSpec((rows_step, a.shape[2]), lambda i: (i, 0)))
            out_shape.append(jax.ShapeDtypeStruct(a.shape[1:], BF16))
            args.append(a)
    if ada_later is not None:
        cond, w_ada, b_ada, layers = ada_later
        blocks_per_layer = N_MOD * D_MODEL // ADA_SHARE_COLS
        n_share = len(layers) * blocks_per_layer
        assert n_share <= n // tile and list(layers) == list(range(layers[0], layers[0] + len(layers)))

        def share(i):
            s = jnp.minimum(i, n_share - 1)
            return s // blocks_per_layer, s % blocks_per_layer

        in_specs += [
            pl.BlockSpec((SUBLANES, D_MODEL), lambda i: (0, 0)),
            pl.BlockSpec((None, D_MODEL, ADA_SHARE_COLS), lambda i: (layers[0] + share(i)[0], 0, share(i)[1])),
            pl.BlockSpec((None, 1, ADA_SHARE_COLS), lambda i: (layers[0] + share(i)[0], 0, share(i)[1])),
        ]
        args += [cond, w_ada, b_ada]
        out_specs.append(pl.BlockSpec((None, SUBLANES, ADA_SHARE_COLS), lambda i: (share(i)[0], 0, share(i)[1])))
        out_shape.append(jax.ShapeDtypeStruct((len(layers), SUBLANES, N_MOD * D_MODEL), F32))
    return pl.pallas_call(
        functools.partial(_ffn_body, seq_len, n_seg, seg, final, cast_next is not None, ada_later is not None),
        grid=(n // tile,),
        in_specs=in_specs,
        out_specs=out_specs,
        out_shape=out_shape,
        scratch_shapes=[pltpu.VMEM((rows, D_MODEL), BF16)]
        + [pltpu.VMEM((rows, FFN_CHUNK), F32)] * 4
        + [pltpu.VMEM((tile, FFN_CHUNK), BF16)] * 2
        + [pltpu.VMEM((tile, D_MODEL), F32)],
        compiler_params=_cparams(1),
        name="conv_ffn",
    )(*args)


def _fnet_body(seq_len, n_seq, x_ref, mod_ref, csc_ref, csrow_ref, w_ref, o_ref):
    mod = mod_ref[...]
    x = x_ref[...]
    h = _modulate(x, mod, 0).astype(BF16)
    table = csc_ref[...].astype(BF16)
    parts = [_dot(h[:, g * FOURIER_GROUP:(g + 1) * FOURIER_GROUP], table) for g in range(N_FOURIER_GROUPS)]
    cos_part = jnp.concatenate([p[:, :FOURIER_GROUP] for p in parts], axis=1).astype(BF16)
    sin_part = jnp.concatenate([p[:, FOURIER_GROUP:] for p in parts], axis=1).astype(BF16)
    positions = csrow_ref[...].astype(BF16)
    scale = (seq_len * FOURIER_GROUP) ** -0.5
    f = []
    for s in range(n_seq):
        rows = slice(s * seq_len, (s + 1) * seq_len)
        stack = jnp.concatenate([cos_part[rows], sin_part[rows]], axis=0)
        f.append((_dot(positions, stack) * scale).astype(BF16))
    y = _dot(jnp.concatenate(f, axis=0), w_ref[...])
    o_ref[...] = x + _gate(mod, 0) * y


def _fnet_split_body(seq_len, x_ref, mod_ref, csc_ref, even_ref, odd_ref, perm_ref, w_ref, o_ref,
                     plus_ref, minus_ref):
    r = pl.program_id(1)
    mod = mod_ref[...]
    half = seq_len // 2

    @pl.when(r == 0)
    def _():
        def rows(j, carry):
            r0 = pl.multiple_of(j * ROW_BLOCK, ROW_BLOCK)
            h_lo = _modulate(x_ref[pl.ds(r0, ROW_BLOCK), :], mod, 0).astype(BF16)
            h_hi = _modulate(x_ref[pl.ds(half + r0, ROW_BLOCK), :], mod, 0).astype(BF16)
            table = csc_ref[...].astype(BF16)
            for g in range(N_FOURIER_GROUPS):
                cols = slice(g * FOURIER_GROUP, (g + 1) * FOURIER_GROUP)
                y_lo = _dot(h_lo[:, cols], table)
                y_hi = _dot(h_hi[:, cols], table)
                for ref, y in ((plus_ref, y_lo + y_hi), (minus_ref, y_lo - y_hi)):
                    ref[pl.ds(r0, ROW_BLOCK), cols] = y[:, :FOURIER_GROUP].astype(BF16)
                    ref[pl.ds(half + r0, ROW_BLOCK), cols] = y[:, FOURIER_GROUP:].astype(BF16)
            return carry
        lax.fori_loop(0, half // ROW_BLOCK, rows, 0)

    scale = (seq_len * FOURIER_GROUP) ** -0.5
    even = (_dot(even_ref[...].astype(BF16), plus_ref[...]) * scale).astype(BF16)
    odd = (_dot(odd_ref[...].astype(BF16), minus_ref[...]) * scale).astype(BF16)
    half_block = ROW_BLOCK // 2
    f = []
    for j in range(FNET_STEP_BLOCKS):
        rows = slice(j * half_block, (j + 1) * half_block)
        f.append(_dot(perm_ref[...], jnp.concatenate([even[rows], odd[rows]], axis=0)).astype(BF16))
    y = _dot(jnp.concatenate(f, axis=0), w_ref[...])
    step_rows = FNET_STEP_BLOCKS * ROW_BLOCK
    r0 = pl.multiple_of(r * step_rows, step_rows)
    o_ref[...] = x_ref[pl.ds(r0, step_rows), :] + _gate(mod, 0) * y


def _channel_table():
    c = np.arange(FOURIER_GROUP)
    ang = 2.0 * np.pi * ((c[:, None] * c[None, :]) % FOURIER_GROUP) / FOURIER_GROUP
    return jnp.asarray(np.concatenate([np.cos(ang), np.sin(ang)], axis=1), F32)


def _position_table(freqs, positions, seq_len):
    ang = 2.0 * np.pi * ((freqs[:, None] * positions[None, :]) % seq_len) / seq_len
    return jnp.asarray(np.concatenate([np.cos(ang), -np.sin(ang)], axis=1), F32)


def _fourier_mix(x, mod, seq_len, w):
    n = x.shape[0]
    batch = n // seq_len
    n_row_blocks = seq_len // ROW_BLOCK
    mod_spec = pl.BlockSpec((None, 1, N_MOD * D_MODEL),
                            (lambda b, r: (b, 0, 0)) if mod.shape[0] > 1 else (lambda b, r: (0, 0, 0)))
    common = dict(
        grid=(batch, n_row_blocks),
        out_specs=pl.BlockSpec((ROW_BLOCK, D_MODEL), lambda b, r: (b * n_row_blocks + r, 0)),
        out_shape=jax.ShapeDtypeStruct((n, D_MODEL), F32),
        compiler_params=_cparams(2),
        name="fourier_mix",
    )
    x_spec = pl.BlockSpec((seq_len, D_MODEL), lambda b, r: (b, 0))
    csc_spec = pl.BlockSpec((FOURIER_GROUP, 2 * FOURIER_GROUP), lambda b, r: (0, 0))
    w_spec = pl.BlockSpec((D_MODEL, D_MODEL), lambda b, r: (0, 0))
    t = np.arange(seq_len)
    if n_row_blocks == 1:
        n_seq = FNET_SEQS if (mod.shape[0] == 1 and batch % FNET_SEQS == 0) else 1
        rows = pl.BlockSpec((n_seq * seq_len, D_MODEL), lambda b, r: (b, 0))
        common.update(grid=(batch // n_seq, 1), out_specs=rows)
        return pl.pallas_call(
            functools.partial(_fnet_body, seq_len, n_seq),
            in_specs=[rows, mod_spec, csc_spec, pl.BlockSpec((seq_len, 2 * seq_len), lambda b, r: (0, 0)), w_spec],
            **common,
        )(x, mod, _channel_table(), _position_table(t, t, seq_len), w)
    half_block = ROW_BLOCK // 2
    u = np.arange(seq_len // 2)
    i = np.arange(half_block)
    perm = np.zeros((ROW_BLOCK, ROW_BLOCK), np.float32)
    perm[2 * i, i] = 1.0
    perm[2 * i + 1, half_block + i] = 1.0
    step_rows = FNET_STEP_BLOCKS * ROW_BLOCK
    steps = seq_len // step_rows
    table_spec = pl.BlockSpec((step_rows // 2, seq_len), lambda b, r: (r, 0))
    common.update(grid=(batch, steps),
                  out_specs=pl.BlockSpec((step_rows, D_MODEL), lambda b, r: (b * steps + r, 0)))
    return pl.pallas_call(
        functools.partial(_fnet_split_body, seq_len),
        in_specs=[x_spec, mod_spec, csc_spec, table_spec, table_spec,
                  pl.BlockSpec((ROW_BLOCK, ROW_BLOCK), lambda b, r: (0, 0)), w_spec],
        scratch_shapes=[pltpu.VMEM((seq_len, D_MODEL), BF16)] * 2,
        **common,
    )(x, mod, _channel_table(), _position_table(2 * u, u, seq_len), _position_table(2 * u + 1, u, seq_len),
      jnp.asarray(perm, BF16), w)


def _pool_bands():
    i = np.arange(ROW_BLOCK)[:, None]
    c = np.arange(2 * ROW_BLOCK)[None, :] - POOL_HALO
    bands = [((c >= i - w // 2) & (c < i - w // 2 + w)) for w in POOL_WINDOWS]
    return jnp.asarray(np.stack(bands), BF16)


def _pool_body(seq_len, n_seq, x_ref, mod_ref, band_ref, w_ref, ps_ref, o_ref, h_ref, hb_ref):
    mod = mod_ref[...]
    n_blocks = seq_len // ROW_BLOCK
    padded = seq_len + 2 * POOL_HALO
    zeros = jnp.zeros((POOL_HALO, D_MODEL), BF16)
    for s in range(n_seq):
        hb_ref[s * padded:s * padded + POOL_HALO, :] = zeros
        hb_ref[(s + 1) * padded - POOL_HALO:(s + 1) * padded, :] = zeros
        for j in range(n_blocks):
            rows = slice(s * seq_len + j * ROW_BLOCK, s * seq_len + (j + 1) * ROW_BLOCK)
            h = _modulate(x_ref[rows, :], mod, 0)
            h_ref[rows, :] = h
            start = s * padded + POOL_HALO + j * ROW_BLOCK
            hb_ref[start:start + ROW_BLOCK, :] = h.astype(BF16)

    gate = _gate(mod, 0)
    for s in range(n_seq):
        for j in range(n_blocks):
            rows = slice(s * seq_len + j * ROW_BLOCK, s * seq_len + (j + 1) * ROW_BLOCK)
            slab = slice(s * padded + j * ROW_BLOCK, s * padded + (j + 2) * ROW_BLOCK)
            t = j * ROW_BLOCK + lax.broadcasted_iota(jnp.int32, (ROW_BLOCK, POOL_GROUP), 0)
            for g, w in enumerate(POOL_WINDOWS):
                cols = slice(g * POOL_GROUP, (g + 1) * POOL_GROUP)
                total = _dot(band_ref[g], hb_ref[slab, cols])
                cnt = jnp.minimum(t - w // 2 + w, seq_len) - jnp.maximum(t - w // 2, 0)
                pooled = total / cnt.astype(F32) - h_ref[rows, cols]
                y = _dot(pooled.astype(BF16), w_ref[g]) * ps_ref[:, cols]
                o_ref[rows, cols] = x_ref[rows, cols] + gate[:, cols] * y


def _pool_mix(x, mod, seq_len, w_pool, pool_scale):
    n = x.shape[0]
    batch = n // seq_len
    assert POOL_HALO >= max(POOL_WINDOWS) // 2 and 2 * POOL_HALO == ROW_BLOCK
    n_seq = POOL_SEQS if (seq_len == ROW_BLOCK and mod.shape[0] == 1 and batch % POOL_SEQS == 0) else 1
    rows = n_seq * seq_len
    return pl.pallas_call(
        functools.partial(_pool_body, seq_len, n_seq),
        grid=(batch // n_seq,),
        in_specs=[
            pl.BlockSpec((rows, D_MODEL), lambda b: (b, 0)),
            pl.BlockSpec((None, 1, N_MOD * D_MODEL),
                         (lambda b: (b, 0, 0)) if mod.shape[0] > 1 else (lambda b: (0, 0, 0))),
            pl.BlockSpec((len(POOL_WINDOWS), ROW_BLOCK, 2 * ROW_BLOCK), lambda b: (0, 0, 0)),
            pl.BlockSpec((len(POOL_WINDOWS), POOL_GROUP, POOL_GROUP), lambda b: (0, 0, 0)),
            pl.BlockSpec((1, D_MODEL), lambda b: (0, 0)),
        ],
        out_specs=pl.BlockSpec((rows, D_MODEL), lambda b: (b, 0)),
        out_shape=jax.ShapeDtypeStruct((n, D_MODEL), F32),
        scratch_shapes=[pltpu.VMEM((rows, D_MODEL), F32),
                        pltpu.VMEM((n_seq * (seq_len + 2 * POOL_HALO), D_MODEL), BF16)],
        compiler_params=_cparams(1),
        name="pool_mix",
    )(x, mod, _pool_bands(), w_pool, pool_scale)


def _rope_pair(a, cs):
    t = a * cs
    return t + pltpu.roll(t, QK_ROPE, 1)


def _store_keys(k_ref, k_nope, k_rope):
    for hd in range(N_HEADS):
        k_ref[:, hd * QK_TILE:hd * QK_TILE + QK_NOPE] = k_nope[:, hd * QK_NOPE:(hd + 1) * QK_NOPE]
        k_ref[:, hd * QK_TILE + QK_NOPE:(hd + 1) * QK_TILE] = k_rope


def _mla_proj_body(rope, state, *refs):
    refs = list(refs)
    x_ref, mod_ref = refs[:2]
    del refs[:2]
    cs_ref = refs.pop(0) if rope else None
    wdq_ref, gq_ref, wuq_ref, wdkv_ref, gkv_ref, wukv_ref = refs[:6]
    del refs[:6]
    q_ref, k_ref, v_ref = refs[:3]
    del refs[:3]
    h = _modulate(x_ref[...], mod_ref[...], 0).astype(BF16)
    cq = (_rms(_dot(h, wdq_ref[...])) * gq_ref[...]).astype(BF16)
    q = _dot(cq, wuq_ref[...]) * SM_SCALE
    n_nope = N_HEADS * QK_NOPE
    for hd in range(N_HEADS):
        a = q[:, n_nope + hd * HEAD_LANES:n_nope + (hd + 1) * HEAD_LANES]
        if rope:
            a = _rope_pair(a, cs_ref[...])
        q_ref[:, hd * QK_TILE:hd * QK_TILE + QK_NOPE] = q[:, hd * QK_NOPE:(hd + 1) * QK_NOPE].astype(BF16)
        q_ref[:, hd * QK_TILE + QK_NOPE:(hd + 1) * QK_TILE] = a.astype(BF16)
    ckv = _dot(h, wdkv_ref[...])
    c = _rms(ckv[:, :KV_RANK]) * gkv_ref[...]
    kp = ckv[:, KV_RANK:]
    if state:
        ckv_out_ref, kpe_out_ref = refs
        ckv_out_ref[...] = c
        kpe_out_ref[...] = kp[:, :QK_ROPE]
    if rope:
        kp = _rope_pair(kp, cs_ref[...])
    lane = lax.broadcasted_iota(jnp.int32, kp.shape, 1)
    kr = jnp.where(lane < QK_ROPE, kp, 0.0).astype(BF16)
    kv = _dot(c.astype(BF16), wukv_ref[...])
    _store_keys(k_ref, kv[:, :n_nope].astype(BF16), kr)
    v_ref[...] = kv[:, n_nope:].astype(BF16)


def _mla_project(x, mod, seq_len, prm, cs=None, state=False):
    n = x.shape[0]
    rope = cs is not None
    tile = min(MLA_TILE, seq_len) if (rope or mod.shape[0] > 1) else MLA_TILE
    tiles_per_seq = max(seq_len // tile, 1)
    wide = N_HEADS * QK_NOPE

    def const(shape):
        return pl.BlockSpec(shape, lambda i: (0,) * len(shape), pipeline_mode=pl.Buffered(1))

    def rows(width):
        return pl.BlockSpec((tile, width), lambda i: (i, 0))

    in_specs = [rows(D_MODEL),
                pl.BlockSpec((None, 1, N_MOD * D_MODEL),
                             (lambda i: (i // tiles_per_seq, 0, 0)) if mod.shape[0] > 1
                             else (lambda i: (0, 0, 0)))]
    args = [x, mod]
    if rope:
        in_specs.append(pl.BlockSpec((tile, HEAD_LANES), lambda i: (i % tiles_per_seq, 0)))
        args.append(cs)
    for name in ("w_dq", "g_q", "w_uq", "w_dkv", "g_kv", "w_ukv"):
        in_specs.append(const(prm[name].shape))
        args.append(prm[name])
    out_specs = [rows(N_HEADS * QK_TILE), rows(N_HEADS * QK_TILE), rows(wide)]
    out_shape = [jax.ShapeDtypeStruct((n, N_HEADS * QK_TILE), BF16),
                 jax.ShapeDtypeStruct((n, N_HEADS * QK_TILE), BF16),
                 jax.ShapeDtypeStruct((n, wide), BF16)]
    if state:
        out_specs += [rows(KV_RANK), rows(QK_ROPE)]
        out_shape += [jax.ShapeDtypeStruct((n, KV_RANK), F32), jax.ShapeDtypeStruct((n, QK_ROPE), F32)]
    return pl.pallas_call(
        functools.partial(_mla_proj_body, rope, state),
        grid=(n // tile,),
        in_specs=in_specs,
        out_specs=out_specs,
        out_shape=out_shape,
        compiler_params=_cparams(1),
        name="mla_project",
    )(*args)


def _ctx_expand_body(c_ref, kr_ref, w_ref, k_ref, v_ref):
    kv = _dot(c_ref[...].astype(BF16), w_ref[...])
    n_nope = N_HEADS * QK_NOPE
    _store_keys(k_ref, kv[:, :n_nope].astype(BF16), kr_ref[...])
    v_ref[...] = kv[:, n_nope:].astype(BF16)


def _ctx_expand(c_kv, k_rope, w_ukv):
    n = c_kv.shape[0]
    wide = N_HEADS * QK_NOPE
    return pl.pallas_call(
        _ctx_expand_body,
        grid=(n // ROW_BLOCK,),
        in_specs=[pl.BlockSpec((ROW_BLOCK, KV_RANK), lambda i: (i, 0)),
                  pl.BlockSpec((ROW_BLOCK, HEAD_LANES), lambda i: (i, 0)),
                  pl.BlockSpec(w_ukv.shape, lambda i: (0, 0))],
        out_specs=[pl.BlockSpec((ROW_BLOCK, N_HEADS * QK_TILE), lambda i: (i, 0)),
                   pl.BlockSpec((ROW_BLOCK, wide), lambda i: (i, 0))],
        out_shape=[jax.ShapeDtypeStruct((n, N_HEADS * QK_TILE), BF16), jax.ShapeDtypeStruct((n, wide), BF16)],
        compiler_params=_cparams(1),
        name="mla_ctx_expand",
    )(c_kv, k_rope, w_ukv)


def _attn_body(has_ctx, n_seq, *refs):
    refs = list(refs)
    q_ref, k_ref, v_ref = refs[:3]
    del refs[:3]
    if has_ctx:
        kc_ref, vc_ref = refs[:2]
        del refs[:2]
    x_ref, mod_ref, wo_ref, o_ref, heads_ref = refs
    n_q = q_ref.shape[0] // n_seq
    n_k = k_ref.shape[0] // n_seq
    for seq in range(n_seq):
        rows = slice(seq * n_q, (seq + 1) * n_q)
        keys = slice(seq * n_k, (seq + 1) * n_k)
        for hd in range(N_HEADS):
            qk = slice(hd * QK_TILE, (hd + 1) * QK_TILE)
            vcols = slice(hd * V_HEAD, (hd + 1) * V_HEAD)
            q = q_ref[rows, qk]
            s = _dot_nt(q, k_ref[keys, qk])
            m = jnp.max(s, axis=-1, keepdims=True)
            if has_ctx:
                sc = _dot_nt(q, kc_ref[:, qk])
                m = jnp.maximum(m, jnp.max(sc, axis=-1, keepdims=True))
            p = jnp.exp(s - m)
            l = jnp.sum(p, axis=-1, keepdims=True)
            o = _dot(p.astype(BF16), v_ref[keys, vcols])
            if has_ctx:
                pc = jnp.exp(sc - m)
                l = l + jnp.sum(pc, axis=-1, keepdims=True)
                o = o + _dot(pc.astype(BF16), vc_ref[:, vcols])
            heads_ref[rows, vcols] = (o / l).astype(BF16)
    y = _dot(heads_ref[...], wo_ref[...])
    o_ref[...] = x_ref[...] + _gate(mod_ref[...], 0) * y


def _mla_attend(x, mod, seq_len, q, k, v, w_o, ctx=None):
    n = x.shape[0]
    tq = min(ATT_Q_TILE, seq_len)
    q_tiles = seq_len // tq
    wide = N_HEADS * QK_NOPE
    has_ctx = ctx is not None
    batch = n // seq_len
    grouped = q_tiles == 1 and not has_ctx and mod.shape[0] == 1 and batch % ATT_SEQS == 0
    n_seq = ATT_SEQS if grouped else 1

    def q_rows(width):
        return pl.BlockSpec((n_seq * tq, width), lambda b, r: (b * q_tiles + r, 0))

    def seq_rows(length, width):
        mode = pl.Buffered(1) if q_tiles > 1 else None
        return pl.BlockSpec((n_seq * length, width), lambda b, r: (b, 0), pipeline_mode=mode)

    in_specs = [q_rows(N_HEADS * QK_TILE), seq_rows(seq_len, N_HEADS * QK_TILE), seq_rows(seq_len, wide)]
    args = [q, k, v]
    if has_ctx:
        kc, vc = ctx
        past = kc.shape[0] // (n // seq_len)
        in_specs += [seq_rows(past, N_HEADS * QK_TILE), seq_rows(past, wide)]
        args += [kc, vc]
    in_specs += [q_rows(D_MODEL),
                 pl.BlockSpec((None, 1, N_MOD * D_MODEL),
                              (lambda b, r: (b, 0, 0)) if mod.shape[0] > 1 else (lambda b, r: (0, 0, 0))),
                 pl.BlockSpec(w_o.shape, lambda b, r: (0, 0), pipeline_mode=pl.Buffered(1))]
    args += [x, mod, w_o]
    return pl.pallas_call(
        functools.partial(_attn_body, has_ctx, n_seq),
        grid=(batch // n_seq, q_tiles),
        in_specs=in_specs,
        out_specs=q_rows(D_MODEL),
        out_shape=jax.ShapeDtypeStruct((n, D_MODEL), F32),
        scratch_shapes=[pltpu.VMEM((n_seq * tq, wide), BF16)],
        compiler_params=_cparams(2),
        name="mla_attend",
    )(*args)


def _rope_table(n_tok):
    rows = n_tok // GRID_W
    row = np.repeat(np.arange(rows), GRID_W).astype(np.float32)
    col = np.tile(np.arange(GRID_W), rows).astype(np.float32)
    inv = (1.0 / (ROPE_BASE ** (np.arange(0, ROPE_AXIS, 2, dtype=np.float32) / ROPE_AXIS))).astype(np.float32)
    ang_r = row[:, None] * inv
    ang_c = col[:, None] * inv
    cos = np.concatenate([np.cos(ang_r), np.cos(ang_r), np.cos(ang_c), np.cos(ang_c)], axis=1)
    sin = np.concatenate([np.sin(ang_r), np.sin(ang_r), np.sin(ang_c), np.sin(ang_c)], axis=1)
    return jnp.asarray(np.concatenate([cos, sin], axis=1), F32)


def _swap_cols(w):
    half = ROPE_AXIS // 2
    return jnp.concatenate([-w[..., half:2 * half], w[..., 0:half],
                            -w[..., 3 * half:4 * half], w[..., 2 * half:3 * half]], axis=-1)


def _mla_weights(w_dq, g_q, w_uq, w_dkv, g_kv, w_ukv, w_o):
    wq = w_uq.reshape(Q_RANK, N_HEADS, QK_NOPE + QK_ROPE)
    wq_pe = wq[:, :, QK_NOPE:]
    wq_pairs = jnp.concatenate([wq_pe, _swap_cols(wq_pe)], axis=-1)
    w_uq2 = jnp.concatenate([wq[:, :, :QK_NOPE].reshape(Q_RANK, -1), wq_pairs.reshape(Q_RANK, -1)], axis=1)
    w_pe = w_dkv[:, KV_RANK:]
    w_dkv2 = jnp.concatenate([w_dkv[:, :KV_RANK], w_pe, _swap_cols(w_pe)], axis=1)
    wkv = w_ukv.reshape(KV_RANK, N_HEADS, QK_NOPE + V_HEAD)
    w_ukv2 = jnp.concatenate([wkv[:, :, :QK_NOPE].reshape(KV_RANK, -1),
                              wkv[:, :, QK_NOPE:].reshape(KV_RANK, -1)], axis=1)
    return {"w_dq": w_dq.astype(BF16), "g_q": g_q.reshape(1, Q_RANK), "w_uq": w_uq2.astype(BF16),
            "w_dkv": w_dkv2.astype(BF16), "g_kv": g_kv.reshape(1, KV_RANK), "w_ukv": w_ukv2.astype(BF16),
            "w_o": w_o.astype(BF16)}


def _cast_body(a_ref, b_ref, ao_ref, bo_ref):
    ao_ref[...] = a_ref[...].astype(BF16)
    bo_ref[...] = b_ref[...].astype(BF16)


def _ffn_first_weights(w_in, w_out):
    steps = D_FF // ROW_BLOCK
    cols_in = 2 * D_FF // steps
    return pl.pallas_call(
        _cast_body,
        grid=(steps,),
        in_specs=[pl.BlockSpec((None, D_MODEL, cols_in), lambda c: (0, 0, c)),
                  pl.BlockSpec((None, ROW_BLOCK, D_MODEL), lambda c: (0, c, 0))],
        out_specs=[pl.BlockSpec((D_MODEL, cols_in), lambda c: (0, c)),
                   pl.BlockSpec((ROW_BLOCK, D_MODEL), lambda c: (c, 0))],
        out_shape=[jax.ShapeDtypeStruct(w_in.shape[1:], BF16), jax.ShapeDtypeStruct(w_out.shape[1:], BF16)],
        compiler_params=_cparams(1),
        name="ffn_weight_cast",
    )(w_in, w_out)


def kernel(x_prompt, x_sample, cache_ckv, cache_kpe, c, c_ctx, w_ada, b_ada, w_fnet, w_pool, pool_scale,
           w_dq, g_q, w_uq, w_dkv, g_kv, w_ukv, w_o, w_ffn_in, w_ffn_conv, b_ffn_conv, w_ffn_out, g_final):
    batch, seq, _ = x_prompt.shape
    dec_batch, dec_seq, _ = x_sample.shape
    past = cache_ckv.shape[2]

    cond = jnp.concatenate([c_ctx[None, :], c, jnp.zeros((SUBLANES - 1 - dec_batch, D_MODEL), F32)], axis=0)
    b_ada = b_ada.reshape(DEPTH, 1, N_MOD * D_MODEL)
    mods = {0: _ada_first_layer(cond, w_ada, b_ada)}
    ada_layers_per_call = (batch * seq // FFN_TILE) // (N_MOD * D_MODEL // ADA_SHARE_COLS)
    w_in_bf, w_out_bf = _ffn_first_weights(w_ffn_in, w_ffn_out)
    b_conv = b_ffn_conv[:, None, :]

    xp = x_prompt.reshape(batch * seq, D_MODEL)
    xs = x_sample.reshape(dec_batch * dec_seq, D_MODEL)
    states = []
    j_f = j_p = j_a = 0
    for i in range(DEPTH):
        mod_p = mods[i][0:1].reshape(1, 1, N_MOD * D_MODEL)
        mod_s = mods[i][1:1 + dec_batch].reshape(dec_batch, 1, N_MOD * D_MODEL)
        kind = i % N_MIXERS
        if kind == 0:
            w = w_fnet[j_f].astype(BF16)
            xp = _fourier_mix(xp, mod_p, seq, w)
            xs = _fourier_mix(xs, mod_s, dec_seq, w)
            j_f += 1
        elif kind == 1:
            w = w_pool[j_p].astype(BF16)
            ps = pool_scale[j_p].reshape(1, D_MODEL)
            xp = _pool_mix(xp, mod_p, seq, w, ps)
            xs = _pool_mix(xs, mod_s, dec_seq, w, ps)
            j_p += 1
        else:
            prm = _mla_weights(w_dq[j_a], g_q[j_a], w_uq[j_a], w_dkv[j_a], g_kv[j_a], w_ukv[j_a], w_o[j_a])
            q, k, v, ckv_p, kpe_p = _mla_project(xp, mod_p, seq, prm, state=True)
            xp = _mla_attend(xp, mod_p, seq, q, k, v, prm["w_o"])
            states.append((ckv_p.reshape(batch, seq, KV_RANK), kpe_p.reshape(batch, seq, QK_ROPE)))
            q, k, v = _mla_project(xs, mod_s, dec_seq, prm, cs=_rope_table(dec_seq))
            kpe_c = cache_kpe[:, j_a].reshape(dec_batch * past, QK_ROPE)
            krc = jnp.concatenate([kpe_c, jnp.zeros_like(kpe_c)], axis=1).astype(BF16)
            kc, vc = _ctx_expand(cache_ckv[:, j_a].reshape(dec_batch * past, KV_RANK), krc, prm["w_ukv"])
            xs = _mla_attend(xs, mod_s, dec_seq, q, k, v, prm["w_o"], ctx=(kc, vc))
            j_a += 1
        g_fin = g_final.reshape(1, D_MODEL) if i == DEPTH - 1 else None
        ffn_w = (w_in_bf, w_ffn_conv, b_conv, w_out_bf)
        if i + 1 < DEPTH:
            todo = [l for l in range(i + 1, DEPTH) if l not in mods][:ada_layers_per_call]
            assert i + 1 in mods or (todo and todo[0] == i + 1)
            ada_later = (cond, w_ada, b_ada, tuple(todo)) if todo else None
            out = _conv_ffn(xp, mod_p, seq, i, *ffn_w, cast_next=(w_ffn_in, w_ffn_out), ada_later=ada_later)
            xp, w_in_bf, w_out_bf = out[:3]
            for n_done, l in enumerate(todo):
                mods[l] = out[3][n_done]
        else:
            xp, = _conv_ffn(xp, mod_p, seq, i, *ffn_w, g_final=g_fin)
        xs, = _conv_ffn(xs, mod_s, dec_seq, i, *ffn_w, g_final=g_fin)

    state_ckv = jnp.stack([s[0] for s in states], axis=1)
    state_kpe = jnp.stack([s[1] for s in states], axis=1)
    return (xp.reshape(batch, seq, D_MODEL), xs.reshape(dec_batch, dec_seq, D_MODEL), state_ckv, state_kpe)
```

```python
import functools

import numpy as np
import jax
import jax.numpy as jnp
from jax import lax
from jax.experimental import pallas as pl
from jax.experimental.pallas import tpu as pltpu

F32 = jnp.float32
BF16 = jnp.bfloat16

D_MODEL = 1024
DEPTH = 4
GRID_W = 64
N_MIXERS = 3
N_FOURIER_GROUPS = 4
FOURIER_GROUP = D_MODEL // N_FOURIER_GROUPS
POOL_WINDOWS = (2, 4, 8, 16)
POOL_GROUP = D_MODEL // len(POOL_WINDOWS)
N_HEADS = 8
QK_NOPE = 128
QK_ROPE = 64
V_HEAD = 128
Q_RANK = 768
KV_RANK = 512
ROPE_AXIS = QK_ROPE // 2
ROPE_BASE = 10000.0
SM_SCALE = (QK_NOPE + QK_ROPE) ** -0.5
D_FF = 2816
N_MOD = 6
EPS = 1e-6

SUBLANES = 8
LANES = 128
BF16_SUBLANES = 16
VMEM_LIMIT_BYTES = 56 * 1024 * 1024

ROW_BLOCK = 256
FFN_CHUNK = 512
FFN_TILE = 512
MLA_TILE = 1024
ATT_Q_TILE = 256
ADA_COLS = 2048
ADA_SHARE_COLS = 1024
POOL_HALO = 128
POOL_SEQS = 4
FNET_SEQS = 4
ATT_SEQS = 4
FNET_STEP_BLOCKS = 2
HEAD_LANES = 2 * QK_ROPE
QK_TILE = QK_NOPE + HEAD_LANES


def _cparams(n_axes):
    return pltpu.CompilerParams(dimension_semantics=("arbitrary",) * n_axes,
                                vmem_limit_bytes=VMEM_LIMIT_BYTES)


def _rms(x):
    return x * lax.rsqrt(jnp.mean(x * x, axis=-1, keepdims=True) + EPS)


def _modulate(x, mod, sub):
    shift = mod[:, (3 * sub) * D_MODEL:(3 * sub + 1) * D_MODEL]
    scale = mod[:, (3 * sub + 1) * D_MODEL:(3 * sub + 2) * D_MODEL]
    return _rms(x) * (1.0 + scale) + shift


def _gate(mod, sub):
    return mod[:, (3 * sub + 2) * D_MODEL:(3 * sub + 3) * D_MODEL]


def _dot(a, b):
    return jnp.dot(a, b, preferred_element_type=F32)


def _dot_nt(a, b):
    return lax.dot_general(a, b, (((1,), (1,)), ((), ())), preferred_element_type=F32)


def _ada_body(cond_ref, w_ref, b_ref, o_ref):
    a = jax.nn.silu(cond_ref[...]).astype(BF16)
    o_ref[...] = _dot(a, w_ref[...].astype(BF16)) + b_ref[...]


def _ada_first_layer(cond, w_ada, b_ada):
    n_out = N_MOD * D_MODEL
    return pl.pallas_call(
        _ada_body,
        grid=(n_out // ADA_COLS,),
        in_specs=[
            pl.BlockSpec((SUBLANES, D_MODEL), lambda j: (0, 0)),
            pl.BlockSpec((None, D_MODEL, ADA_COLS), lambda j: (0, 0, j)),
            pl.BlockSpec((None, 1, ADA_COLS), lambda j: (0, 0, j)),
        ],
        out_specs=pl.BlockSpec((SUBLANES, ADA_COLS), lambda j: (0, j)),
        out_shape=jax.ShapeDtypeStruct((SUBLANES, n_out), F32),
        compiler_params=_cparams(1),
        name="ada_mod",
    )(cond, w_ada, b_ada)


def _ffn_chunks():
    edges = list(range(0, D_FF, FFN_CHUNK)) + [D_FF]
    return list(zip(edges[:-1], edges[1:]))


def _ffn_body(seq_len, n_seg, seg, final, cast_next, ada_later, *refs):
    refs = list(refs)
    x_ref, xp_ref, xn_ref, mod_ref, win_ref, wc_ref, bc_ref, wout_ref = refs[:8]
    del refs[:8]
    gfin_ref = refs.pop(0) if final else None
    side_in = [refs.pop(0) for _ in range(2 * cast_next + 3 * ada_later)]
    o_ref = refs.pop(0)
    side_out = [refs.pop(0) for _ in range(2 * cast_next + ada_later)]
    if cast_next:
        side_out[0][...] = side_in[0][...].astype(BF16)
        side_out[1][...] = side_in[1][...].astype(BF16)
    if ada_later:
        _ada_body(*side_in[-3:], side_out[-1])
    hext_ref, uga_ref, uva_ref, ugb_ref, uvb_ref, acta_ref, actb_ref, acc_ref = refs
    tile = n_seg * seg
    gap = BF16_SUBLANES
    i = pl.program_id(0)
    mod = mod_ref[...]
    x = x_ref[...]

    start = i * tile
    prev_ok = (start % seq_len != 0).astype(F32)
    next_ok = ((start + tile) % seq_len != 0).astype(F32)
    zeros8 = jnp.zeros((SUBLANES, D_MODEL), F32)
    h_prev = _modulate(xp_ref[...], mod, 1) * prev_ok
    h_next = _modulate(xn_ref[...], mod, 1) * next_ok
    hext_ref[0:gap, :] = jnp.concatenate([zeros8, h_prev], axis=0).astype(BF16)
    h = _modulate(x, mod, 1).astype(BF16)
    for k in range(n_seg):
        base = gap + k * (seg + gap)
        hext_ref[base:base + seg, :] = h[k * seg:(k + 1) * seg, :]
        if k == n_seg - 1:
            tail = jnp.concatenate([h_next, zeros8], axis=0).astype(BF16)
        else:
            tail = jnp.zeros((gap, D_MODEL), BF16)
        hext_ref[base + seg:base + seg + gap, :] = tail
    acc_ref[...] = jnp.zeros_like(acc_ref)

    def up(cols, ug_ref, uv_ref):
        lo, hi = cols
        ug_ref[:, :hi - lo] = _dot(hext_ref[...], win_ref[:, lo:hi])
        uv_ref[:, :hi - lo] = _dot(hext_ref[...], win_ref[:, D_FF + lo:D_FF + hi])

    def conv(u_ref, b, lo, hi):
        wc = wc_ref[:, lo:hi]
        return (u_ref[b - 1:b - 1 + seg, :hi - lo] * wc[0:1, :] + u_ref[b:b + seg, :hi - lo] * wc[1:2, :]
                + u_ref[b + 1:b + 1 + seg, :hi - lo] * wc[2:3, :] + bc_ref[:, lo:hi])

    def down(cols, ug_ref, uv_ref, act_ref):
        lo, hi = cols
        for k in range(n_seg):
            b = gap + k * (seg + gap)
            a = jax.nn.silu(conv(ug_ref, b, lo, hi)) * conv(uv_ref, b, D_FF + lo, D_FF + hi)
            act_ref[k * seg:(k + 1) * seg, :hi - lo] = a.astype(BF16)
        acc_ref[...] += _dot(act_ref[:, :hi - lo], wout_ref[lo:hi, :])

    chunks = _ffn_chunks()
    u_bufs = ((uga_ref, uva_ref), (ugb_ref, uvb_ref))
    act_bufs = (acta_ref, actb_ref)
    up(chunks[0], *u_bufs[0])
    for c, cols in enumerate(chunks):
        if c + 1 < len(chunks):
            up(chunks[c + 1], *u_bufs[(c + 1) % 2])
        down(cols, *u_bufs[c % 2], act_bufs[c % 2])
    y = x + _gate(mod, 1) * acc_ref[...]
    if final:
        y = _rms(y) * gfin_ref[...]
    o_ref[...] = y


def _conv_ffn(x, mod, seq_len, layer, w_in, w_conv, b_conv, w_out, g_final=None, cast_next=None,
              ada_later=None):
    n = x.shape[0]
    tile = FFN_TILE
    seg = min(seq_len, tile)
    n_seg = tile // seg
    rows = BF16_SUBLANES + n_seg * (seg + BF16_SUBLANES)
    per_seq_mod = mod.shape[0] > 1
    final = g_final is not None
    blocks8 = n // SUBLANES
    t8 = tile // SUBLANES

    def resident(a):
        return pl.BlockSpec(a.shape, lambda i: (0,) * a.ndim, pipeline_mode=pl.Buffered(1))

    def resident_layer(a):
        shape = a.shape[1:]
        return pl.BlockSpec((None,) + shape, lambda i: (layer,) + (0,) * len(shape),
                            pipeline_mode=pl.Buffered(1))

    in_specs = [
        pl.BlockSpec((tile, D_MODEL), lambda i: (i, 0)),
        pl.BlockSpec((SUBLANES, D_MODEL), lambda i: (jnp.maximum(i * t8 - 1, 0), 0)),
        pl.BlockSpec((SUBLANES, D_MODEL), lambda i: (jnp.minimum((i + 1) * t8, blocks8 - 1), 0)),
        pl.BlockSpec((None, 1, N_MOD * D_MODEL),
                     (lambda i: ((i * tile) // seq_len, 0, 0)) if per_seq_mod else (lambda i: (0, 0, 0))),
        resident(w_in), resident_layer(w_conv), resident_layer(b_conv), resident(w_out),
    ]
    args = [x, x, x, mod, w_in, w_conv, b_conv, w_out]
    if final:
        in_specs.append(pl.BlockSpec((1, D_MODEL), lambda i: (0, 0)))
        args.append(g_final)
    out_specs = [pl.BlockSpec((tile, D_MODEL), lambda i: (i, 0))]
    out_shape = [jax.ShapeDtypeStruct((n, D_MODEL), F32)]
    if cast_next is not None:
        steps = n // tile
        for a in cast_next:
            rows_step = a.shape[1] // steps
            assert rows_step * steps == a.shape[1] and rows_step % BF16_SUBLANES == 0
            in_specs.append(pl.BlockSpec((None, rows_step, a.shape[2]), lambda i: (layer + 1, i, 0)))
            out_specs.append(pl.BlockSpec((rows_step, a.shape[2]), lambda i: (i, 0)))
            out_shape.append(jax.ShapeDtypeStruct(a.shape[1:], BF16))
            args.append(a)
    if ada_later is not None:
        cond, w_ada, b_ada, layers = ada_later
        blocks_per_layer = N_MOD * D_MODEL // ADA_SHARE_COLS
        n_share = len(layers) * blocks_per_layer
        assert n_share <= n // tile and list(layers) == list(range(layers[0], layers[0] + len(layers)))

        def share(i):
            s = jnp.minimum(i, n_share - 1)
            return s // blocks_per_layer, s % blocks_per_layer

        in_specs += [
            pl.BlockSpec((SUBLANES, D_MODEL), lambda i: (0, 0)),
            pl.BlockSpec((None, D_MODEL, ADA_SHARE_COLS), lambda i: (layers[0] + share(i)[0], 0, share(i)[1])),
            pl.BlockSpec((None, 1, ADA_SHARE_COLS), lambda i: (layers[0] + share(i)[0], 0, share(i)[1])),
        ]
        args += [cond, w_ada, b_ada]
        out_specs.append(pl.BlockSpec((None, SUBLANES, ADA_SHARE_COLS), lambda i: (share(i)[0], 0, share(i)[1])))
        out_shape.append(jax.ShapeDtypeStruct((len(layers), SUBLANES, N_MOD * D_MODEL), F32))
    return pl.pallas_call(
        functools.partial(_ffn_body, seq_len, n_seg, seg, final, cast_next is not None, ada_later is not None),
        grid=(n // tile,),
        in_specs=in_specs,
        out_specs=out_specs,
        out_shape=out_shape,
        scratch_shapes=[pltpu.VMEM((rows, D_MODEL), BF16)]
        + [pltpu.VMEM((rows, FFN_CHUNK), F32)] * 4
        + [pltpu.VMEM((tile, FFN_CHUNK), BF16)] * 2
        + [pltpu.VMEM((tile, D_MODEL), F32)],
        compiler_params=_cparams(1),
        name="conv_ffn",
    )(*args)


def _fnet_body(seq_len, n_seq, x_ref, mod_ref, csc_ref, csrow_ref, w_ref, o_ref):
    mod = mod_ref[...]
    x = x_ref[...]
    h = _modulate(x, mod, 0).astype(BF16)
    table = csc_ref[...].astype(BF16)
    parts = [_dot(h[:, g * FOURIER_GROUP:(g + 1) * FOURIER_GROUP], table) for g in range(N_FOURIER_GROUPS)]
    cos_part = jnp.concatenate([p[:, :FOURIER_GROUP] for p in parts], axis=1).astype(BF16)
    sin_part = jnp.concatenate([p[:, FOURIER_GROUP:] for p in parts], axis=1).astype(BF16)
    positions = csrow_ref[...].astype(BF16)
    scale = (seq_len * FOURIER_GROUP) ** -0.5
    f = []
    for s in range(n_seq):
        rows = slice(s * seq_len, (s + 1) * seq_len)
        stack = jnp.concatenate([cos_part[rows], sin_part[rows]], axis=0)
        f.append((_dot(positions, stack) * scale).astype(BF16))
    y = _dot(jnp.concatenate(f, axis=0), w_ref[...])
    o_ref[...] = x + _gate(mod, 0) * y


def _fnet_split_body(seq_len, x_ref, mod_ref, csc_ref, even_ref, odd_ref, perm_ref, w_ref, o_ref,
                     plus_ref, minus_ref):
    r = pl.program_id(1)
    mod = mod_ref[...]
    half = seq_len // 2

    @pl.when(r == 0)
    def _():
        def rows(j, carry):
            r0 = pl.multiple_of(j * ROW_BLOCK, ROW_BLOCK)
            h_lo = _modulate(x_ref[pl.ds(r0, ROW_BLOCK), :], mod, 0).astype(BF16)
            h_hi = _modulate(x_ref[pl.ds(half + r0, ROW_BLOCK), :], mod, 0).astype(BF16)
            table = csc_ref[...].astype(BF16)
            for g in range(N_FOURIER_GROUPS):
                cols = slice(g * FOURIER_GROUP, (g + 1) * FOURIER_GROUP)
                y_lo = _dot(h_lo[:, cols], table)
                y_hi = _dot(h_hi[:, cols], table)
                for ref, y in ((plus_ref, y_lo + y_hi), (minus_ref, y_lo - y_hi)):
                    ref[pl.ds(r0, ROW_BLOCK), cols] = y[:, :FOURIER_GROUP].astype(BF16)
                    ref[pl.ds(half + r0, ROW_BLOCK), cols] = y[:, FOURIER_GROUP:].astype(BF16)
            return carry
        lax.fori_loop(0, half // ROW_BLOCK, rows, 0)

    scale = (seq_len * FOURIER_GROUP) ** -0.5
    even = (_dot(even_ref[...].astype(BF16), plus_ref[...]) * scale).astype(BF16)
    odd = (_dot(odd_ref[...].astype(BF16), minus_ref[...]) * scale).astype(BF16)
    half_block = ROW_BLOCK // 2
    f = []
    for j in range(FNET_STEP_BLOCKS):
        rows = slice(j * half_block, (j + 1) * half_block)
        f.append(_dot(perm_ref[...], jnp.concatenate([even[rows], odd[rows]], axis=0)).astype(BF16))
    y = _dot(jnp.concatenate(f, axis=0), w_ref[...])
    step_rows = FNET_STEP_BLOCKS * ROW_BLOCK
    r0 = pl.multiple_of(r * step_rows, step_rows)
    o_ref[...] = x_ref[pl.ds(r0, step_rows), :] + _gate(mod, 0) * y


def _channel_table():
    c = np.arange(FOURIER_GROUP)
    ang = 2.0 * np.pi * ((c[:, None] * c[None, :]) % FOURIER_GROUP) / FOURIER_GROUP
    return jnp.asarray(np.concatenate([np.cos(ang), np.sin(ang)], axis=1), F32)


def _position_table(freqs, positions, seq_len):
    ang = 2.0 * np.pi * ((freqs[:, None] * positions[None, :]) % seq_len) / seq_len
    return jnp.asarray(np.concatenate([np.cos(ang), -np.sin(ang)], axis=1), F32)


def _fourier_mix(x, mod, seq_len, w):
    n = x.shape[0]
    batch = n // seq_len
    n_row_blocks = seq_len // ROW_BLOCK
    mod_spec = pl.BlockSpec((None, 1, N_MOD * D_MODEL),
                            (lambda b, r: (b, 0, 0)) if mod.shape[0] > 1 else (lambda b, r: (0, 0, 0)))
    common = dict(
        grid=(batch, n_row_blocks),
        out_specs=pl.BlockSpec((ROW_BLOCK, D_MODEL), lambda b, r: (b * n_row_blocks + r, 0)),
        out_shape=jax.ShapeDtypeStruct((n, D_MODEL), F32),
        compiler_params=_cparams(2),
        name="fourier_mix",
    )
    x_spec = pl.BlockSpec((seq_len, D_MODEL), lambda b, r: (b, 0))
    csc_spec = pl.BlockSpec((FOURIER_GROUP, 2 * FOURIER_GROUP), lambda b, r: (0, 0))
    w_spec = pl.BlockSpec((D_MODEL, D_MODEL), lambda b, r: (0, 0))
    t = np.arange(seq_len)
    if n_row_blocks == 1:
        n_seq = FNET_SEQS if (mod.shape[0] == 1 and batch % FNET_SEQS == 0) else 1
        rows = pl.BlockSpec((n_seq * seq_len, D_MODEL), lambda b, r: (b, 0))
        common.update(grid=(batch // n_seq, 1), out_specs=rows)
        return pl.pallas_call(
            functools.partial(_fnet_body, seq_len, n_seq),
            in_specs=[rows, mod_spec, csc_spec, pl.BlockSpec((seq_len, 2 * seq_len), lambda b, r: (0, 0)), w_spec],
            **common,
        )(x, mod, _channel_table(), _position_table(t, t, seq_len), w)
    half_block = ROW_BLOCK // 2
    u = np.arange(seq_len // 2)
    i = np.arange(half_block)
    perm = np.zeros((ROW_BLOCK, ROW_BLOCK), np.float32)
    perm[2 * i, i] = 1.0
    perm[2 * i + 1, half_block + i] = 1.0
    step_rows = FNET_STEP_BLOCKS * ROW_BLOCK
    steps = seq_len // step_rows
    table_spec = pl.BlockSpec((step_rows // 2, seq_len), lambda b, r: (r, 0))
    common.update(grid=(batch, steps),
                  out_specs=pl.BlockSpec((step_rows, D_MODEL), lambda b, r: (b * steps + r, 0)))
    return pl.pallas_call(
        functools.partial(_fnet_split_body, seq_len),
        in_specs=[x_spec, mod_spec, csc_spec, table_spec, table_spec,
                  pl.BlockSpec((ROW_BLOCK, ROW_BLOCK), lambda b, r: (0, 0)), w_spec],
        scratch_shapes=[pltpu.VMEM((seq_len, D_MODEL), BF16)] * 2,
        **common,
    )(x, mod, _channel_table(), _position_table(2 * u, u, seq_len), _position_table(2 * u + 1, u, seq_len),
      jnp.asarray(perm, BF16), w)


def _pool_bands():
    i = np.arange(ROW_BLOCK)[:, None]
    c = np.arange(2 * ROW_BLOCK)[None, :] - POOL_HALO
    bands = [((c >= i - w // 2) & (c < i - w // 2 + w)) for w in POOL_WINDOWS]
    return jnp.asarray(np.stack(bands), BF16)


def _pool_body(seq_len, n_seq, x_ref, mod_ref, band_ref, w_ref, ps_ref, o_ref, h_ref, hb_ref):
    mod = mod_ref[...]
    n_blocks = seq_len // ROW_BLOCK
    padded = seq_len + 2 * POOL_HALO
    zeros = jnp.zeros((POOL_HALO, D_MODEL), BF16)
    for s in range(n_seq):
        hb_ref[s * padded:s * padded + POOL_HALO, :] = zeros
        hb_ref[(s + 1) * padded - POOL_HALO:(s + 1) * padded, :] = zeros
        for j in range(n_blocks):
            rows = slice(s * seq_len + j * ROW_BLOCK, s * seq_len + (j + 1) * ROW_BLOCK)
            h = _modulate(x_ref[rows, :], mod, 0)
            h_ref[rows, :] = h
            start = s * padded + POOL_HALO + j * ROW_BLOCK
            hb_ref[start:start + ROW_BLOCK, :] = h.astype(BF16)

    gate = _gate(mod, 0)
    for s in range(n_seq):
        for j in range(n_blocks):
            rows = slice(s * seq_len + j * ROW_BLOCK, s * seq_len + (j + 1) * ROW_BLOCK)
            slab = slice(s * padded + j * ROW_BLOCK, s * padded + (j + 2) * ROW_BLOCK)
            t = j * ROW_BLOCK + lax.broadcasted_iota(jnp.int32, (ROW_BLOCK, POOL_GROUP), 0)
            for g, w in enumerate(POOL_WINDOWS):
                cols = slice(g * POOL_GROUP, (g + 1) * POOL_GROUP)
                total = _dot(band_ref[g], hb_ref[slab, cols])
                cnt = jnp.minimum(t - w // 2 + w, seq_len) - jnp.maximum(t - w // 2, 0)
                pooled = total / cnt.astype(F32) - h_ref[rows, cols]
                y = _dot(pooled.astype(BF16), w_ref[g]) * ps_ref[:, cols]
                o_ref[rows, cols] = x_ref[rows, cols] + gate[:, cols] * y


def _pool_mix(x, mod, seq_len, w_pool, pool_scale):
    n = x.shape[0]
    batch = n // seq_len
    assert POOL_HALO >= max(POOL_WINDOWS) // 2 and 2 * POOL_HALO == ROW_BLOCK
    n_seq = POOL_SEQS if (seq_len == ROW_BLOCK and mod.shape[0] == 1 and batch % POOL_SEQS == 0) else 1
    rows = n_seq * seq_len
    return pl.pallas_call(
        functools.partial(_pool_body, seq_len, n_seq),
        grid=(batch // n_seq,),
        in_specs=[
            pl.BlockSpec((rows, D_MODEL), lambda b: (b, 0)),
            pl.BlockSpec((None, 1, N_MOD * D_MODEL),
                         (lambda b: (b, 0, 0)) if mod.shape[0] > 1 else (lambda b: (0, 0, 0))),
            pl.BlockSpec((len(POOL_WINDOWS), ROW_BLOCK, 2 * ROW_BLOCK), lambda b: (0, 0, 0)),
            pl.BlockSpec((len(POOL_WINDOWS), POOL_GROUP, POOL_GROUP), lambda b: (0, 0, 0)),
            pl.BlockSpec((1, D_MODEL), lambda b: (0, 0)),
        ],
        out_specs=pl.BlockSpec((rows, D_MODEL), lambda b: (b, 0)),
        out_shape=jax.ShapeDtypeStruct((n, D_MODEL), F32),
        scratch_shapes=[pltpu.VMEM((rows, D_MODEL), F32),
                        pltpu.VMEM((n_seq * (seq_len + 2 * POOL_HALO), D_MODEL), BF16)],
        compiler_params=_cparams(1),
        name="pool_mix",
    )(x, mod, _pool_bands(), w_pool, pool_scale)


def _rope_pair(a, cs):
    t = a * cs
    return t + pltpu.roll(t, QK_ROPE, 1)


def _store_keys(k_ref, k_nope, k_rope):
    for hd in range(N_HEADS):
        k_ref[:, hd * QK_TILE:hd * QK_TILE + QK_NOPE] = k_nope[:, hd * QK_NOPE:(hd + 1) * QK_NOPE]
        k_ref[:, hd * QK_TILE + QK_NOPE:(hd + 1) * QK_TILE] = k_rope


def _mla_proj_body(rope, state, *refs):
    refs = list(refs)
    x_ref, mod_ref = refs[:2]
    del refs[:2]
    cs_ref = refs.pop(0) if rope else None
    wdq_ref, gq_ref, wuq_ref, wdkv_ref, gkv_ref, wukv_ref = refs[:6]
    del refs[:6]
    q_ref, k_ref, v_ref = refs[:3]
    del refs[:3]
    h = _modulate(x_ref[...], mod_ref[...], 0).astype(BF16)
    cq = (_rms(_dot(h, wdq_ref[...])) * gq_ref[...]).astype(BF16)
    q = _dot(cq, wuq_ref[...]) * SM_SCALE
    n_nope = N_HEADS * QK_NOPE
    for hd in range(N_HEADS):
        a = q[:, n_nope + hd * HEAD_LANES:n_nope + (hd + 1) * HEAD_LANES]
        if rope:
            a = _rope_pair(a, cs_ref[...])
        q_ref[:, hd * QK_TILE:hd * QK_TILE + QK_NOPE] = q[:, hd * QK_NOPE:(hd + 1) * QK_NOPE].astype(BF16)
        q_ref[:, hd * QK_TILE + QK_NOPE:(hd + 1) * QK_TILE] = a.astype(BF16)
    ckv = _dot(h, wdkv_ref[...])
    c = _rms(ckv[:, :KV_RANK]) * gkv_ref[...]
    kp = ckv[:, KV_RANK:]
    if state:
        ckv_out_ref, kpe_out_ref = refs
        ckv_out_ref[...] = c
        kpe_out_ref[...] = kp[:, :QK_ROPE]
    if rope:
        kp = _rope_pair(kp, cs_ref[...])
    lane = lax.broadcasted_iota(jnp.int32, kp.shape, 1)
    kr = jnp.where(lane < QK_ROPE, kp, 0.0).astype(BF16)
    kv = _dot(c.astype(BF16), wukv_ref[...])
    _store_keys(k_ref, kv[:, :n_nope].astype(BF16), kr)
    v_ref[...] = kv[:, n_nope:].astype(BF16)


def _mla_project(x, mod, seq_len, prm, cs=None, state=False):
    n = x.shape[0]
    rope = cs is not None
    tile = min(MLA_TILE, seq_len) if (rope or mod.shape[0] > 1) else MLA_TILE
    tiles_per_seq = max(seq_len // tile, 1)
    wide = N_HEADS * QK_NOPE

    def const(shape):
        return pl.BlockSpec(shape, lambda i: (0,) * len(shape), pipeline_mode=pl.Buffered(1))

    def rows(width):
        return pl.BlockSpec((tile, width), lambda i: (i, 0))

    in_specs = [rows(D_MODEL),
                pl.BlockSpec((None, 1, N_MOD * D_MODEL),
                             (lambda i: (i // tiles_per_seq, 0, 0)) if mod.shape[0] > 1
                             else (lambda i: (0, 0, 0)))]
    args = [x, mod]
    if rope:
        in_specs.append(pl.BlockSpec((tile, HEAD_LANES), lambda i: (i % tiles_per_seq, 0)))
        args.append(cs)
    for name in ("w_dq", "g_q", "w_uq", "w_dkv", "g_kv", "w_ukv"):
        in_specs.append(const(prm[name].shape))
        args.append(prm[name])
    out_specs = [rows(N_HEADS * QK_TILE), rows(N_HEADS * QK_TILE), rows(wide)]
    out_shape = [jax.ShapeDtypeStruct((n, N_HEADS * QK_TILE), BF16),
                 jax.ShapeDtypeStruct((n, N_HEADS * QK_TILE), BF16),
                 jax.ShapeDtypeStruct((n, wide), BF16)]
    if state:
        out_specs += [rows(KV_RANK), rows(QK_ROPE)]
        out_shape += [jax.ShapeDtypeStruct((n, KV_RANK), F32), jax.ShapeDtypeStruct((n, QK_ROPE), F32)]
    return pl.pallas_call(
        functools.partial(_mla_proj_body, rope, state),
        grid=(n // tile,),
        in_specs=in_specs,
        out_specs=out_specs,
        out_shape=out_shape,
        compiler_params=_cparams(1),
        name="mla_project",
    )(*args)


def _ctx_expand_body(c_ref, kr_ref, w_ref, k_ref, v_ref):
    kv = _dot(c_ref[...].astype(BF16), w_ref[...])
    n_nope = N_HEADS * QK_NOPE
    _store_keys(k_ref, kv[:, :n_nope].astype(BF16), kr_ref[...])
    v_ref[...] = kv[:, n_nope:].astype(BF16)


def _ctx_expand(c_kv, k_rope, w_ukv):
    n = c_kv.shape[0]
    wide = N_HEADS * QK_NOPE
    return pl.pallas_call(
        _ctx_expand_body,
        grid=(n // ROW_BLOCK,),
        in_specs=[pl.BlockSpec((ROW_BLOCK, KV_RANK), lambda i: (i, 0)),
                  pl.BlockSpec((ROW_BLOCK, HEAD_LANES), lambda i: (i, 0)),
                  pl.BlockSpec(w_ukv.shape, lambda i: (0, 0))],
        out_specs=[pl.BlockSpec((ROW_BLOCK, N_HEADS * QK_TILE), lambda i: (i, 0)),
                   pl.BlockSpec((ROW_BLOCK, wide), lambda i: (i, 0))],
        out_shape=[jax.ShapeDtypeStruct((n, N_HEADS * QK_TILE), BF16), jax.ShapeDtypeStruct((n, wide), BF16)],
        compiler_params=_cparams(1),
        name="mla_ctx_expand",
    )(c_kv, k_rope, w_ukv)


def _attn_body(has_ctx, n_seq, *refs):
    refs = list(refs)
    q_ref, k_ref, v_ref = refs[:3]
    del refs[:3]
    if has_ctx:
        kc_ref, vc_ref = refs[:2]
        del refs[:2]
    x_ref, mod_ref, wo_ref, o_ref, heads_ref = refs
    n_q = q_ref.shape[0] // n_seq
    n_k = k_ref.shape[0] // n_seq
    for seq in range(n_seq):
        rows = slice(seq * n_q, (seq + 1) * n_q)
        keys = slice(seq * n_k, (seq + 1) * n_k)
        for hd in range(N_HEADS):
            qk = slice(hd * QK_TILE, (hd + 1) * QK_TILE)
            vcols = slice(hd * V_HEAD, (hd + 1) * V_HEAD)
            q = q_ref[rows, qk]
            s = _dot_nt(q, k_ref[keys, qk])
            m = jnp.max(s, axis=-1, keepdims=True)
            if has_ctx:
                sc = _dot_nt(q, kc_ref[:, qk])
                m = jnp.maximum(m, jnp.max(sc, axis=-1, keepdims=True))
            p = jnp.exp(s - m)
            l = jnp.sum(p, axis=-1, keepdims=True)
            o = _dot(p.astype(BF16), v_ref[keys, vcols])
            if has_ctx:
                pc = jnp.exp(sc - m)
                l = l + jnp.sum(pc, axis=-1, keepdims=True)
                o = o + _dot(pc.astype(BF16), vc_ref[:, vcols])
            heads_ref[rows, vcols] = (o / l).astype(BF16)
    y = _dot(heads_ref[...], wo_ref[...])
    o_ref[...] = x_ref[...] + _gate(mod_ref[...], 0) * y


def _mla_attend(x, mod, seq_len, q, k, v, w_o, ctx=None):
    n = x.shape[0]
    tq = min(ATT_Q_TILE, seq_len)
    q_tiles = seq_len // tq
    wide = N_HEADS * QK_NOPE
    has_ctx = ctx is not None
    batch = n // seq_len
    grouped = q_tiles == 1 and not has_ctx and mod.shape[0] == 1 and batch % ATT_SEQS == 0
    n_seq = ATT_SEQS if grouped else 1

    def q_rows(width):
        return pl.BlockSpec((n_seq * tq, width), lambda b, r: (b * q_tiles + r, 0))

    def seq_rows(length, width):
        return pl.BlockSpec((n_seq * length, width), lambda b, r: (b, 0))

    in_specs = [q_rows(N_HEADS * QK_TILE), seq_rows(seq_len, N_HEADS * QK_TILE), seq_rows(seq_len, wide)]
    args = [q, k, v]
    if has_ctx:
        kc, vc = ctx
        past = kc.shape[0] // (n // seq_len)
        in_specs += [seq_rows(past, N_HEADS * QK_TILE), seq_rows(past, wide)]
        args += [kc, vc]
    in_specs += [q_rows(D_MODEL),
                 pl.BlockSpec((None, 1, N_MOD * D_MODEL),
                              (lambda b, r: (b, 0, 0)) if mod.shape[0] > 1 else (lambda b, r: (0, 0, 0))),
                 pl.BlockSpec(w_o.shape, lambda b, r: (0, 0))]
    args += [x, mod, w_o]
    return pl.pallas_call(
        functools.partial(_attn_body, has_ctx, n_seq),
        grid=(batch // n_seq, q_tiles),
        in_specs=in_specs,
        out_specs=q_rows(D_MODEL),
        out_shape=jax.ShapeDtypeStruct((n, D_MODEL), F32),
        scratch_shapes=[pltpu.VMEM((n_seq * tq, wide), BF16)],
        compiler_params=_cparams(2),
        name="mla_attend",
    )(*args)


def _rope_table(n_tok):
    rows = n_tok // GRID_W
    row = np.repeat(np.arange(rows), GRID_W).astype(np.float32)
    col = np.tile(np.arange(GRID_W), rows).astype(np.float32)
    inv = (1.0 / (ROPE_BASE ** (np.arange(0, ROPE_AXIS, 2, dtype=np.float32) / ROPE_AXIS))).astype(np.float32)
    ang_r = row[:, None] * inv
    ang_c = col[:, None] * inv
    cos = np.concatenate([np.cos(ang_r), np.cos(ang_r), np.cos(ang_c), np.cos(ang_c)], axis=1)
    sin = np.concatenate([np.sin(ang_r), np.sin(ang_r), np.sin(ang_c), np.sin(ang_c)], axis=1)
    return jnp.asarray(np.concatenate([cos, sin], axis=1), F32)


def _swap_cols(w):
    half = ROPE_AXIS // 2
    return jnp.concatenate([-w[..., half:2 * half], w[..., 0:half],
                            -w[..., 3 * half:4 * half], w[..., 2 * half:3 * half]], axis=-1)


def _mla_weights(w_dq, g_q, w_uq, w_dkv, g_kv, w_ukv, w_o):
    wq = w_uq.reshape(Q_RANK, N_HEADS, QK_NOPE + QK_ROPE)
    wq_pe = wq[:, :, QK_NOPE:]
    wq_pairs = jnp.concatenate([wq_pe, _swap_cols(wq_pe)], axis=-1)
    w_uq2 = jnp.concatenate([wq[:, :, :QK_NOPE].reshape(Q_RANK, -1), wq_pairs.reshape(Q_RANK, -1)], axis=1)
    w_pe = w_dkv[:, KV_RANK:]
    w_dkv2 = jnp.concatenate([w_dkv[:, :KV_RANK], w_pe, _swap_cols(w_pe)], axis=1)
    wkv = w_ukv.reshape(KV_RANK, N_HEADS, QK_NOPE + V_HEAD)
    w_ukv2 = jnp.concatenate([wkv[:, :, :QK_NOPE].reshape(KV_RANK, -1),
                              wkv[:, :, QK_NOPE:].reshape(KV_RANK, -1)], axis=1)
    return {"w_dq": w_dq.astype(BF16), "g_q": g_q.reshape(1, Q_RANK), "w_uq": w_uq2.astype(BF16),
            "w_dkv": w_dkv2.astype(BF16), "g_kv": g_kv.reshape(1, KV_RANK), "w_ukv": w_ukv2.astype(BF16),
            "w_o": w_o.astype(BF16)}


def _cast_body(a_ref, b_ref, ao_ref, bo_ref):
    ao_ref[...] = a_ref[...].astype(BF16)
    bo_ref[...] = b_ref[...].astype(BF16)


def _ffn_first_weights(w_in, w_out):
    steps = D_FF // ROW_BLOCK
    cols_in = 2 * D_FF // steps
    return pl.pallas_call(
        _cast_body,
        grid=(steps,),
        in_specs=[pl.BlockSpec((None, D_MODEL, cols_in), lambda c: (0, 0, c)),
                  pl.BlockSpec((None, ROW_BLOCK, D_MODEL), lambda c: (0, c, 0))],
        out_specs=[pl.BlockSpec((D_MODEL, cols_in), lambda c: (0, c)),
                   pl.BlockSpec((ROW_BLOCK, D_MODEL), lambda c: (c, 0))],
        out_shape=[jax.ShapeDtypeStruct(w_in.shape[1:], BF16), jax.ShapeDtypeStruct(w_out.shape[1:], BF16)],
        compiler_params=_cparams(1),
        name="ffn_weight_cast",
    )(w_in, w_out)


def kernel(x_prompt, x_sample, cache_ckv, cache_kpe, c, c_ctx, w_ada, b_ada, w_fnet, w_pool, pool_scale,
           w_dq, g_q, w_uq, w_dkv, g_kv, w_ukv, w_o, w_ffn_in, w_ffn_conv, b_ffn_conv, w_ffn_out, g_final):
    batch, seq, _ = x_prompt.shape
    dec_batch, dec_seq, _ = x_sample.shape
    past = cache_ckv.shape[2]

    cond = jnp.concatenate([c_ctx[None, :], c, jnp.zeros((SUBLANES - 1 - dec_batch, D_MODEL), F32)], axis=0)
    b_ada = b_ada.reshape(DEPTH, 1, N_MOD * D_MODEL)
    mods = {0: _ada_first_layer(cond, w_ada, b_ada)}
    ada_layers_per_call = (batch * seq // FFN_TILE) // (N_MOD * D_MODEL // ADA_SHARE_COLS)
    w_in_bf, w_out_bf = _ffn_first_weights(w_ffn_in, w_ffn_out)
    b_conv = b_ffn_conv[:, None, :]

    xp = x_prompt.reshape(batch * seq, D_MODEL)
    xs = x_sample.reshape(dec_batch * dec_seq, D_MODEL)
    states = []
    j_f = j_p = j_a = 0
    for i in range(DEPTH):
        mod_p = mods[i][0:1].reshape(1, 1, N_MOD * D_MODEL)
        mod_s = mods[i][1:1 + dec_batch].reshape(dec_batch, 1, N_MOD * D_MODEL)
        kind = i % N_MIXERS
        if kind == 0:
            w = w_fnet[j_f].astype(BF16)
            xp = _fourier_mix(xp, mod_p, seq, w)
            xs = _fourier_mix(xs, mod_s, dec_seq, w)
            j_f += 1
        elif kind == 1:
            w = w_pool[j_p].astype(BF16)
            ps = pool_scale[j_p].reshape(1, D_MODEL)
            xp = _pool_mix(xp, mod_p, seq, w, ps)
            xs = _pool_mix(xs, mod_s, dec_seq, w, ps)
            j_p += 1
        else:
            prm = _mla_weights(w_dq[j_a], g_q[j_a], w_uq[j_a], w_dkv[j_a], g_kv[j_a], w_ukv[j_a], w_o[j_a])
            q, k, v, ckv_p, kpe_p = _mla_project(xp, mod_p, seq, prm, state=True)
            xp = _mla_attend(xp, mod_p, seq, q, k, v, prm["w_o"])
            states.append((ckv_p.reshape(batch, seq, KV_RANK), kpe_p.reshape(batch, seq, QK_ROPE)))
            q, k, v = _mla_project(xs, mod_s, dec_seq, prm, cs=_rope_table(dec_seq))
            kpe_c = cache_kpe[:, j_a].reshape(dec_batch * past, QK_ROPE)
            krc = jnp.concatenate([kpe_c, jnp.zeros_like(kpe_c)], axis=1).astype(BF16)
            kc, vc = _ctx_expand(cache_ckv[:, j_a].reshape(dec_batch * past, KV_RANK), krc, prm["w_ukv"])
            xs = _mla_attend(xs, mod_s, dec_seq, q, k, v, prm["w_o"], ctx=(kc, vc))
            j_a += 1
        g_fin = g_final.reshape(1, D_MODEL) if i == DEPTH - 1 else None
        ffn_w = (w_in_bf, w_ffn_conv, b_conv, w_out_bf)
        if i + 1 < DEPTH:
            todo = [l for l in range(i + 1, DEPTH) if l not in mods][:ada_layers_per_call]
            assert i + 1 in mods or (todo and todo[0] == i + 1)
            ada_later = (cond, w_ada, b_ada, tuple(todo)) if todo else None
            out = _conv_ffn(xp, mod_p, seq, i, *ffn_w, cast_next=(w_ffn_in, w_ffn_out), ada_later=ada_later)
            xp, w_in_bf, w_out_bf = out[:3]
            for n_done, l in enumerate(todo):
                mods[l] = out[3][n_done]
        else:
            xp, = _conv_ffn(xp, mod_p, seq, i, *ffn_w, g_final=g_fin)
        xs, = _conv_ffn(xs, mod_s, dec_seq, i, *ffn_w, g_final=g_fin)

    state_ckv = jnp.stack([s[0] for s in states], axis=1)
    state_kpe = jnp.stack([s[1] for s in states], axis=1)
    return (xp.reshape(batch, seq, D_MODEL), xs.reshape(dec_batch, dec_seq, D_MODEL), state_ckv, state_kpe)
```

```python
import functools

import numpy as np
import jax
import jax.numpy as jnp
from jax import lax
from jax.experimental import pallas as pl
from jax.experimental.pallas import tpu as pltpu

F32 = jnp.float32
BF16 = jnp.bfloat16

D_MODEL = 1024
DEPTH = 4
GRID_W = 64
N_MIXERS = 3
N_FOURIER_GROUPS = 4
FOURIER_GROUP = D_MODEL // N_FOURIER_GROUPS
POOL_WINDOWS = (2, 4, 8, 16)
POOL_GROUP = D_MODEL // len(POOL_WINDOWS)
N_HEADS = 8
QK_NOPE = 128
QK_ROPE = 64
V_HEAD = 128
Q_RANK = 768
KV_RANK = 512
ROPE_AXIS = QK_ROPE // 2
ROPE_BASE = 10000.0
SM_SCALE = (QK_NOPE + QK_ROPE) ** -0.5
D_FF = 2816
N_MOD = 6
EPS = 1e-6

SUBLANES = 8
LANES = 128
BF16_SUBLANES = 16
VMEM_LIMIT_BYTES = 56 * 1024 * 1024
ATT_VMEM_LIMIT_BYTES = 58 * 1024 * 1024

ROW_BLOCK = 256
FFN_CHUNK = 512
FFN_TILE = 512
MLA_TILE = 1024
ATT_Q_TILE = 512
ADA_COLS = 2048
ADA_SHARE_COLS = 1024
POOL_HALO = 128
POOL_SEQS = 4
FNET_SEQS = 4
ATT_SEQS = 4
FNET_STEP_BLOCKS = 2
HEAD_LANES = 2 * QK_ROPE
QK_TILE = QK_NOPE + HEAD_LANES


def _cparams(n_axes, vmem_limit_bytes=VMEM_LIMIT_BYTES):
    return pltpu.CompilerParams(dimension_semantics=("arbitrary",) * n_axes,
                                vmem_limit_bytes=vmem_limit_bytes)


def _rms(x):
    return x * lax.rsqrt(jnp.mean(x * x, axis=-1, keepdims=True) + EPS)


def _modulate(x, mod, sub):
    shift = mod[:, (3 * sub) * D_MODEL:(3 * sub + 1) * D_MODEL]
    scale = mod[:, (3 * sub + 1) * D_MODEL:(3 * sub + 2) * D_MODEL]
    return _rms(x) * (1.0 + scale) + shift


def _gate(mod, sub):
    return mod[:, (3 * sub + 2) * D_MODEL:(3 * sub + 3) * D_MODEL]


def _dot(a, b):
    return jnp.dot(a, b, preferred_element_type=F32)


def _dot_nt(a, b):
    return lax.dot_general(a, b, (((1,), (1,)), ((), ())), preferred_element_type=F32)


def _ada_body(cond_ref, w_ref, b_ref, o_ref):
    a = jax.nn.silu(cond_ref[...]).astype(BF16)
    o_ref[...] = _dot(a, w_ref[...].astype(BF16)) + b_ref[...]


def _ada_first_layer(cond, w_ada, b_ada):
    n_out = N_MOD * D_MODEL
    return pl.pallas_call(
        _ada_body,
        grid=(n_out // ADA_COLS,),
        in_specs=[
            pl.BlockSpec((SUBLANES, D_MODEL), lambda j: (0, 0)),
            pl.BlockSpec((None, D_MODEL, ADA_COLS), lambda j: (0, 0, j)),
            pl.BlockSpec((None, 1, ADA_COLS), lambda j: (0, 0, j)),
        ],
        out_specs=pl.BlockSpec((SUBLANES, ADA_COLS), lambda j: (0, j)),
        out_shape=jax.ShapeDtypeStruct((SUBLANES, n_out), F32),
        compiler_params=_cparams(1),
        name="ada_mod",
    )(cond, w_ada, b_ada)


def _ffn_chunks():
    edges = list(range(0, D_FF, FFN_CHUNK)) + [D_FF]
    return list(zip(edges[:-1], edges[1:]))


def _ffn_body(seq_len, n_seg, seg, final, cast_next, ada_later, *refs):
    refs = list(refs)
    x_ref, xp_ref, xn_ref, mod_ref, win_ref, wc_ref, bc_ref, wout_ref = refs[:8]
    del refs[:8]
    gfin_ref = refs.pop(0) if final else None
    side_in = [refs.pop(0) for _ in range(2 * cast_next + 3 * ada_later)]
    o_ref = refs.pop(0)
    side_out = [refs.pop(0) for _ in range(2 * cast_next + ada_later)]
    if cast_next:
        side_out[0][...] = side_in[0][...].astype(BF16)
        side_out[1][...] = side_in[1][...].astype(BF16)
    if ada_later:
        _ada_body(*side_in[-3:], side_out[-1])
    hext_ref, uga_ref, uva_ref, ugb_ref, uvb_ref, acta_ref, actb_ref, acc_ref = refs
    tile = n_seg * seg
    gap = BF16_SUBLANES
    i = pl.program_id(0)
    mod = mod_ref[...]
    x = x_ref[...]

    start = i * tile
    prev_ok = (start % seq_len != 0).astype(F32)
    next_ok = ((start + tile) % seq_len != 0).astype(F32)
    zeros8 = jnp.zeros((SUBLANES, D_MODEL), F32)
    h_prev = _modulate(xp_ref[...], mod, 1) * prev_ok
    h_next = _modulate(xn_ref[...], mod, 1) * next_ok
    hext_ref[0:gap, :] = jnp.concatenate([zeros8, h_prev], axis=0).astype(BF16)
    h = _modulate(x, mod, 1).astype(BF16)
    for k in range(n_seg):
        base = gap + k * (seg + gap)
        hext_ref[base:base + seg, :] = h[k * seg:(k + 1) * seg, :]
        if k == n_seg - 1:
            tail = jnp.concatenate([h_next, zeros8], axis=0).astype(BF16)
        else:
            tail = jnp.zeros((gap, D_MODEL), BF16)
        hext_ref[base + seg:base + seg + gap, :] = tail
    acc_ref[...] = jnp.zeros_like(acc_ref)

    def up(cols, ug_ref, uv_ref):
        lo, hi = cols
        ug_ref[:, :hi - lo] = _dot(hext_ref[...], win_ref[:, lo:hi])
        uv_ref[:, :hi - lo] = _dot(hext_ref[...], win_ref[:, D_FF + lo:D_FF + hi])

    def conv(u_ref, b, lo, hi):
        wc = wc_ref[:, lo:hi]
        return (u_ref[b - 1:b - 1 + seg, :hi - lo] * wc[0:1, :] + u_ref[b:b + seg, :hi - lo] * wc[1:2, :]
                + u_ref[b + 1:b + 1 + seg, :hi - lo] * wc[2:3, :] + bc_ref[:, lo:hi])

    def down(cols, ug_ref, uv_ref, act_ref):
        lo, hi = cols
        for k in range(n_seg):
            b = gap + k * (seg + gap)
            a = jax.nn.silu(conv(ug_ref, b, lo, hi)) * conv(uv_ref, b, D_FF + lo, D_FF + hi)
            act_ref[k * seg:(k + 1) * seg, :hi - lo] = a.astype(BF16)
        acc_ref[...] += _dot(act_ref[:, :hi - lo], wout_ref[lo:hi, :])

    chunks = _ffn_chunks()
    u_bufs = ((uga_ref, uva_ref), (ugb_ref, uvb_ref))
    act_bufs = (acta_ref, actb_ref)
    up(chunks[0], *u_bufs[0])
    for c, cols in enumerate(chunks):
        if c + 1 < len(chunks):
            up(chunks[c + 1], *u_bufs[(c + 1) % 2])
        down(cols, *u_bufs[c % 2], act_bufs[c % 2])
    y = x + _gate(mod, 1) * acc_ref[...]
    if final:
        y = _rms(y) * gfin_ref[...]
    o_ref[...] = y


def _conv_ffn(x, mod, seq_len, layer, w_in, w_conv, b_conv, w_out, g_final=None, cast_next=None,
              ada_later=None):
    n = x.shape[0]
    tile = FFN_TILE
    seg = min(seq_len, tile)
    n_seg = tile // seg
    rows = BF16_SUBLANES + n_seg * (seg + BF16_SUBLANES)
    per_seq_mod = mod.shape[0] > 1
    final = g_final is not None
    blocks8 = n // SUBLANES
    t8 = tile // SUBLANES

    def resident(a):
        return pl.BlockSpec(a.shape, lambda i: (0,) * a.ndim, pipeline_mode=pl.Buffered(1))

    def resident_layer(a):
        shape = a.shape[1:]
        return pl.BlockSpec((None,) + shape, lambda i: (layer,) + (0,) * len(shape),
                            pipeline_mode=pl.Buffered(1))

    in_specs = [
        pl.BlockSpec((tile, D_MODEL), lambda i: (i, 0)),
        pl.BlockSpec((SUBLANES, D_MODEL), lambda i: (jnp.maximum(i * t8 - 1, 0), 0)),
        pl.BlockSpec((SUBLANES, D_MODEL), lambda i: (jnp.minimum((i + 1) * t8, blocks8 - 1), 0)),
        pl.BlockSpec((None, 1, N_MOD * D_MODEL),
                     (lambda i: ((i * tile) // seq_len, 0, 0)) if per_seq_mod else (lambda i: (0, 0, 0))),
        resident(w_in), resident_layer(w_conv), resident_layer(b_conv), resident(w_out),
    ]
    args = [x, x, x, mod, w_in, w_conv, b_conv, w_out]
    if final:
        in_specs.append(pl.BlockSpec((1, D_MODEL), lambda i: (0, 0)))
        args.append(g_final)
    out_specs = [pl.BlockSpec((tile, D_MODEL), lambda i: (i, 0))]
    out_shape = [jax.ShapeDtypeStruct((n, D_MODEL), F32)]
    if cast_next is not None:
        steps = n // tile
        for a in cast_next:
            rows_step = a.shape[1] // steps
            assert rows_step * steps == a.shape[1] and rows_step % BF16_SUBLANES == 0
            in_specs.append(pl.BlockSpec((None, rows_step, a.shape[2]), lambda i: (layer + 1, i, 0)))
            out_specs.append(pl.BlockSpec((rows_step, a.shape[2]), lambda i: (i, 0)))
            out_shape.append(jax.ShapeDtypeStruct(a.shape[1:], BF16))
            args.append(a)
    if ada_later is not None:
        cond, w_ada, b_ada, layers = ada_later
        blocks_per_layer = N_MOD * D_MODEL // ADA_SHARE_COLS
        n_share = len(layers) * blocks_per_layer
        assert n_share <= n // tile and list(layers) == list(range(layers[0], layers[0] + len(layers)))

        def share(i):
            s = jnp.minimum(i, n_share - 1)
            return s // blocks_per_layer, s % blocks_per_layer

        in_specs += [
            pl.BlockSpec((SUBLANES, D_MODEL), lambda i: (0, 0)),
            pl.BlockSpec((None, D_MODEL, ADA_SHARE_COLS), lambda i: (layers[0] + share(i)[0], 0, share(i)[1])),
            pl.BlockSpec((None, 1, ADA_SHARE_COLS), lambda i: (layers[0] + share(i)[0], 0, share(i)[1])),
        ]
        args += [cond, w_ada, b_ada]
        out_specs.append(pl.BlockSpec((None, SUBLANES, ADA_SHARE_COLS), lambda i: (share(i)[0], 0, share(i)[1])))
        out_shape.append(jax.ShapeDtypeStruct((len(layers), SUBLANES, N_MOD * D_MODEL), F32))
    return pl.pallas_call(
        functools.partial(_ffn_body, seq_len, n_seg, seg, final, cast_next is not None, ada_later is not None),
        grid=(n // tile,),
        in_specs=in_specs,
        out_specs=out_specs,
        out_shape=out_shape,
        scratch_shapes=[pltpu.VMEM((rows, D_MODEL), BF16)]
        + [pltpu.VMEM((rows, FFN_CHUNK), F32)] * 4
        + [pltpu.VMEM((tile, FFN_CHUNK), BF16)] * 2
        + [pltpu.VMEM((tile, D_MODEL), F32)],
        compiler_params=_cparams(1),
        name="conv_ffn",
    )(*args)


def _fnet_body(seq_len, n_seq, x_ref, mod_ref, csc_ref, csrow_ref, w_ref, o_ref):
    mod = mod_ref[...]
    x = x_ref[...]
    h = _modulate(x, mod, 0).astype(BF16)
    table = csc_ref[...].astype(BF16)
    parts = [_dot(h[:, g * FOURIER_GROUP:(g + 1) * FOURIER_GROUP], table) for g in range(N_FOURIER_GROUPS)]
    cos_part = jnp.concatenate([p[:, :FOURIER_GROUP] for p in parts], axis=1).astype(BF16)
    sin_part = jnp.concatenate([p[:, FOURIER_GROUP:] for p in parts], axis=1).astype(BF16)
    positions = csrow_ref[...].astype(BF16)
    scale = (seq_len * FOURIER_GROUP) ** -0.5
    f = []
    for s in range(n_seq):
        rows = slice(s * seq_len, (s + 1) * seq_len)
        stack = jnp.concatenate([cos_part[rows], sin_part[rows]], axis=0)
        f.append((_dot(positions, stack) * scale).astype(BF16))
    y = _dot(jnp.concatenate(f, axis=0), w_ref[...])
    o_ref[...] = x + _gate(mod, 0) * y


def _fnet_split_body(seq_len, x_ref, mod_ref, csc_ref, even_ref, odd_ref, perm_ref, w_ref, o_ref,
                     plus_ref, minus_ref):
    r = pl.program_id(1)
    mod = mod_ref[...]
    half = seq_len // 2

    @pl.when(r == 0)
    def _():
        def rows(j, carry):
            r0 = pl.multiple_of(j * ROW_BLOCK, ROW_BLOCK)
            h_lo = _modulate(x_ref[pl.ds(r0, ROW_BLOCK), :], mod, 0).astype(BF16)
            h_hi = _modulate(x_ref[pl.ds(half + r0, ROW_BLOCK), :], mod, 0).astype(BF16)
            table = csc_ref[...].astype(BF16)
            for g in range(N_FOURIER_GROUPS):
                cols = slice(g * FOURIER_GROUP, (g + 1) * FOURIER_GROUP)
                y_lo = _dot(h_lo[:, cols], table)
                y_hi = _dot(h_hi[:, cols], table)
                for ref, y in ((plus_ref, y_lo + y_hi), (minus_ref, y_lo - y_hi)):
                    ref[pl.ds(r0, ROW_BLOCK), cols] = y[:, :FOURIER_GROUP].astype(BF16)
                    ref[pl.ds(half + r0, ROW_BLOCK), cols] = y[:, FOURIER_GROUP:].astype(BF16)
            return carry
        lax.fori_loop(0, half // ROW_BLOCK, rows, 0)

    scale = (seq_len * FOURIER_GROUP) ** -0.5
    even = (_dot(even_ref[...].astype(BF16), plus_ref[...]) * scale).astype(BF16)
    odd = (_dot(odd_ref[...].astype(BF16), minus_ref[...]) * scale).astype(BF16)
    half_block = ROW_BLOCK // 2
    f = []
    for j in range(FNET_STEP_BLOCKS):
        rows = slice(j * half_block, (j + 1) * half_block)
        f.append(_dot(perm_ref[...], jnp.concatenate([even[rows], odd[rows]], axis=0)).astype(BF16))
    y = _dot(jnp.concatenate(f, axis=0), w_ref[...])
    step_rows = FNET_STEP_BLOCKS * ROW_BLOCK
    r0 = pl.multiple_of(r * step_rows, step_rows)
    o_ref[...] = x_ref[pl.ds(r0, step_rows), :] + _gate(mod, 0) * y


def _channel_table():
    c = np.arange(FOURIER_GROUP)
    ang = 2.0 * np.pi * ((c[:, None] * c[None, :]) % FOURIER_GROUP) / FOURIER_GROUP
    return jnp.asarray(np.concatenate([np.cos(ang), np.sin(ang)], axis=1), F32)


def _position_table(freqs, positions, seq_len):
    ang = 2.0 * np.pi * ((freqs[:, None] * positions[None, :]) % seq_len) / seq_len
    return jnp.asarray(np.concatenate([np.cos(ang), -np.sin(ang)], axis=1), F32)


def _fourier_mix(x, mod, seq_len, w):
    n = x.shape[0]
    batch = n // seq_len
    n_row_blocks = seq_len // ROW_BLOCK
    mod_spec = pl.BlockSpec((None, 1, N_MOD * D_MODEL),
                            (lambda b, r: (b, 0, 0)) if mod.shape[0] > 1 else (lambda b, r: (0, 0, 0)))
    common = dict(
        grid=(batch, n_row_blocks),
        out_specs=pl.BlockSpec((ROW_BLOCK, D_MODEL), lambda b, r: (b * n_row_blocks + r, 0)),
        out_shape=jax.ShapeDtypeStruct((n, D_MODEL), F32),
        compiler_params=_cparams(2),
        name="fourier_mix",
    )
    x_spec = pl.BlockSpec((seq_len, D_MODEL), lambda b, r: (b, 0))
    csc_spec = pl.BlockSpec((FOURIER_GROUP, 2 * FOURIER_GROUP), lambda b, r: (0, 0))
    w_spec = pl.BlockSpec((D_MODEL, D_MODEL), lambda b, r: (0, 0))
    t = np.arange(seq_len)
    if n_row_blocks == 1:
        n_seq = FNET_SEQS if (mod.shape[0] == 1 and batch % FNET_SEQS == 0) else 1
        rows = pl.BlockSpec((n_seq * seq_len, D_MODEL), lambda b, r: (b, 0))
        common.update(grid=(batch // n_seq, 1), out_specs=rows)
        return pl.pallas_call(
            functools.partial(_fnet_body, seq_len, n_seq),
            in_specs=[rows, mod_spec, csc_spec, pl.BlockSpec((seq_len, 2 * seq_len), lambda b, r: (0, 0)), w_spec],
            **common,
        )(x, mod, _channel_table(), _position_table(t, t, seq_len), w)
    half_block = ROW_BLOCK // 2
    u = np.arange(seq_len // 2)
    i = np.arange(half_block)
    perm = np.zeros((ROW_BLOCK, ROW_BLOCK), np.float32)
    perm[2 * i, i] = 1.0
    perm[2 * i + 1, half_block + i] = 1.0
    step_rows = FNET_STEP_BLOCKS * ROW_BLOCK
    steps = seq_len // step_rows
    table_spec = pl.BlockSpec((step_rows // 2, seq_len), lambda b, r: (r, 0))
    common.update(grid=(batch, steps),
                  out_specs=pl.BlockSpec((step_rows, D_MODEL), lambda b, r: (b * steps + r, 0)))
    return pl.pallas_call(
        functools.partial(_fnet_split_body, seq_len),
        in_specs=[x_spec, mod_spec, csc_spec, table_spec, table_spec,
                  pl.BlockSpec((ROW_BLOCK, ROW_BLOCK), lambda b, r: (0, 0)), w_spec],
        scratch_shapes=[pltpu.VMEM((seq_len, D_MODEL), BF16)] * 2,
        **common,
    )(x, mod, _channel_table(), _position_table(2 * u, u, seq_len), _position_table(2 * u + 1, u, seq_len),
      jnp.asarray(perm, BF16), w)


def _pool_bands():
    i = np.arange(ROW_BLOCK)[:, None]
    c = np.arange(2 * ROW_BLOCK)[None, :] - POOL_HALO
    bands = [((c >= i - w // 2) & (c < i - w // 2 + w)) for w in POOL_WINDOWS]
    return jnp.asarray(np.stack(bands), BF16)


def _pool_body(seq_len, n_seq, x_ref, mod_ref, band_ref, w_ref, ps_ref, o_ref, h_ref, hb_ref):
    mod = mod_ref[...]
    n_blocks = seq_len // ROW_BLOCK
    padded = seq_len + 2 * POOL_HALO
    zeros = jnp.zeros((POOL_HALO, D_MODEL), BF16)
    for s in range(n_seq):
        hb_ref[s * padded:s * padded + POOL_HALO, :] = zeros
        hb_ref[(s + 1) * padded - POOL_HALO:(s + 1) * padded, :] = zeros
        for j in range(n_blocks):
            rows = slice(s * seq_len + j * ROW_BLOCK, s * seq_len + (j + 1) * ROW_BLOCK)
            h = _modulate(x_ref[rows, :], mod, 0)
            h_ref[rows, :] = h
            start = s * padded + POOL_HALO + j * ROW_BLOCK
            hb_ref[start:start + ROW_BLOCK, :] = h.astype(BF16)

    gate = _gate(mod, 0)
    for s in range(n_seq):
        for j in range(n_blocks):
            rows = slice(s * seq_len + j * ROW_BLOCK, s * seq_len + (j + 1) * ROW_BLOCK)
            slab = slice(s * padded + j * ROW_BLOCK, s * padded + (j + 2) * ROW_BLOCK)
            t = j * ROW_BLOCK + lax.broadcasted_iota(jnp.int32, (ROW_BLOCK, POOL_GROUP), 0)
            for g, w in enumerate(POOL_WINDOWS):
                cols = slice(g * POOL_GROUP, (g + 1) * POOL_GROUP)
                total = _dot(band_ref[g], hb_ref[slab, cols])
                cnt = jnp.minimum(t - w // 2 + w, seq_len) - jnp.maximum(t - w // 2, 0)
                pooled = total / cnt.astype(F32) - h_ref[rows, cols]
                y = _dot(pooled.astype(BF16), w_ref[g]) * ps_ref[:, cols]
                o_ref[rows, cols] = x_ref[rows, cols] + gate[:, cols] * y


def _pool_mix(x, mod, seq_len, w_pool, pool_scale):
    n = x.shape[0]
    batch = n // seq_len
    assert POOL_HALO >= max(POOL_WINDOWS) // 2 and 2 * POOL_HALO == ROW_BLOCK
    n_seq = POOL_SEQS if (seq_len == ROW_BLOCK and mod.shape[0] == 1 and batch % POOL_SEQS == 0) else 1
    rows = n_seq * seq_len
    return pl.pallas_call(
        functools.partial(_pool_body, seq_len, n_seq),
        grid=(batch // n_seq,),
        in_specs=[
            pl.BlockSpec((rows, D_MODEL), lambda b: (b, 0)),
            pl.BlockSpec((None, 1, N_MOD * D_MODEL),
                         (lambda b: (b, 0, 0)) if mod.shape[0] > 1 else (lambda b: (0, 0, 0))),
            pl.BlockSpec((len(POOL_WINDOWS), ROW_BLOCK, 2 * ROW_BLOCK), lambda b: (0, 0, 0)),
            pl.BlockSpec((len(POOL_WINDOWS), POOL_GROUP, POOL_GROUP), lambda b: (0, 0, 0)),
            pl.BlockSpec((1, D_MODEL), lambda b: (0, 0)),
        ],
        out_specs=pl.BlockSpec((rows, D_MODEL), lambda b: (b, 0)),
        out_shape=jax.ShapeDtypeStruct((n, D_MODEL), F32),
        scratch_shapes=[pltpu.VMEM((rows, D_MODEL), F32),
                        pltpu.VMEM((n_seq * (seq_len + 2 * POOL_HALO), D_MODEL), BF16)],
        compiler_params=_cparams(1),
        name="pool_mix",
    )(x, mod, _pool_bands(), w_pool, pool_scale)


def _rope_pair(a, cs):
    t = a * cs
    return t + pltpu.roll(t, QK_ROPE, 1)


def _store_keys(k_ref, k_nope, k_rope):
    for hd in range(N_HEADS):
        k_ref[:, hd * QK_TILE:hd * QK_TILE + QK_NOPE] = k_nope[:, hd * QK_NOPE:(hd + 1) * QK_NOPE]
        k_ref[:, hd * QK_TILE + QK_NOPE:(hd + 1) * QK_TILE] = k_rope


def _mla_proj_body(rope, state, *refs):
    refs = list(refs)
    x_ref, mod_ref = refs[:2]
    del refs[:2]
    cs_ref = refs.pop(0) if rope else None
    wdq_ref, gq_ref, wuq_ref, wdkv_ref, gkv_ref, wukv_ref = refs[:6]
    del refs[:6]
    q_ref, k_ref, v_ref = refs[:3]
    del refs[:3]
    h = _modulate(x_ref[...], mod_ref[...], 0).astype(BF16)
    cq = (_rms(_dot(h, wdq_ref[...])) * gq_ref[...]).astype(BF16)
    q = _dot(cq, wuq_ref[...]) * SM_SCALE
    n_nope = N_HEADS * QK_NOPE
    for hd in range(N_HEADS):
        a = q[:, n_nope + hd * HEAD_LANES:n_nope + (hd + 1) * HEAD_LANES]
        if rope:
            a = _rope_pair(a, cs_ref[...])
        q_ref[:, hd * QK_TILE:hd * QK_TILE + QK_NOPE] = q[:, hd * QK_NOPE:(hd + 1) * QK_NOPE].astype(BF16)
        q_ref[:, hd * QK_TILE + QK_NOPE:(hd + 1) * QK_TILE] = a.astype(BF16)
    ckv = _dot(h, wdkv_ref[...])
    c = _rms(ckv[:, :KV_RANK]) * gkv_ref[...]
    kp = ckv[:, KV_RANK:]
    if state:
        ckv_out_ref, kpe_out_ref = refs
        ckv_out_ref[...] = c
        kpe_out_ref[...] = kp[:, :QK_ROPE]
    if rope:
        kp = _rope_pair(kp, cs_ref[...])
    lane = lax.broadcasted_iota(jnp.int32, kp.shape, 1)
    kr = jnp.where(lane < QK_ROPE, kp, 0.0).astype(BF16)
    kv = _dot(c.astype(BF16), wukv_ref[...])
    _store_keys(k_ref, kv[:, :n_nope].astype(BF16), kr)
    v_ref[...] = kv[:, n_nope:].astype(BF16)


def _mla_project(x, mod, seq_len, prm, cs=None, state=False):
    n = x.shape[0]
    rope = cs is not None
    tile = min(MLA_TILE, seq_len) if (rope or mod.shape[0] > 1) else MLA_TILE
    tiles_per_seq = max(seq_len // tile, 1)
    wide = N_HEADS * QK_NOPE

    def const(shape):
        return pl.BlockSpec(shape, lambda i: (0,) * len(shape), pipeline_mode=pl.Buffered(1))

    def rows(width):
        return pl.BlockSpec((tile, width), lambda i: (i, 0))

    in_specs = [rows(D_MODEL),
                pl.BlockSpec((None, 1, N_MOD * D_MODEL),
                             (lambda i: (i // tiles_per_seq, 0, 0)) if mod.shape[0] > 1
                             else (lambda i: (0, 0, 0)))]
    args = [x, mod]
    if rope:
        in_specs.append(pl.BlockSpec((tile, HEAD_LANES), lambda i: (i % tiles_per_seq, 0)))
        args.append(cs)
    for name in ("w_dq", "g_q", "w_uq", "w_dkv", "g_kv", "w_ukv"):
        in_specs.append(const(prm[name].shape))
        args.append(prm[name])
    out_specs = [rows(N_HEADS * QK_TILE), rows(N_HEADS * QK_TILE), rows(wide)]
    out_shape = [jax.ShapeDtypeStruct((n, N_HEADS * QK_TILE), BF16),
                 jax.ShapeDtypeStruct((n, N_HEADS * QK_TILE), BF16),
                 jax.ShapeDtypeStruct((n, wide), BF16)]
    if state:
        out_specs += [rows(KV_RANK), rows(QK_ROPE)]
        out_shape += [jax.ShapeDtypeStruct((n, KV_RANK), F32), jax.ShapeDtypeStruct((n, QK_ROPE), F32)]
    return pl.pallas_call(
        functools.partial(_mla_proj_body, rope, state),
        grid=(n // tile,),
        in_specs=in_specs,
        out_specs=out_specs,
        out_shape=out_shape,
        compiler_params=_cparams(1),
        name="mla_project",
    )(*args)


def _ctx_expand_body(c_ref, kr_ref, w_ref, k_ref, v_ref):
    kv = _dot(c_ref[...].astype(BF16), w_ref[...])
    n_nope = N_HEADS * QK_NOPE
    _store_keys(k_ref, kv[:, :n_nope].astype(BF16), kr_ref[...])
    v_ref[...] = kv[:, n_nope:].astype(BF16)


def _ctx_expand(c_kv, k_rope, w_ukv):
    n = c_kv.shape[0]
    wide = N_HEADS * QK_NOPE
    return pl.pallas_call(
        _ctx_expand_body,
        grid=(n // ROW_BLOCK,),
        in_specs=[pl.BlockSpec((ROW_BLOCK, KV_RANK), lambda i: (i, 0)),
                  pl.BlockSpec((ROW_BLOCK, HEAD_LANES), lambda i: (i, 0)),
                  pl.BlockSpec(w_ukv.shape, lambda i: (0, 0))],
        out_specs=[pl.BlockSpec((ROW_BLOCK, N_HEADS * QK_TILE), lambda i: (i, 0)),
                   pl.BlockSpec((ROW_BLOCK, wide), lambda i: (i, 0))],
        out_shape=[jax.ShapeDtypeStruct((n, N_HEADS * QK_TILE), BF16), jax.ShapeDtypeStruct((n, wide), BF16)],
        compiler_params=_cparams(1),
        name="mla_ctx_expand",
    )(c_kv, k_rope, w_ukv)


def _attn_body(has_ctx, n_seq, *refs):
    refs = list(refs)
    q_ref, k_ref, v_ref = refs[:3]
    del refs[:3]
    if has_ctx:
        kc_ref, vc_ref = refs[:2]
        del refs[:2]
    x_ref, mod_ref, wo_ref, o_ref, heads_ref = refs
    n_q = q_ref.shape[0] // n_seq
    n_k = k_ref.shape[0] // n_seq
    for seq in range(n_seq):
        rows = slice(seq * n_q, (seq + 1) * n_q)
        keys = slice(seq * n_k, (seq + 1) * n_k)
        for hd in range(N_HEADS):
            qk = slice(hd * QK_TILE, (hd + 1) * QK_TILE)
            vcols = slice(hd * V_HEAD, (hd + 1) * V_HEAD)
            q = q_ref[rows, qk]
            s = _dot_nt(q, k_ref[keys, qk])
            m = jnp.max(s, axis=-1, keepdims=True)
            if has_ctx:
                sc = _dot_nt(q, kc_ref[:, qk])
                m = jnp.maximum(m, jnp.max(sc, axis=-1, keepdims=True))
            p = jnp.exp(s - m)
            l = jnp.sum(p, axis=-1, keepdims=True)
            o = _dot(p.astype(BF16), v_ref[keys, vcols])
            if has_ctx:
                pc = jnp.exp(sc - m)
                l = l + jnp.sum(pc, axis=-1, keepdims=True)
                o = o + _dot(pc.astype(BF16), vc_ref[:, vcols])
            heads_ref[rows, vcols] = (o / l).astype(BF16)
    y = _dot(heads_ref[...], wo_ref[...])
    o_ref[...] = x_ref[...] + _gate(mod_ref[...], 0) * y


def _mla_attend(x, mod, seq_len, q, k, v, w_o, ctx=None):
    n = x.shape[0]
    tq = min(ATT_Q_TILE, seq_len)
    q_tiles = seq_len // tq
    wide = N_HEADS * QK_NOPE
    has_ctx = ctx is not None
    batch = n // seq_len
    grouped = q_tiles == 1 and not has_ctx and mod.shape[0] == 1 and batch % ATT_SEQS == 0
    n_seq = ATT_SEQS if grouped else 1

    def q_rows(width):
        return pl.BlockSpec((n_seq * tq, width), lambda b, r: (b * q_tiles + r, 0))

    def seq_rows(length, width):
        mode = pl.Buffered(1) if q_tiles > 1 else None
        return pl.BlockSpec((n_seq * length, width), lambda b, r: (b, 0), pipeline_mode=mode)

    in_specs = [q_rows(N_HEADS * QK_TILE), seq_rows(seq_len, N_HEADS * QK_TILE), seq_rows(seq_len, wide)]
    args = [q, k, v]
    if has_ctx:
        kc, vc = ctx
        past = kc.shape[0] // (n // seq_len)
        in_specs += [seq_rows(past, N_HEADS * QK_TILE), seq_rows(past, wide)]
        args += [kc, vc]
    in_specs += [q_rows(D_MODEL),
                 pl.BlockSpec((None, 1, N_MOD * D_MODEL),
                              (lambda b, r: (b, 0, 0)) if mod.shape[0] > 1 else (lambda b, r: (0, 0, 0))),
                 pl.BlockSpec(w_o.shape, lambda b, r: (0, 0))]
    args += [x, mod, w_o]
    return pl.pallas_call(
        functools.partial(_attn_body, has_ctx, n_seq),
        grid=(batch // n_seq, q_tiles),
        in_specs=in_specs,
        out_specs=q_rows(D_MODEL),
        out_shape=jax.ShapeDtypeStruct((n, D_MODEL), F32),
        scratch_shapes=[pltpu.VMEM((n_seq * tq, wide), BF16)],
        compiler_params=_cparams(2, ATT_VMEM_LIMIT_BYTES),
        name="mla_attend",
    )(*args)


def _rope_table(n_tok):
    rows = n_tok // GRID_W
    row = np.repeat(np.arange(rows), GRID_W).astype(np.float32)
    col = np.tile(np.arange(GRID_W), rows).astype(np.float32)
    inv = (1.0 / (ROPE_BASE ** (np.arange(0, ROPE_AXIS, 2, dtype=np.float32) / ROPE_AXIS))).astype(np.float32)
    ang_r = row[:, None] * inv
    ang_c = col[:, None] * inv
    cos = np.concatenate([np.cos(ang_r), np.cos(ang_r), np.cos(ang_c), np.cos(ang_c)], axis=1)
    sin = np.concatenate([np.sin(ang_r), np.sin(ang_r), np.sin(ang_c), np.sin(ang_c)], axis=1)
    return jnp.asarray(np.concatenate([cos, sin], axis=1), F32)


def _swap_cols(w):
    half = ROPE_AXIS // 2
    return jnp.concatenate([-w[..., half:2 * half], w[..., 0:half],
                            -w[..., 3 * half:4 * half], w[..., 2 * half:3 * half]], axis=-1)


def _mla_weights(w_dq, g_q, w_uq, w_dkv, g_kv, w_ukv, w_o):
    wq = w_uq.reshape(Q_RANK, N_HEADS, QK_NOPE + QK_ROPE)
    wq_pe = wq[:, :, QK_NOPE:]
    wq_pairs = jnp.concatenate([wq_pe, _swap_cols(wq_pe)], axis=-1)
    w_uq2 = jnp.concatenate([wq[:, :, :QK_NOPE].reshape(Q_RANK, -1), wq_pairs.reshape(Q_RANK, -1)], axis=1)
    w_pe = w_dkv[:, KV_RANK:]
    w_dkv2 = jnp.concatenate([w_dkv[:, :KV_RANK], w_pe, _swap_cols(w_pe)], axis=1)
    wkv = w_ukv.reshape(KV_RANK, N_HEADS, QK_NOPE + V_HEAD)
    w_ukv2 = jnp.concatenate([wkv[:, :, :QK_NOPE].reshape(KV_RANK, -1),
                              wkv[:, :, QK_NOPE:].reshape(KV_RANK, -1)], axis=1)
    return {"w_dq": w_dq.astype(BF16), "g_q": g_q.reshape(1, Q_RANK), "w_uq": w_uq2.astype(BF16),
            "w_dkv": w_dkv2.astype(BF16), "g_kv": g_kv.reshape(1, KV_RANK), "w_ukv": w_ukv2.astype(BF16),
            "w_o": w_o.astype(BF16)}


def _cast_body(a_ref, b_ref, ao_ref, bo_ref):
    ao_ref[...] = a_ref[...].astype(BF16)
    bo_ref[...] = b_ref[...].astype(BF16)


def _ffn_first_weights(w_in, w_out):
    steps = D_FF // ROW_BLOCK
    cols_in = 2 * D_FF // steps
    return pl.pallas_call(
        _cast_body,
        grid=(steps,),
        in_specs=[pl.BlockSpec((None, D_MODEL, cols_in), lambda c: (0, 0, c)),
                  pl.BlockSpec((None, ROW_BLOCK, D_MODEL), lambda c: (0, c, 0))],
        out_specs=[pl.BlockSpec((D_MODEL, cols_in), lambda c: (0, c)),
                   pl.BlockSpec((ROW_BLOCK, D_MODEL), lambda c: (c, 0))],
        out_shape=[jax.ShapeDtypeStruct(w_in.shape[1:], BF16), jax.ShapeDtypeStruct(w_out.shape[1:], BF16)],
        compiler_params=_cparams(1),
        name="ffn_weight_cast",
    )(w_in, w_out)


def kernel(x_prompt, x_sample, cache_ckv, cache_kpe, c, c_ctx, w_ada, b_ada, w_fnet, w_pool, pool_scale,
           w_dq, g_q, w_uq, w_dkv, g_kv, w_ukv, w_o, w_ffn_in, w_ffn_conv, b_ffn_conv, w_ffn_out, g_final):
    batch, seq, _ = x_prompt.shape
    dec_batch, dec_seq, _ = x_sample.shape
    past = cache_ckv.shape[2]

    cond = jnp.concatenate([c_ctx[None, :], c, jnp.zeros((SUBLANES - 1 - dec_batch, D_MODEL), F32)], axis=0)
    b_ada = b_ada.reshape(DEPTH, 1, N_MOD * D_MODEL)
    mods = {0: _ada_first_layer(cond, w_ada, b_ada)}
    ada_layers_per_call = (batch * seq // FFN_TILE) // (N_MOD * D_MODEL // ADA_SHARE_COLS)
    w_in_bf, w_out_bf = _ffn_first_weights(w_ffn_in, w_ffn_out)
    b_conv = b_ffn_conv[:, None, :]

    xp = x_prompt.reshape(batch * seq, D_MODEL)
    xs = x_sample.reshape(dec_batch * dec_seq, D_MODEL)
    states = []
    j_f = j_p = j_a = 0
    for i in range(DEPTH):
        mod_p = mods[i][0:1].reshape(1, 1, N_MOD * D_MODEL)
        mod_s = mods[i][1:1 + dec_batch].reshape(dec_batch, 1, N_MOD * D_MODEL)
        kind = i % N_MIXERS
        if kind == 0:
            w = w_fnet[j_f].astype(BF16)
            xp = _fourier_mix(xp, mod_p, seq, w)
            xs = _fourier_mix(xs, mod_s, dec_seq, w)
            j_f += 1
        elif kind == 1:
            w = w_pool[j_p].astype(BF16)
            ps = pool_scale[j_p].reshape(1, D_MODEL)
            xp = _pool_mix(xp, mod_p, seq, w, ps)
            xs = _pool_mix(xs, mod_s, dec_seq, w, ps)
            j_p += 1
        else:
            prm = _mla_weights(w_dq[j_a], g_q[j_a], w_uq[j_a], w_dkv[j_a], g_kv[j_a], w_ukv[j_a], w_o[j_a])
            q, k, v, ckv_p, kpe_p = _mla_project(xp, mod_p, seq, prm, state=True)
            xp = _mla_attend(xp, mod_p, seq, q, k, v, prm["w_o"])
            states.append((ckv_p.reshape(batch, seq, KV_RANK), kpe_p.reshape(batch, seq, QK_ROPE)))
            q, k, v = _mla_project(xs, mod_s, dec_seq, prm, cs=_rope_table(dec_seq))
            kpe_c = cache_kpe[:, j_a].reshape(dec_batch * past, QK_ROPE)
            krc = jnp.concatenate([kpe_c, jnp.zeros_like(kpe_c)], axis=1).astype(BF16)
            kc, vc = _ctx_expand(cache_ckv[:, j_a].reshape(dec_batch * past, KV_RANK), krc, prm["w_ukv"])
            xs = _mla_attend(xs, mod_s, dec_seq, q, k, v, prm["w_o"], ctx=(kc, vc))
            j_a += 1
        g_fin = g_final.reshape(1, D_MODEL) if i == DEPTH - 1 else None
        ffn_w = (w_in_bf, w_ffn_conv, b_conv, w_out_bf)
        if i + 1 < DEPTH:
            todo = [l for l in range(i + 1, DEPTH) if l not in mods][:ada_layers_per_call]
            assert i + 1 in mods or (todo and todo[0] == i + 1)
            ada_later = (cond, w_ada, b_ada, tuple(todo)) if todo else None
            out = _conv_ffn(xp, mod_p, seq, i, *ffn_w, cast_next=(w_ffn_in, w_ffn_out), ada_later=ada_later)
            xp, w_in_bf, w_out_bf = out[:3]
            for n_done, l in enumerate(todo):
                mods[l] = out[3][n_done]
        else:
            xp, = _conv_ffn(xp, mod_p, seq, i, *ffn_w, g_final=g_fin)
        xs, = _conv_ffn(xs, mod_s, dec_seq, i, *ffn_w, g_final=g_fin)

    state_ckv = jnp.stack([s[0] for s in states], axis=1)
    state_kpe = jnp.stack([s[1] for s in states], axis=1)
    return (xp.reshape(batch, seq, D_MODEL), xs.reshape(dec_batch, dec_seq, D_MODEL), state_ckv, state_kpe)
```

```python
import functools

import numpy as np
import jax
import jax.numpy as jnp
from jax import lax
from jax.experimental import pallas as pl
from jax.experimental.pallas import tpu as pltpu

F32 = jnp.float32
BF16 = jnp.bfloat16

D_MODEL = 1024
DEPTH = 4
GRID_W = 64
N_MIXERS = 3
N_FOURIER_GROUPS = 4
FOURIER_GROUP = D_MODEL // N_FOURIER_GROUPS
POOL_WINDOWS = (2, 4, 8, 16)
POOL_GROUP = D_MODEL // len(POOL_WINDOWS)
N_HEADS = 8
QK_NOPE = 128
QK_ROPE = 64
V_HEAD = 128
Q_RANK = 768
KV_RANK = 512
ROPE_AXIS = QK_ROPE // 2
ROPE_BASE = 10000.0
SM_SCALE = (QK_NOPE + QK_ROPE) ** -0.5
D_FF = 2816
N_MOD = 6
EPS = 1e-6

SUBLANES = 8
LANES = 128
BF16_SUBLANES = 16
VMEM_LIMIT_BYTES = 56 * 1024 * 1024
ATT_VMEM_LIMIT_BYTES = 58 * 1024 * 1024

ROW_BLOCK = 256
FFN_CHUNK = 512
FFN_TILE = 512
MLA_TILE = 1024
ATT_Q_TILE = 512
ADA_COLS = 2048
ADA_SHARE_COLS = 1024
POOL_HALO = 128
POOL_SEQS = 4
FNET_SEQS = 4
ATT_SEQS = 4
FNET_STEP_BLOCKS = 2
HEAD_LANES = 2 * QK_ROPE
QK_TILE = QK_NOPE + HEAD_LANES


def _cparams(n_axes, vmem_limit_bytes=VMEM_LIMIT_BYTES):
    return pltpu.CompilerParams(dimension_semantics=("arbitrary",) * n_axes,
                                vmem_limit_bytes=vmem_limit_bytes)


def _rms(x):
    return x * lax.rsqrt(jnp.mean(x * x, axis=-1, keepdims=True) + EPS)


def _modulate(x, mod, sub):
    shift = mod[:, (3 * sub) * D_MODEL:(3 * sub + 1) * D_MODEL]
    scale = mod[:, (3 * sub + 1) * D_MODEL:(3 * sub + 2) * D_MODEL]
    return _rms(x) * (1.0 + scale) + shift


def _gate(mod, sub):
    return mod[:, (3 * sub + 2) * D_MODEL:(3 * sub + 3) * D_MODEL]


def _dot(a, b):
    return jnp.dot(a, b, preferred_element_type=F32)


def _dot_nt(a, b):
    return lax.dot_general(a, b, (((1,), (1,)), ((), ())), preferred_element_type=F32)


def _ada_body(cond_ref, w_ref, b_ref, o_ref):
    a = jax.nn.silu(cond_ref[...]).astype(BF16)
    o_ref[...] = _dot(a, w_ref[...].astype(BF16)) + b_ref[...]


def _ada_first_layer(cond, w_ada, b_ada):
    n_out = N_MOD * D_MODEL
    return pl.pallas_call(
        _ada_body,
        grid=(n_out // ADA_COLS,),
        in_specs=[
            pl.BlockSpec((SUBLANES, D_MODEL), lambda j: (0, 0)),
            pl.BlockSpec((None, D_MODEL, ADA_COLS), lambda j: (0, 0, j)),
            pl.BlockSpec((None, 1, ADA_COLS), lambda j: (0, 0, j)),
        ],
        out_specs=pl.BlockSpec((SUBLANES, ADA_COLS), lambda j: (0, j)),
        out_shape=jax.ShapeDtypeStruct((SUBLANES, n_out), F32),
        compiler_params=_cparams(1),
        name="ada_mod",
    )(cond, w_ada, b_ada)


def _ffn_chunks():
    edges = list(range(0, D_FF, FFN_CHUNK)) + [D_FF]
    return list(zip(edges[:-1], edges[1:]))


def _ffn_body(seq_len, n_seg, seg, final, cast_next, ada_later, *refs):
    refs = list(refs)
    x_ref, xp_ref, xn_ref, mod_ref, win_ref, wc_ref, bc_ref, wout_ref = refs[:8]
    del refs[:8]
    gfin_ref = refs.pop(0) if final else None
    side_in = [refs.pop(0) for _ in range(2 * cast_next + 3 * ada_later)]
    o_ref = refs.pop(0)
    side_out = [refs.pop(0) for _ in range(2 * cast_next + ada_later)]
    if cast_next:
        side_out[0][...] = side_in[0][...].astype(BF16)
        side_out[1][...] = side_in[1][...].astype(BF16)
    if ada_later:
        _ada_body(*side_in[-3:], side_out[-1])
    hext_ref, uga_ref, uva_ref, ugb_ref, uvb_ref, acta_ref, actb_ref, acc_ref = refs
    tile = n_seg * seg
    gap = BF16_SUBLANES
    i = pl.program_id(0)
    mod = mod_ref[...]
    x = x_ref[...]

    start = i * tile
    prev_ok = (start % seq_len != 0).astype(F32)
    next_ok = ((start + tile) % seq_len != 0).astype(F32)
    zeros8 = jnp.zeros((SUBLANES, D_MODEL), F32)
    h_prev = _modulate(xp_ref[...], mod, 1) * prev_ok
    h_next = _modulate(xn_ref[...], mod, 1) * next_ok
    hext_ref[0:gap, :] = jnp.concatenate([zeros8, h_prev], axis=0).astype(BF16)
    h = _modulate(x, mod, 1).astype(BF16)
    for k in range(n_seg):
        base = gap + k * (seg + gap)
        hext_ref[base:base + seg, :] = h[k * seg:(k + 1) * seg, :]
        if k == n_seg - 1:
            tail = jnp.concatenate([h_next, zeros8], axis=0).astype(BF16)
        else:
            tail = jnp.zeros((gap, D_MODEL), BF16)
        hext_ref[base + seg:base + seg + gap, :] = tail
    acc_ref[...] = jnp.zeros_like(acc_ref)

    def up(cols, ug_ref, uv_ref):
        lo, hi = cols
        ug_ref[:, :hi - lo] = _dot(hext_ref[...], win_ref[:, lo:hi])
        uv_ref[:, :hi - lo] = _dot(hext_ref[...], win_ref[:, D_FF + lo:D_FF + hi])

    def conv(u_ref, b, lo, hi):
        wc = wc_ref[:, lo:hi]
        return (u_ref[b - 1:b - 1 + seg, :hi - lo] * wc[0:1, :] + u_ref[b:b + seg, :hi - lo] * wc[1:2, :]
                + u_ref[b + 1:b + 1 + seg, :hi - lo] * wc[2:3, :] + bc_ref[:, lo:hi])

    def down(cols, ug_ref, uv_ref, act_ref):
        lo, hi = cols
        for k in range(n_seg):
            b = gap + k * (seg + gap)
            a = jax.nn.silu(conv(ug_ref, b, lo, hi)) * conv(uv_ref, b, D_FF + lo, D_FF + hi)
            act_ref[k * seg:(k + 1) * seg, :hi - lo] = a.astype(BF16)
        acc_ref[...] += _dot(act_ref[:, :hi - lo], wout_ref[lo:hi, :])

    chunks = _ffn_chunks()
    u_bufs = ((uga_ref, uva_ref), (ugb_ref, uvb_ref))
    act_bufs = (acta_ref, actb_ref)
    up(chunks[0], *u_bufs[0])
    for c, cols in enumerate(chunks):
        if c + 1 < len(chunks):
            up(chunks[c + 1], *u_bufs[(c + 1) % 2])
        down(cols, *u_bufs[c % 2], act_bufs[c % 2])
    y = x + _gate(mod, 1) * acc_ref[...]
    if final:
        y = _rms(y) * gfin_ref[...]
    o_ref[...] = y


def _conv_ffn(x, mod, seq_len, layer, w_in, w_conv, b_conv, w_out, g_final=None, cast_next=None,
              ada_later=None):
    n = x.shape[0]
    tile = FFN_TILE
    seg = min(seq_len, tile)
    n_seg = tile // seg
    rows = BF16_SUBLANES + n_seg * (seg + BF16_SUBLANES)
    per_seq_mod = mod.shape[0] > 1
    final = g_final is not None
    blocks8 = n // SUBLANES
    t8 = tile // SUBLANES

    def resident(a):
        return pl.BlockSpec(a.shape, lambda i: (0,) * a.ndim, pipeline_mode=pl.Buffered(1))

    def resident_layer(a):
        shape = a.shape[1:]
        return pl.BlockSpec((None,) + shape, lambda i: (layer,) + (0,) * len(shape),
                            pipeline_mode=pl.Buffered(1))

    in_specs = [
        pl.BlockSpec((tile, D_MODEL), lambda i: (i, 0)),
        pl.BlockSpec((SUBLANES, D_MODEL), lambda i: (jnp.maximum(i * t8 - 1, 0), 0)),
        pl.BlockSpec((SUBLANES, D_MODEL), lambda i: (jnp.minimum((i + 1) * t8, blocks8 - 1), 0)),
        pl.BlockSpec((None, 1, N_MOD * D_MODEL),
                     (lambda i: ((i * tile) // seq_len, 0, 0)) if per_seq_mod else (lambda i: (0, 0, 0))),
        resident(w_in), resident_layer(w_conv), resident_layer(b_conv), resident(w_out),
    ]
    args = [x, x, x, mod, w_in, w_conv, b_conv, w_out]
    if final:
        in_specs.append(pl.BlockSpec((1, D_MODEL), lambda i: (0, 0)))
        args.append(g_final)
    out_specs = [pl.BlockSpec((tile, D_MODEL), lambda i: (i, 0))]
    out_shape = [jax.ShapeDtypeStruct((n, D_MODEL), F32)]
    if cast_next is not None:
        steps = n // tile
        for a in cast_next:
            rows_step = a.shape[1] // steps
            assert rows_step * steps == a.shape[1] and rows_step % BF16_SUBLANES == 0
            in_specs.append(pl.BlockSpec((None, rows_step, a.shape[2]), lambda i: (layer + 1, i, 0)))
            out_specs.append(pl.BlockSpec((rows_step, a.shape[2]), lambda i: (i, 0)))
            out_shape.append(jax.ShapeDtypeStruct(a.shape[1:], BF16))
            args.append(a)
    if ada_later is not None:
        cond, w_ada, b_ada, layers = ada_later
        blocks_per_layer = N_MOD * D_MODEL // ADA_SHARE_COLS
        n_share = len(layers) * blocks_per_layer
        assert n_share <= n // tile and list(layers) == list(range(layers[0], layers[0] + len(layers)))

        def share(i):
            s = jnp.minimum(i, n_share - 1)
            return s // blocks_per_layer, s % blocks_per_layer

        in_specs += [
            pl.BlockSpec((SUBLANES, D_MODEL), lambda i: (0, 0)),
            pl.BlockSpec((None, D_MODEL, ADA_SHARE_COLS), lambda i: (layers[0] + share(i)[0], 0, share(i)[1])),
            pl.BlockSpec((None, 1, ADA_SHARE_COLS), lambda i: (layers[0] + share(i)[0], 0, share(i)[1])),
        ]
        args += [cond, w_ada, b_ada]
        out_specs.append(pl.BlockSpec((None, SUBLANES, ADA_SHARE_COLS), lambda i: (share(i)[0], 0, share(i)[1])))
        out_shape.append(jax.ShapeDtypeStruct((len(layers), SUBLANES, N_MOD * D_MODEL), F32))
    return pl.pallas_call(
        functools.partial(_ffn_body, seq_len, n_seg, seg, final, cast_next is not None, ada_later is not None),
        grid=(n // tile,),
        in_specs=in_specs,
        out_specs=out_specs,
        out_shape=out_shape,
        scratch_shapes=[pltpu.VMEM((rows, D_MODEL), BF16)]
        + [pltpu.VMEM((rows, FFN_CHUNK), F32)] * 4
        + [pltpu.VMEM((tile, FFN_CHUNK), BF16)] * 2
        + [pltpu.VMEM((tile, D_MODEL), F32)],
        compiler_params=_cparams(1),
        name="conv_ffn",
    )(*args)


def _fnet_body(seq_len, n_seq, x_ref, mod_ref, csc_ref, csrow_ref, w_ref, o_ref):
    mod = mod_ref[...]
    x = x_ref[...]
    h = _modulate(x, mod, 0).astype(BF16)
    table = csc_ref[...].astype(BF16)
    parts = [_dot(h[:, g * FOURIER_GROUP:(g + 1) * FOURIER_GROUP], table) for g in range(N_FOURIER_GROUPS)]
    cos_part = jnp.concatenate([p[:, :FOURIER_GROUP] for p in parts], axis=1).astype(BF16)
    sin_part = jnp.concatenate([p[:, FOURIER_GROUP:] for p in parts], axis=1).astype(BF16)
    positions = csrow_ref[...].astype(BF16)
    scale = (seq_len * FOURIER_GROUP) ** -0.5
    f = []
    for s in range(n_seq):
        rows = slice(s * seq_len, (s + 1) * seq_len)
        stack = jnp.concatenate([cos_part[rows], sin_part[rows]], axis=0)
        f.append((_dot(positions, stack) * scale).astype(BF16))
    y = _dot(jnp.concatenate(f, axis=0), w_ref[...])
    o_ref[...] = x + _gate(mod, 0) * y


def _fnet_split_body(seq_len, x_ref, mod_ref, csc_ref, even_ref, odd_ref, perm_ref, w_ref, o_ref,
                     plus_ref, minus_ref):
    r = pl.program_id(1)
    mod = mod_ref[...]
    half = seq_len // 2

    @pl.when(r == 0)
    def _():
        def rows(j, carry):
            r0 = pl.multiple_of(j * ROW_BLOCK, ROW_BLOCK)
            h_lo = _modulate(x_ref[pl.ds(r0, ROW_BLOCK), :], mod, 0).astype(BF16)
            h_hi = _modulate(x_ref[pl.ds(half + r0, ROW_BLOCK), :], mod, 0).astype(BF16)
            table = csc_ref[...].astype(BF16)
            for g in range(N_FOURIER_GROUPS):
                cols = slice(g * FOURIER_GROUP, (g + 1) * FOURIER_GROUP)
                y_lo = _dot(h_lo[:, cols], table)
                y_hi = _dot(h_hi[:, cols], table)
                for ref, y in ((plus_ref, y_lo + y_hi), (minus_ref, y_lo - y_hi)):
                    ref[pl.ds(r0, ROW_BLOCK), cols] = y[:, :FOURIER_GROUP].astype(BF16)
                    ref[pl.ds(half + r0, ROW_BLOCK), cols] = y[:, FOURIER_GROUP:].astype(BF16)
            return carry
        lax.fori_loop(0, half // ROW_BLOCK, rows, 0)

    scale = (seq_len * FOURIER_GROUP) ** -0.5
    even = (_dot(even_ref[...].astype(BF16), plus_ref[...]) * scale).astype(BF16)
    odd = (_dot(odd_ref[...].astype(BF16), minus_ref[...]) * scale).astype(BF16)
    half_block = ROW_BLOCK // 2
    f = []
    for j in range(FNET_STEP_BLOCKS):
        rows = slice(j * half_block, (j + 1) * half_block)
        f.append(_dot(perm_ref[...], jnp.concatenate([even[rows], odd[rows]], axis=0)).astype(BF16))
    y = _dot(jnp.concatenate(f, axis=0), w_ref[...])
    step_rows = FNET_STEP_BLOCKS * ROW_BLOCK
    r0 = pl.multiple_of(r * step_rows, step_rows)
    o_ref[...] = x_ref[pl.ds(r0, step_rows), :] + _gate(mod, 0) * y


def _channel_table():
    c = np.arange(FOURIER_GROUP)
    ang = 2.0 * np.pi * ((c[:, None] * c[None, :]) % FOURIER_GROUP) / FOURIER_GROUP
    return jnp.asarray(np.concatenate([np.cos(ang), np.sin(ang)], axis=1), F32)


def _position_table(freqs, positions, seq_len):
    ang = 2.0 * np.pi * ((freqs[:, None] * positions[None, :]) % seq_len) / seq_len
    return jnp.asarray(np.concatenate([np.cos(ang), -np.sin(ang)], axis=1), F32)


def _fourier_mix(x, mod, seq_len, w):
    n = x.shape[0]
    batch = n // seq_len
    n_row_blocks = seq_len // ROW_BLOCK
    mod_spec = pl.BlockSpec((None, 1, N_MOD * D_MODEL),
                            (lambda b, r: (b, 0, 0)) if mod.shape[0] > 1 else (lambda b, r: (0, 0, 0)))
    common = dict(
        grid=(batch, n_row_blocks),
        out_specs=pl.BlockSpec((ROW_BLOCK, D_MODEL), lambda b, r: (b * n_row_blocks + r, 0)),
        out_shape=jax.ShapeDtypeStruct((n, D_MODEL), F32),
        compiler_params=_cparams(2),
        name="fourier_mix",
    )
    x_spec = pl.BlockSpec((seq_len, D_MODEL), lambda b, r: (b, 0))
    csc_spec = pl.BlockSpec((FOURIER_GROUP, 2 * FOURIER_GROUP), lambda b, r: (0, 0))
    w_spec = pl.BlockSpec((D_MODEL, D_MODEL), lambda b, r: (0, 0))
    t = np.arange(seq_len)
    if n_row_blocks == 1:
        n_seq = FNET_SEQS if (mod.shape[0] == 1 and batch % FNET_SEQS == 0) else 1
        rows = pl.BlockSpec((n_seq * seq_len, D_MODEL), lambda b, r: (b, 0))
        common.update(grid=(batch // n_seq, 1), out_specs=rows)
        return pl.pallas_call(
            functools.partial(_fnet_body, seq_len, n_seq),
            in_specs=[rows, mod_spec, csc_spec, pl.BlockSpec((seq_len, 2 * seq_len), lambda b, r: (0, 0)), w_spec],
            **common,
        )(x, mod, _channel_table(), _position_table(t, t, seq_len), w)
    half_block = ROW_BLOCK // 2
    u = np.arange(seq_len // 2)
    i = np.arange(half_block)
    perm = np.zeros((ROW_BLOCK, ROW_BLOCK), np.float32)
    perm[2 * i, i] = 1.0
    perm[2 * i + 1, half_block + i] = 1.0
    step_rows = FNET_STEP_BLOCKS * ROW_BLOCK
    steps = seq_len // step_rows
    table_spec = pl.BlockSpec((step_rows // 2, seq_len), lambda b, r: (r, 0))
    common.update(grid=(batch, steps),
                  out_specs=pl.BlockSpec((step_rows, D_MODEL), lambda b, r: (b * steps + r, 0)))
    return pl.pallas_call(
        functools.partial(_fnet_split_body, seq_len),
        in_specs=[x_spec, mod_spec, csc_spec, table_spec, table_spec,
                  pl.BlockSpec((ROW_BLOCK, ROW_BLOCK), lambda b, r: (0, 0)), w_spec],
        scratch_shapes=[pltpu.VMEM((seq_len, D_MODEL), BF16)] * 2,
        **common,
    )(x, mod, _channel_table(), _position_table(2 * u, u, seq_len), _position_table(2 * u + 1, u, seq_len),
      jnp.asarray(perm, BF16), w)


def _pool_bands():
    i = np.arange(ROW_BLOCK)[:, None]
    c = np.arange(2 * ROW_BLOCK)[None, :] - POOL_HALO
    bands = [((c >= i - w // 2) & (c < i - w // 2 + w)) for w in POOL_WINDOWS]
    return jnp.asarray(np.stack(bands), BF16)


def _pool_body(seq_len, n_seq, x_ref, mod_ref, band_ref, w_ref, ps_ref, o_ref, h_ref, hb_ref):
    mod = mod_ref[...]
    n_blocks = seq_len // ROW_BLOCK
    padded = seq_len + 2 * POOL_HALO
    zeros = jnp.zeros((POOL_HALO, D_MODEL), BF16)
    for s in range(n_seq):
        hb_ref[s * padded:s * padded + POOL_HALO, :] = zeros
        hb_ref[(s + 1) * padded - POOL_HALO:(s + 1) * padded, :] = zeros
        for j in range(n_blocks):
            rows = slice(s * seq_len + j * ROW_BLOCK, s * seq_len + (j + 1) * ROW_BLOCK)
            h = _modulate(x_ref[rows, :], mod, 0)
            h_ref[rows, :] = h
            start = s * padded + POOL_HALO + j * ROW_BLOCK
            hb_ref[start:start + ROW_BLOCK, :] = h.astype(BF16)

    gate = _gate(mod, 0)
    for s in range(n_seq):
        for j in range(n_blocks):
            rows = slice(s * seq_len + j * ROW_BLOCK, s * seq_len + (j + 1) * ROW_BLOCK)
            slab = slice(s * padded + j * ROW_BLOCK, s * padded + (j + 2) * ROW_BLOCK)
            t = j * ROW_BLOCK + lax.broadcasted_iota(jnp.int32, (ROW_BLOCK, POOL_GROUP), 0)
            for g, w in enumerate(POOL_WINDOWS):
                cols = slice(g * POOL_GROUP, (g + 1) * POOL_GROUP)
                total = _dot(band_ref[g], hb_ref[slab, cols])
                cnt = jnp.minimum(t - w // 2 + w, seq_len) - jnp.maximum(t - w // 2, 0)
                pooled = total / cnt.astype(F32) - h_ref[rows, cols]
                y = _dot(pooled.astype(BF16), w_ref[g]) * ps_ref[:, cols]
                o_ref[rows, cols] = x_ref[rows, cols] + gate[:, cols] * y


def _pool_mix(x, mod, seq_len, w_pool, pool_scale):
    n = x.shape[0]
    batch = n // seq_len
    assert POOL_HALO >= max(POOL_WINDOWS) // 2 and 2 * POOL_HALO == ROW_BLOCK
    n_seq = POOL_SEQS if (seq_len == ROW_BLOCK and mod.shape[0] == 1 and batch % POOL_SEQS == 0) else 1
    rows = n_seq * seq_len
    return pl.pallas_call(
        functools.partial(_pool_body, seq_len, n_seq),
        grid=(batch // n_seq,),
        in_specs=[
            pl.BlockSpec((rows, D_MODEL), lambda b: (b, 0)),
            pl.BlockSpec((None, 1, N_MOD * D_MODEL),
                         (lambda b: (b, 0, 0)) if mod.shape[0] > 1 else (lambda b: (0, 0, 0))),
            pl.BlockSpec((len(POOL_WINDOWS), ROW_BLOCK, 2 * ROW_BLOCK), lambda b: (0, 0, 0)),
            pl.BlockSpec((len(POOL_WINDOWS), POOL_GROUP, POOL_GROUP), lambda b: (0, 0, 0)),
            pl.BlockSpec((1, D_MODEL), lambda b: (0, 0)),
        ],
        out_specs=pl.BlockSpec((rows, D_MODEL), lambda b: (b, 0)),
        out_shape=jax.ShapeDtypeStruct((n, D_MODEL), F32),
        scratch_shapes=[pltpu.VMEM((rows, D_MODEL), F32),
                        pltpu.VMEM((n_seq * (seq_len + 2 * POOL_HALO), D_MODEL), BF16)],
        compiler_params=_cparams(1),
        name="pool_mix",
    )(x, mod, _pool_bands(), w_pool, pool_scale)


def _rope_pair(a, cs):
    t = a * cs
    return t + pltpu.roll(t, QK_ROPE, 1)


def _store_keys(k_ref, k_nope, k_rope):
    for hd in range(N_HEADS):
        k_ref[:, hd * QK_TILE:hd * QK_TILE + QK_NOPE] = k_nope[:, hd * QK_NOPE:(hd + 1) * QK_NOPE]
        k_ref[:, hd * QK_TILE + QK_NOPE:(hd + 1) * QK_TILE] = k_rope


def _mla_proj_body(rope, state, *refs):
    refs = list(refs)
    x_ref, mod_ref = refs[:2]
    del refs[:2]
    cs_ref = refs.pop(0) if rope else None
    wdq_ref, gq_ref, wuq_ref, wdkv_ref, gkv_ref, wukv_ref = refs[:6]
    del refs[:6]
    q_ref, k_ref, v_ref = refs[:3]
    del refs[:3]
    h = _modulate(x_ref[...], mod_ref[...], 0).astype(BF16)
    cq = (_rms(_dot(h, wdq_ref[...])) * gq_ref[...]).astype(BF16)
    q = _dot(cq, wuq_ref[...]) * SM_SCALE
    n_nope = N_HEADS * QK_NOPE
    for hd in range(N_HEADS):
        a = q[:, n_nope + hd * HEAD_LANES:n_nope + (hd + 1) * HEAD_LANES]
        if rope:
            a = _rope_pair(a, cs_ref[...])
        q_ref[:, hd * QK_TILE:hd * QK_TILE + QK_NOPE] = q[:, hd * QK_NOPE:(hd + 1) * QK_NOPE].astype(BF16)
        q_ref[:, hd * QK_TILE + QK_NOPE:(hd + 1) * QK_TILE] = a.astype(BF16)
    ckv = _dot(h, wdkv_ref[...])
    c = _rms(ckv[:, :KV_RANK]) * gkv_ref[...]
    kp = ckv[:, KV_RANK:]
    if state:
        ckv_out_ref, kpe_out_ref = refs
        ckv_out_ref[...] = c
        kpe_out_ref[...] = kp[:, :QK_ROPE]
    if rope:
        kp = _rope_pair(kp, cs_ref[...])
    lane = lax.broadcasted_iota(jnp.int32, kp.shape, 1)
    kr = jnp.where(lane < QK_ROPE, kp, 0.0).astype(BF16)
    kv = _dot(c.astype(BF16), wukv_ref[...])
    _store_keys(k_ref, kv[:, :n_nope].astype(BF16), kr)
    v_ref[...] = kv[:, n_nope:].astype(BF16)


def _mla_project(x, mod, seq_len, prm, cs=None, state=False):
    n = x.shape[0]
    rope = cs is not None
    tile = min(MLA_TILE, seq_len) if (rope or mod.shape[0] > 1) else MLA_TILE
    tiles_per_seq = max(seq_len // tile, 1)
    wide = N_HEADS * QK_NOPE

    def const(shape):
        return pl.BlockSpec(shape, lambda i: (0,) * len(shape), pipeline_mode=pl.Buffered(1))

    def rows(width):
        return pl.BlockSpec((tile, width), lambda i: (i, 0))

    in_specs = [rows(D_MODEL),
                pl.BlockSpec((None, 1, N_MOD * D_MODEL),
                             (lambda i: (i // tiles_per_seq, 0, 0)) if mod.shape[0] > 1
                             else (lambda i: (0, 0, 0)))]
    args = [x, mod]
    if rope:
        in_specs.append(pl.BlockSpec((tile, HEAD_LANES), lambda i: (i % tiles_per_seq, 0)))
        args.append(cs)
    for name in ("w_dq", "g_q", "w_uq", "w_dkv", "g_kv", "w_ukv"):
        in_specs.append(const(prm[name].shape))
        args.append(prm[name])
    out_specs = [rows(N_HEADS * QK_TILE), rows(N_HEADS * QK_TILE), rows(wide)]
    out_shape = [jax.ShapeDtypeStruct((n, N_HEADS * QK_TILE), BF16),
                 jax.ShapeDtypeStruct((n, N_HEADS * QK_TILE), BF16),
                 jax.ShapeDtypeStruct((n, wide), BF16)]
    if state:
        out_specs += [rows(KV_RANK), rows(QK_ROPE)]
        out_shape += [jax.ShapeDtypeStruct((n, KV_RANK), F32), jax.ShapeDtypeStruct((n, QK_ROPE), F32)]
    return pl.pallas_call(
        functools.partial(_mla_proj_body, rope, state),
        grid=(n // tile,),
        in_specs=in_specs,
        out_specs=out_specs,
        out_shape=out_shape,
        compiler_params=_cparams(1),
        name="mla_project",
    )(*args)


def _ctx_expand_body(c_ref, kr_ref, w_ref, k_ref, v_ref):
    kv = _dot(c_ref[...].astype(BF16), w_ref[...])
    n_nope = N_HEADS * QK_NOPE
    _store_keys(k_ref, kv[:, :n_nope].astype(BF16), kr_ref[...])
    v_ref[...] = kv[:, n_nope:].astype(BF16)


def _ctx_expand(c_kv, k_rope, w_ukv):
    n = c_kv.shape[0]
    wide = N_HEADS * QK_NOPE
    return pl.pallas_call(
        _ctx_expand_body,
        grid=(n // ROW_BLOCK,),
        in_specs=[pl.BlockSpec((ROW_BLOCK, KV_RANK), lambda i: (i, 0)),
                  pl.BlockSpec((ROW_BLOCK, HEAD_LANES), lambda i: (i, 0)),
                  pl.BlockSpec(w_ukv.shape, lambda i: (0, 0))],
        out_specs=[pl.BlockSpec((ROW_BLOCK, N_HEADS * QK_TILE), lambda i: (i, 0)),
                   pl.BlockSpec((ROW_BLOCK, wide), lambda i: (i, 0))],
        out_shape=[jax.ShapeDtypeStruct((n, N_HEADS * QK_TILE), BF16), jax.ShapeDtypeStruct((n, wide), BF16)],
        compiler_params=_cparams(1),
        name="mla_ctx_expand",
    )(c_kv, k_rope, w_ukv)


def _attn_body(has_ctx, n_seq, *refs):
    refs = list(refs)
    q_ref, k_ref, v_ref = refs[:3]
    del refs[:3]
    if has_ctx:
        kc_ref, vc_ref = refs[:2]
        del refs[:2]
    x_ref, mod_ref, wo_ref, o_ref, heads_ref = refs
    n_q = q_ref.shape[0] // n_seq
    n_k = k_ref.shape[0] // n_seq
    for seq in range(n_seq):
        rows = slice(seq * n_q, (seq + 1) * n_q)
        keys = slice(seq * n_k, (seq + 1) * n_k)
        for hd in range(N_HEADS):
            qk = slice(hd * QK_TILE, (hd + 1) * QK_TILE)
            vcols = slice(hd * V_HEAD, (hd + 1) * V_HEAD)
            q = q_ref[rows, qk]
            s = _dot_nt(q, k_ref[keys, qk])
            m = jnp.max(s, axis=-1, keepdims=True)
            if has_ctx:
                sc = _dot_nt(q, kc_ref[:, qk])
                m = jnp.maximum(m, jnp.max(sc, axis=-1, keepdims=True))
            p = jnp.exp(s - m)
            l = jnp.sum(p, axis=-1, keepdims=True)
            o = _dot(p.astype(BF16), v_ref[keys, vcols])
            if has_ctx:
                pc = jnp.exp(sc - m)
                l = l + jnp.sum(pc, axis=-1, keepdims=True)
                o = o + _dot(pc.astype(BF16), vc_ref[:, vcols])
            heads_ref[rows, vcols] = (o / l).astype(BF16)
    y = _dot(heads_ref[...], wo_ref[...])
    o_ref[...] = x_ref[...] + _gate(mod_ref[...], 0) * y


def _mla_context_body(n_seq, x_ref, mod_ref, wdq_ref, gq_ref, wuq_ref, wdkv_ref, gkv_ref, wukv_ref, wo_ref,
                      o_ref, ckv_ref, kpe_ref, q_ref, k_ref, v_ref, heads_ref):
    _mla_proj_body(False, True, x_ref, mod_ref, wdq_ref, gq_ref, wuq_ref, wdkv_ref, gkv_ref, wukv_ref,
                   q_ref, k_ref, v_ref, ckv_ref, kpe_ref)
    _attn_body(False, n_seq, q_ref, k_ref, v_ref, x_ref, mod_ref, wo_ref, o_ref, heads_ref)


def _mla_context(x, mod, seq_len, prm):
    n = x.shape[0]
    n_seq = ATT_SEQS
    rows = n_seq * seq_len
    assert mod.shape[0] == 1 and n % rows == 0
    wide = N_HEADS * QK_NOPE

    def const(a):
        return pl.BlockSpec(a.shape, lambda i: (0,) * a.ndim, pipeline_mode=pl.Buffered(1))

    def tokens(width):
        return pl.BlockSpec((rows, width), lambda i: (i, 0))

    weights = [prm[name] for name in ("w_dq", "g_q", "w_uq", "w_dkv", "g_kv", "w_ukv", "w_o")]
    return pl.pallas_call(
        functools.partial(_mla_context_body, n_seq),
        grid=(n // rows,),
        in_specs=[tokens(D_MODEL), pl.BlockSpec((None, 1, N_MOD * D_MODEL), lambda i: (0, 0, 0))]
        + [const(w) for w in weights],
        out_specs=[tokens(D_MODEL), tokens(KV_RANK), tokens(QK_ROPE)],
        out_shape=[jax.ShapeDtypeStruct((n, D_MODEL), F32), jax.ShapeDtypeStruct((n, KV_RANK), F32),
                   jax.ShapeDtypeStruct((n, QK_ROPE), F32)],
        scratch_shapes=[pltpu.VMEM((rows, N_HEADS * QK_TILE), BF16), pltpu.VMEM((rows, N_HEADS * QK_TILE), BF16),
                        pltpu.VMEM((rows, wide), BF16), pltpu.VMEM((rows, wide), BF16)],
        compiler_params=_cparams(1, ATT_VMEM_LIMIT_BYTES),
        name="mla_context",
    )(x, mod, *weights)


def _mla_attend(x, mod, seq_len, q, k, v, w_o, ctx=None):
    n = x.shape[0]
    tq = min(ATT_Q_TILE, seq_len)
    q_tiles = seq_len // tq
    wide = N_HEADS * QK_NOPE
    has_ctx = ctx is not None
    batch = n // seq_len
    grouped = q_tiles == 1 and not has_ctx and mod.shape[0] == 1 and batch % ATT_SEQS == 0
    n_seq = ATT_SEQS if grouped else 1

    def q_rows(width):
        return pl.BlockSpec((n_seq * tq, width), lambda b, r: (b * q_tiles + r, 0))

    def seq_rows(length, width):
        mode = pl.Buffered(1) if q_tiles > 1 else None
        return pl.BlockSpec((n_seq * length, width), lambda b, r: (b, 0), pipeline_mode=mode)

    in_specs = [q_rows(N_HEADS * QK_TILE), seq_rows(seq_len, N_HEADS * QK_TILE), seq_rows(seq_len, wide)]
    args = [q, k, v]
    if has_ctx:
        kc, vc = ctx
        past = kc.shape[0] // (n // seq_len)
        in_specs += [seq_rows(past, N_HEADS * QK_TILE), seq_rows(past, wide)]
        args += [kc, vc]
    in_specs += [q_rows(D_MODEL),
                 pl.BlockSpec((None, 1, N_MOD * D_MODEL),
                              (lambda b, r: (b, 0, 0)) if mod.shape[0] > 1 else (lambda b, r: (0, 0, 0))),
                 pl.BlockSpec(w_o.shape, lambda b, r: (0, 0))]
    args += [x, mod, w_o]
    return pl.pallas_call(
        functools.partial(_attn_body, has_ctx, n_seq),
        grid=(batch // n_seq, q_tiles),
        in_specs=in_specs,
        out_specs=q_rows(D_MODEL),
        out_shape=jax.ShapeDtypeStruct((n, D_MODEL), F32),
        scratch_shapes=[pltpu.VMEM((n_seq * tq, wide), BF16)],
        compiler_params=_cparams(2, ATT_VMEM_LIMIT_BYTES),
        name="mla_attend",
    )(*args)


def _rope_table(n_tok):
    rows = n_tok // GRID_W
    row = np.repeat(np.arange(rows), GRID_W).astype(np.float32)
    col = np.tile(np.arange(GRID_W), rows).astype(np.float32)
    inv = (1.0 / (ROPE_BASE ** (np.arange(0, ROPE_AXIS, 2, dtype=np.float32) / ROPE_AXIS))).astype(np.float32)
    ang_r = row[:, None] * inv
    ang_c = col[:, None] * inv
    cos = np.concatenate([np.cos(ang_r), np.cos(ang_r), np.cos(ang_c), np.cos(ang_c)], axis=1)
    sin = np.concatenate([np.sin(ang_r), np.sin(ang_r), np.sin(ang_c), np.sin(ang_c)], axis=1)
    return jnp.asarray(np.concatenate([cos, sin], axis=1), F32)


def _swap_cols(w):
    half = ROPE_AXIS // 2
    return jnp.concatenate([-w[..., half:2 * half], w[..., 0:half],
                            -w[..., 3 * half:4 * half], w[..., 2 * half:3 * half]], axis=-1)


def _mla_weights(w_dq, g_q, w_uq, w_dkv, g_kv, w_ukv, w_o):
    wq = w_uq.reshape(Q_RANK, N_HEADS, QK_NOPE + QK_ROPE)
    wq_pe = wq[:, :, QK_NOPE:]
    wq_pairs = jnp.concatenate([wq_pe, _swap_cols(wq_pe)], axis=-1)
    w_uq2 = jnp.concatenate([wq[:, :, :QK_NOPE].reshape(Q_RANK, -1), wq_pairs.reshape(Q_RANK, -1)], axis=1)
    w_pe = w_dkv[:, KV_RANK:]
    w_dkv2 = jnp.concatenate([w_dkv[:, :KV_RANK], w_pe, _swap_cols(w_pe)], axis=1)
    wkv = w_ukv.reshape(KV_RANK, N_HEADS, QK_NOPE + V_HEAD)
    w_ukv2 = jnp.concatenate([wkv[:, :, :QK_NOPE].reshape(KV_RANK, -1),
                              wkv[:, :, QK_NOPE:].reshape(KV_RANK, -1)], axis=1)
    return {"w_dq": w_dq.astype(BF16), "g_q": g_q.reshape(1, Q_RANK), "w_uq": w_uq2.astype(BF16),
            "w_dkv": w_dkv2.astype(BF16), "g_kv": g_kv.reshape(1, KV_RANK), "w_ukv": w_ukv2.astype(BF16),
            "w_o": w_o.astype(BF16)}


def _cast_body(a_ref, b_ref, ao_ref, bo_ref):
    ao_ref[...] = a_ref[...].astype(BF16)
    bo_ref[...] = b_ref[...].astype(BF16)


def _ffn_first_weights(w_in, w_out):
    steps = D_FF // ROW_BLOCK
    cols_in = 2 * D_FF // steps
    return pl.pallas_call(
        _cast_body,
        grid=(steps,),
        in_specs=[pl.BlockSpec((None, D_MODEL, cols_in), lambda c: (0, 0, c)),
                  pl.BlockSpec((None, ROW_BLOCK, D_MODEL), lambda c: (0, c, 0))],
        out_specs=[pl.BlockSpec((D_MODEL, cols_in), lambda c: (0, c)),
                   pl.BlockSpec((ROW_BLOCK, D_MODEL), lambda c: (c, 0))],
        out_shape=[jax.ShapeDtypeStruct(w_in.shape[1:], BF16), jax.ShapeDtypeStruct(w_out.shape[1:], BF16)],
        compiler_params=_cparams(1),
        name="ffn_weight_cast",
    )(w_in, w_out)


def kernel(x_prompt, x_sample, cache_ckv, cache_kpe, c, c_ctx, w_ada, b_ada, w_fnet, w_pool, pool_scale,
           w_dq, g_q, w_uq, w_dkv, g_kv, w_ukv, w_o, w_ffn_in, w_ffn_conv, b_ffn_conv, w_ffn_out, g_final):
    batch, seq, _ = x_prompt.shape
    dec_batch, dec_seq, _ = x_sample.shape
    past = cache_ckv.shape[2]

    cond = jnp.concatenate([c_ctx[None, :], c, jnp.zeros((SUBLANES - 1 - dec_batch, D_MODEL), F32)], axis=0)
    b_ada = b_ada.reshape(DEPTH, 1, N_MOD * D_MODEL)
    mods = {0: _ada_first_layer(cond, w_ada, b_ada)}
    ada_layers_per_call = (batch * seq // FFN_TILE) // (N_MOD * D_MODEL // ADA_SHARE_COLS)
    w_in_bf, w_out_bf = _ffn_first_weights(w_ffn_in, w_ffn_out)
    b_conv = b_ffn_conv[:, None, :]

    xp = x_prompt.reshape(batch * seq, D_MODEL)
    xs = x_sample.reshape(dec_batch * dec_seq, D_MODEL)
    states = []
    j_f = j_p = j_a = 0
    for i in range(DEPTH):
        mod_p = mods[i][0:1].reshape(1, 1, N_MOD * D_MODEL)
        mod_s = mods[i][1:1 + dec_batch].reshape(dec_batch, 1, N_MOD * D_MODEL)
        kind = i % N_MIXERS
        if kind == 0:
            w = w_fnet[j_f].astype(BF16)
            xp = _fourier_mix(xp, mod_p, seq, w)
            xs = _fourier_mix(xs, mod_s, dec_seq, w)
            j_f += 1
        elif kind == 1:
            w = w_pool[j_p].astype(BF16)
            ps = pool_scale[j_p].reshape(1, D_MODEL)
            xp = _pool_mix(xp, mod_p, seq, w, ps)
            xs = _pool_mix(xs, mod_s, dec_seq, w, ps)
            j_p += 1
        else:
            prm = _mla_weights(w_dq[j_a], g_q[j_a], w_uq[j_a], w_dkv[j_a], g_kv[j_a], w_ukv[j_a], w_o[j_a])
            xp, ckv_p, kpe_p = _mla_context(xp, mod_p, seq, prm)
            states.append((ckv_p.reshape(batch, seq, KV_RANK), kpe_p.reshape(batch, seq, QK_ROPE)))
            q, k, v = _mla_project(xs, mod_s, dec_seq, prm, cs=_rope_table(dec_seq))
            kpe_c = cache_kpe[:, j_a].reshape(dec_batch * past, QK_ROPE)
            krc = jnp.concatenate([kpe_c, jnp.zeros_like(kpe_c)], axis=1).astype(BF16)
            kc, vc = _ctx_expand(cache_ckv[:, j_a].reshape(dec_batch * past, KV_RANK), krc, prm["w_ukv"])
            xs = _mla_attend(xs, mod_s, dec_seq, q, k, v, prm["w_o"], ctx=(kc, vc))
            j_a += 1
        g_fin = g_final.reshape(1, D_MODEL) if i == DEPTH - 1 else None
        ffn_w = (w_in_bf, w_ffn_conv, b_conv, w_out_bf)
        if i + 1 < DEPTH:
            todo = [l for l in range(i + 1, DEPTH) if l not in mods][:ada_layers_per_call]
            assert i + 1 in mods or (todo and todo[0] == i + 1)
            ada_later = (cond, w_ada, b_ada, tuple(todo)) if todo else None
            out = _conv_ffn(xp, mod_p, seq, i, *ffn_w, cast_next=(w_ffn_in, w_ffn_out), ada_later=ada_later)
            xp, w_in_bf, w_out_bf = out[:3]
            for n_done, l in enumerate(todo):
                mods[l] = out[3][n_done]
        else:
            xp, = _conv_ffn(xp, mod_p, seq, i, *ffn_w, g_final=g_fin)
        xs, = _conv_ffn(xs, mod_s, dec_seq, i, *ffn_w, g_final=g_fin)

    state_ckv = jnp.stack([s[0] for s in states], axis=1)
    state_kpe = jnp.stack([s[1] for s in states], axis=1)
    return (xp.reshape(batch, seq, D_MODEL), xs.reshape(dec_batch, dec_seq, D_MODEL), state_ckv, state_kpe)
```

```python
import functools

import numpy as np
import jax
import jax.numpy as jnp
from jax import lax
from jax.experimental import pallas as pl
from jax.experimental.pallas import tpu as pltpu

F32 = jnp.float32
BF16 = jnp.bfloat16

D_MODEL = 1024
DEPTH = 4
GRID_W = 64
N_MIXERS = 3
N_FOURIER_GROUPS = 4
FOURIER_GROUP = D_MODEL // N_FOURIER_GROUPS
POOL_WINDOWS = (2, 4, 8, 16)
POOL_GROUP = D_MODEL // len(POOL_WINDOWS)
N_HEADS = 8
QK_NOPE = 128
QK_ROPE = 64
V_HEAD = 128
Q_RANK = 768
KV_RANK = 512
ROPE_AXIS = QK_ROPE // 2
ROPE_BASE = 10000.0
SM_SCALE = (QK_NOPE + QK_ROPE) ** -0.5
D_FF = 2816
N_MOD = 6
EPS = 1e-6

SUBLANES = 8
LANES = 128
BF16_SUBLANES = 16
VMEM_LIMIT_BYTES = 56 * 1024 * 1024
ATT_VMEM_LIMIT_BYTES = 58 * 1024 * 1024

ROW_BLOCK = 256
FFN_CHUNK = 512
FFN_TILE = 512
MLA_TILE = 1024
ATT_Q_TILE = 512
ADA_COLS = 2048
ADA_SHARE_COLS = 1024
POOL_HALO = 128
POOL_SEQS = 4
FNET_SEQS = 4
ATT_SEQS = 4
FNET_STEP_BLOCKS = 2
HEAD_LANES = 2 * QK_ROPE
QK_TILE = QK_NOPE + HEAD_LANES


def _cparams(n_axes, vmem_limit_bytes=VMEM_LIMIT_BYTES):
    return pltpu.CompilerParams(dimension_semantics=("arbitrary",) * n_axes,
                                vmem_limit_bytes=vmem_limit_bytes)


def _rms(x):
    return x * lax.rsqrt(jnp.mean(x * x, axis=-1, keepdims=True) + EPS)


def _modulate(x, mod, sub):
    shift = mod[:, (3 * sub) * D_MODEL:(3 * sub + 1) * D_MODEL]
    scale = mod[:, (3 * sub + 1) * D_MODEL:(3 * sub + 2) * D_MODEL]
    return _rms(x) * (1.0 + scale) + shift


def _gate(mod, sub):
    return mod[:, (3 * sub + 2) * D_MODEL:(3 * sub + 3) * D_MODEL]


def _dot(a, b):
    return jnp.dot(a, b, preferred_element_type=F32)


def _dot_nt(a, b):
    return lax.dot_general(a, b, (((1,), (1,)), ((), ())), preferred_element_type=F32)


def _ada_body(cond_ref, w_ref, b_ref, o_ref):
    a = jax.nn.silu(cond_ref[...]).astype(BF16)
    o_ref[...] = _dot(a, w_ref[...].astype(BF16)) + b_ref[...]


def _ada_first_layer(cond, w_ada, b_ada):
    n_out = N_MOD * D_MODEL
    return pl.pallas_call(
        _ada_body,
        grid=(n_out // ADA_COLS,),
        in_specs=[
            pl.BlockSpec((SUBLANES, D_MODEL), lambda j: (0, 0)),
            pl.BlockSpec((None, D_MODEL, ADA_COLS), lambda j: (0, 0, j)),
            pl.BlockSpec((None, 1, ADA_COLS), lambda j: (0, 0, j)),
        ],
        out_specs=pl.BlockSpec((SUBLANES, ADA_COLS), lambda j: (0, j)),
        out_shape=jax.ShapeDtypeStruct((SUBLANES, n_out), F32),
        compiler_params=_cparams(1),
        name="ada_mod",
    )(cond, w_ada, b_ada)


def _ffn_chunks():
    edges = list(range(0, D_FF, FFN_CHUNK)) + [D_FF]
    return list(zip(edges[:-1], edges[1:]))


def _ffn_body(seq_len, n_seg, seg, final, cast_next, ada_later, *refs):
    refs = list(refs)
    x_ref, xp_ref, xn_ref, mod_ref, win_ref, wc_ref, bc_ref, wout_ref = refs[:8]
    del refs[:8]
    gfin_ref = refs.pop(0) if final else None
    side_in = [refs.pop(0) for _ in range(2 * cast_next + 3 * ada_later)]
    o_ref = refs.pop(0)
    side_out = [refs.pop(0) for _ in range(2 * cast_next + ada_later)]
    if cast_next:
        side_out[0][...] = side_in[0][...].astype(BF16)
        side_out[1][...] = side_in[1][...].astype(BF16)
    if ada_later:
        _ada_body(*side_in[-3:], side_out[-1])
    hext_ref, uga_ref, uva_ref, ugb_ref, uvb_ref, acta_ref, actb_ref, acc_ref = refs
    tile = n_seg * seg
    gap = BF16_SUBLANES
    i = pl.program_id(0)
    mod = mod_ref[...]
    x = x_ref[...]

    start = i * tile
    prev_ok = (start % seq_len != 0).astype(F32)
    next_ok = ((start + tile) % seq_len != 0).astype(F32)
    zeros8 = jnp.zeros((SUBLANES, D_MODEL), F32)
    h_prev = _modulate(xp_ref[...], mod, 1) * prev_ok
    h_next = _modulate(xn_ref[...], mod, 1) * next_ok
    hext_ref[0:gap, :] = jnp.concatenate([zeros8, h_prev], axis=0).astype(BF16)
    h = _modulate(x, mod, 1).astype(BF16)
    for k in range(n_seg):
        base = gap + k * (seg + gap)
        hext_ref[base:base + seg, :] = h[k * seg:(k + 1) * seg, :]
        if k == n_seg - 1:
            tail = jnp.concatenate([h_next, zeros8], axis=0).astype(BF16)
        else:
            tail = jnp.zeros((gap, D_MODEL), BF16)
        hext_ref[base + seg:base + seg + gap, :] = tail
    acc_ref[...] = jnp.zeros_like(acc_ref)

    def up(cols, ug_ref, uv_ref):
        lo, hi = cols
        ug_ref[:, :hi - lo] = _dot(hext_ref[...], win_ref[:, lo:hi])
        uv_ref[:, :hi - lo] = _dot(hext_ref[...], win_ref[:, D_FF + lo:D_FF + hi])

    def conv(u_ref, b, lo, hi):
        wc = wc_ref[:, lo:hi]
        return (u_ref[b - 1:b - 1 + seg, :hi - lo] * wc[0:1, :] + u_ref[b:b + seg, :hi - lo] * wc[1:2, :]
                + u_ref[b + 1:b + 1 + seg, :hi - lo] * wc[2:3, :] + bc_ref[:, lo:hi])

    def down(cols, ug_ref, uv_ref, act_ref):
        lo, hi = cols
        for k in range(n_seg):
            b = gap + k * (seg + gap)
            a = jax.nn.silu(conv(ug_ref, b, lo, hi)) * conv(uv_ref, b, D_FF + lo, D_FF + hi)
            act_ref[k * seg:(k + 1) * seg, :hi - lo] = a.astype(BF16)
        acc_ref[...] += _dot(act_ref[:, :hi - lo], wout_ref[lo:hi, :])

    chunks = _ffn_chunks()
    u_bufs = ((uga_ref, uva_ref), (ugb_ref, uvb_ref))
    act_bufs = (acta_ref, actb_ref)
    up(chunks[0], *u_bufs[0])
    for c, cols in enumerate(chunks):
        if c + 1 < len(chunks):
            up(chunks[c + 1], *u_bufs[(c + 1) % 2])
        down(cols, *u_bufs[c % 2], act_bufs[c % 2])
    y = x + _gate(mod, 1) * acc_ref[...]
    if final:
        y = _rms(y) * gfin_ref[...]
    o_ref[...] = y


def _conv_ffn(x, mod, seq_len, layer, w_in, w_conv, b_conv, w_out, g_final=None, cast_next=None,
              ada_later=None):
    n = x.shape[0]
    tile = FFN_TILE
    seg = min(seq_len, tile)
    n_seg = tile // seg
    rows = BF16_SUBLANES + n_seg * (seg + BF16_SUBLANES)
    per_seq_mod = mod.shape[0] > 1
    final = g_final is not None
    blocks8 = n // SUBLANES
    t8 = tile // SUBLANES

    def resident(a):
        return pl.BlockSpec(a.shape, lambda i: (0,) * a.ndim, pipeline_mode=pl.Buffered(1))

    def resident_layer(a):
        shape = a.shape[1:]
        return pl.BlockSpec((None,) + shape, lambda i: (layer,) + (0,) * len(shape),
                            pipeline_mode=pl.Buffered(1))

    in_specs = [
        pl.BlockSpec((tile, D_MODEL), lambda i: (i, 0)),
        pl.BlockSpec((SUBLANES, D_MODEL), lambda i: (jnp.maximum(i * t8 - 1, 0), 0)),
        pl.BlockSpec((SUBLANES, D_MODEL), lambda i: (jnp.minimum((i + 1) * t8, blocks8 - 1), 0)),
        pl.BlockSpec((None, 1, N_MOD * D_MODEL),
                     (lambda i: ((i * tile) // seq_len, 0, 0)) if per_seq_mod else (lambda i: (0, 0, 0))),
        resident(w_in), resident_layer(w_conv), resident_layer(b_conv), resident(w_out),
    ]
    args = [x, x, x, mod, w_in, w_conv, b_conv, w_out]
    if final:
        in_specs.append(pl.BlockSpec((1, D_MODEL), lambda i: (0, 0)))
        args.append(g_final)
    out_specs = [pl.BlockSpec((tile, D_MODEL), lambda i: (i, 0))]
    out_shape = [jax.ShapeDtypeStruct((n, D_MODEL), F32)]
    if cast_next is not None:
        steps = n // tile
        for a in cast_next:
            rows_step = a.shape[1] // steps
            assert rows_step * steps == a.shape[1] and rows_step % BF16_SUBLANES == 0
            in_specs.append(pl.BlockSpec((None, rows_step, a.shape[2]), lambda i: (layer + 1, i, 0)))
            out_specs.append(pl.BlockSpec((rows_step, a.shape[2]), lambda i: (i, 0)))
            out_shape.append(jax.ShapeDtypeStruct(a.shape[1:], BF16))
            args.append(a)
    if ada_later is not None:
        cond, w_ada, b_ada, layers = ada_later
        blocks_per_layer = N_MOD * D_MODEL // ADA_SHARE_COLS
        n_share = len(layers) * blocks_per_layer
        assert n_share <= n // tile and list(layers) == list(range(layers[0], layers[0] + len(layers)))

        def share(i):
            s = jnp.minimum(i, n_share - 1)
            return s // blocks_per_layer, s % blocks_per_layer

        in_specs += [
            pl.BlockSpec((SUBLANES, D_MODEL), lambda i: (0, 0)),
            pl.BlockSpec((None, D_MODEL, ADA_SHARE_COLS), lambda i: (layers[0] + share(i)[0], 0, share(i)[1])),
            pl.BlockSpec((None, 1, ADA_SHARE_COLS), lambda i: (layers[0] + share(i)[0], 0, share(i)[1])),
        ]
        args += [cond, w_ada, b_ada]
        out_specs.append(pl.BlockSpec((None, SUBLANES, ADA_SHARE_COLS), lambda i: (share(i)[0], 0, share(i)[1])))
        out_shape.append(jax.ShapeDtypeStruct((len(layers), SUBLANES, N_MOD * D_MODEL), F32))
    return pl.pallas_call(
        functools.partial(_ffn_body, seq_len, n_seg, seg, final, cast_next is not None, ada_later is not None),
        grid=(n // tile,),
        in_specs=in_specs,
        out_specs=out_specs,
        out_shape=out_shape,
        scratch_shapes=[pltpu.VMEM((rows, D_MODEL), BF16)]
        + [pltpu.VMEM((rows, FFN_CHUNK), F32)] * 4
        + [pltpu.VMEM((tile, FFN_CHUNK), BF16)] * 2
        + [pltpu.VMEM((tile, D_MODEL), F32)],
        compiler_params=_cparams(1),
        name="conv_ffn",
    )(*args)


def _fnet_body(seq_len, n_seq, x_ref, mod_ref, csc_ref, csrow_ref, w_ref, o_ref):
    mod = mod_ref[...]
    x = x_ref[...]
    h = _modulate(x, mod, 0).astype(BF16)
    table = csc_ref[...].astype(BF16)
    parts = [_dot(h[:, g * FOURIER_GROUP:(g + 1) * FOURIER_GROUP], table) for g in range(N_FOURIER_GROUPS)]
    cos_part = jnp.concatenate([p[:, :FOURIER_GROUP] for p in parts], axis=1).astype(BF16)
    sin_part = jnp.concatenate([p[:, FOURIER_GROUP:] for p in parts], axis=1).astype(BF16)
    positions = csrow_ref[...].astype(BF16)
    scale = (seq_len * FOURIER_GROUP) ** -0.5
    f = []
    for s in range(n_seq):
        rows = slice(s * seq_len, (s + 1) * seq_len)
        stack = jnp.concatenate([cos_part[rows], sin_part[rows]], axis=0)
        f.append((_dot(positions, stack) * scale).astype(BF16))
    y = _dot(jnp.concatenate(f, axis=0), w_ref[...])
    o_ref[...] = x + _gate(mod, 0) * y


def _fnet_split_body(seq_len, cast_first, *refs):
    if cast_first:
        (x_ref, mod_ref, csc_ref, even_ref, odd_ref, perm_ref, w_ref, win_ref, wout_ref,
         o_ref, win_bf_ref, wout_bf_ref, plus_ref, minus_ref) = refs
        win_bf_ref[...] = win_ref[...].astype(BF16)
        wout_bf_ref[...] = wout_ref[...].astype(BF16)
    else:
        x_ref, mod_ref, csc_ref, even_ref, odd_ref, perm_ref, w_ref, o_ref, plus_ref, minus_ref = refs
    r = pl.program_id(1)
    mod = mod_ref[...]
    half = seq_len // 2

    @pl.when(r == 0)
    def _():
        def rows(j, carry):
            r0 = pl.multiple_of(j * ROW_BLOCK, ROW_BLOCK)
            h_lo = _modulate(x_ref[pl.ds(r0, ROW_BLOCK), :], mod, 0).astype(BF16)
            h_hi = _modulate(x_ref[pl.ds(half + r0, ROW_BLOCK), :], mod, 0).astype(BF16)
            table = csc_ref[...].astype(BF16)
            for g in range(N_FOURIER_GROUPS):
                cols = slice(g * FOURIER_GROUP, (g + 1) * FOURIER_GROUP)
                y_lo = _dot(h_lo[:, cols], table)
                y_hi = _dot(h_hi[:, cols], table)
                for ref, y in ((plus_ref, y_lo + y_hi), (minus_ref, y_lo - y_hi)):
                    ref[pl.ds(r0, ROW_BLOCK), cols] = y[:, :FOURIER_GROUP].astype(BF16)
                    ref[pl.ds(half + r0, ROW_BLOCK), cols] = y[:, FOURIER_GROUP:].astype(BF16)
            return carry
        lax.fori_loop(0, half // ROW_BLOCK, rows, 0)

    scale = (seq_len * FOURIER_GROUP) ** -0.5
    even = (_dot(even_ref[...].astype(BF16), plus_ref[...]) * scale).astype(BF16)
    odd = (_dot(odd_ref[...].astype(BF16), minus_ref[...]) * scale).astype(BF16)
    half_block = ROW_BLOCK // 2
    f = []
    for j in range(FNET_STEP_BLOCKS):
        rows = slice(j * half_block, (j + 1) * half_block)
        f.append(_dot(perm_ref[...], jnp.concatenate([even[rows], odd[rows]], axis=0)).astype(BF16))
    y = _dot(jnp.concatenate(f, axis=0), w_ref[...])
    step_rows = FNET_STEP_BLOCKS * ROW_BLOCK
    r0 = pl.multiple_of(r * step_rows, step_rows)
    o_ref[...] = x_ref[pl.ds(r0, step_rows), :] + _gate(mod, 0) * y


def _channel_table():
    c = np.arange(FOURIER_GROUP)
    ang = 2.0 * np.pi * ((c[:, None] * c[None, :]) % FOURIER_GROUP) / FOURIER_GROUP
    return jnp.asarray(np.concatenate([np.cos(ang), np.sin(ang)], axis=1), F32)


def _position_table(freqs, positions, seq_len):
    ang = 2.0 * np.pi * ((freqs[:, None] * positions[None, :]) % seq_len) / seq_len
    return jnp.asarray(np.concatenate([np.cos(ang), -np.sin(ang)], axis=1), F32)


def _fourier_mix(x, mod, seq_len, w, cast_first=None):
    n = x.shape[0]
    batch = n // seq_len
    n_row_blocks = seq_len // ROW_BLOCK
    mod_spec = pl.BlockSpec((None, 1, N_MOD * D_MODEL),
                            (lambda b, r: (b, 0, 0)) if mod.shape[0] > 1 else (lambda b, r: (0, 0, 0)))
    common = dict(
        grid=(batch, n_row_blocks),
        out_specs=pl.BlockSpec((ROW_BLOCK, D_MODEL), lambda b, r: (b * n_row_blocks + r, 0)),
        out_shape=jax.ShapeDtypeStruct((n, D_MODEL), F32),
        compiler_params=_cparams(2),
        name="fourier_mix",
    )
    x_spec = pl.BlockSpec((seq_len, D_MODEL), lambda b, r: (b, 0))
    csc_spec = pl.BlockSpec((FOURIER_GROUP, 2 * FOURIER_GROUP), lambda b, r: (0, 0))
    w_spec = pl.BlockSpec((D_MODEL, D_MODEL), lambda b, r: (0, 0))
    t = np.arange(seq_len)
    if n_row_blocks == 1:
        n_seq = FNET_SEQS if (mod.shape[0] == 1 and batch % FNET_SEQS == 0) else 1
        rows = pl.BlockSpec((n_seq * seq_len, D_MODEL), lambda b, r: (b, 0))
        common.update(grid=(batch // n_seq, 1), out_specs=rows)
        return pl.pallas_call(
            functools.partial(_fnet_body, seq_len, n_seq),
            in_specs=[rows, mod_spec, csc_spec, pl.BlockSpec((seq_len, 2 * seq_len), lambda b, r: (0, 0)), w_spec],
            **common,
        )(x, mod, _channel_table(), _position_table(t, t, seq_len), w)
    half_block = ROW_BLOCK // 2
    u = np.arange(seq_len // 2)
    i = np.arange(half_block)
    perm = np.zeros((ROW_BLOCK, ROW_BLOCK), np.float32)
    perm[2 * i, i] = 1.0
    perm[2 * i + 1, half_block + i] = 1.0
    step_rows = FNET_STEP_BLOCKS * ROW_BLOCK
    steps = seq_len // step_rows
    table_spec = pl.BlockSpec((step_rows // 2, seq_len), lambda b, r: (r, 0))
    common.update(grid=(batch, steps),
                  out_specs=pl.BlockSpec((step_rows, D_MODEL), lambda b, r: (b * steps + r, 0)))
    in_specs = [x_spec, mod_spec, csc_spec, table_spec, table_spec,
                pl.BlockSpec((ROW_BLOCK, ROW_BLOCK), lambda b, r: (0, 0)), w_spec]
    args = [x, mod, _channel_table(), _position_table(2 * u, u, seq_len), _position_table(2 * u + 1, u, seq_len),
            jnp.asarray(perm, BF16), w]
    if cast_first is not None:
        out_specs, out_shape = [common["out_specs"]], [common["out_shape"]]
        for a in cast_first:
            rows_step = a.shape[1] // (batch * steps)
            assert rows_step * batch * steps == a.shape[1] and rows_step % BF16_SUBLANES == 0
            in_specs.append(pl.BlockSpec((None, rows_step, a.shape[2]), lambda b, r: (0, b * steps + r, 0)))
            out_specs.append(pl.BlockSpec((rows_step, a.shape[2]), lambda b, r: (b * steps + r, 0)))
            out_shape.append(jax.ShapeDtypeStruct(a.shape[1:], BF16))
            args.append(a)
        common.update(out_specs=out_specs, out_shape=out_shape)
    return pl.pallas_call(
        functools.partial(_fnet_split_body, seq_len, cast_first is not None),
        in_specs=in_specs,
        scratch_shapes=[pltpu.VMEM((seq_len, D_MODEL), BF16)] * 2,
        **common,
    )(*args)


def _pool_bands():
    i = np.arange(ROW_BLOCK)[:, None]
    c = np.arange(2 * ROW_BLOCK)[None, :] - POOL_HALO
    bands = [((c >= i - w // 2) & (c < i - w // 2 + w)) for w in POOL_WINDOWS]
    return jnp.asarray(np.stack(bands), BF16)


def _pool_body(seq_len, n_seq, x_ref, mod_ref, band_ref, w_ref, ps_ref, o_ref, h_ref, hb_ref):
    mod = mod_ref[...]
    n_blocks = seq_len // ROW_BLOCK
    padded = seq_len + 2 * POOL_HALO
    zeros = jnp.zeros((POOL_HALO, D_MODEL), BF16)
    for s in range(n_seq):
        hb_ref[s * padded:s * padded + POOL_HALO, :] = zeros
        hb_ref[(s + 1) * padded - POOL_HALO:(s + 1) * padded, :] = zeros
        for j in range(n_blocks):
            rows = slice(s * seq_len + j * ROW_BLOCK, s * seq_len + (j + 1) * ROW_BLOCK)
            h = _modulate(x_ref[rows, :], mod, 0)
            h_ref[rows, :] = h
            start = s * padded + POOL_HALO + j * ROW_BLOCK
            hb_ref[start:start + ROW_BLOCK, :] = h.astype(BF16)

    gate = _gate(mod, 0)
    for s in range(n_seq):
        for j in range(n_blocks):
            rows = slice(s * seq_len + j * ROW_BLOCK, s * seq_len + (j + 1) * ROW_BLOCK)
            slab = slice(s * padded + j * ROW_BLOCK, s * padded + (j + 2) * ROW_BLOCK)
            t = j * ROW_BLOCK + lax.broadcasted_iota(jnp.int32, (ROW_BLOCK, POOL_GROUP), 0)
            for g, w in enumerate(POOL_WINDOWS):
                cols = slice(g * POOL_GROUP, (g + 1) * POOL_GROUP)
                total = _dot(band_ref[g], hb_ref[slab, cols])
                cnt = jnp.minimum(t - w // 2 + w, seq_len) - jnp.maximum(t - w // 2, 0)
                pooled = total / cnt.astype(F32) - h_ref[rows, cols]
                y = _dot(pooled.astype(BF16), w_ref[g]) * ps_ref[:, cols]
                o_ref[rows, cols] = x_ref[rows, cols] + gate[:, cols] * y


def _pool_mix(x, mod, seq_len, w_pool, pool_scale):
    n = x.shape[0]
    batch = n // seq_len
    assert POOL_HALO >= max(POOL_WINDOWS) // 2 and 2 * POOL_HALO == ROW_BLOCK
    n_seq = POOL_SEQS if (seq_len == ROW_BLOCK and mod.shape[0] == 1 and batch % POOL_SEQS == 0) else 1
    rows = n_seq * seq_len
    return pl.pallas_call(
        functools.partial(_pool_body, seq_len, n_seq),
        grid=(batch // n_seq,),
        in_specs=[
            pl.BlockSpec((rows, D_MODEL), lambda b: (b, 0)),
            pl.BlockSpec((None, 1, N_MOD * D_MODEL),
                         (lambda b: (b, 0, 0)) if mod.shape[0] > 1 else (lambda b: (0, 0, 0))),
            pl.BlockSpec((len(POOL_WINDOWS), ROW_BLOCK, 2 * ROW_BLOCK), lambda b: (0, 0, 0)),
            pl.BlockSpec((len(POOL_WINDOWS), POOL_GROUP, POOL_GROUP), lambda b: (0, 0, 0)),
            pl.BlockSpec((1, D_MODEL), lambda b: (0, 0)),
        ],
        out_specs=pl.BlockSpec((rows, D_MODEL), lambda b: (b, 0)),
        out_shape=jax.ShapeDtypeStruct((n, D_MODEL), F32),
        scratch_shapes=[pltpu.VMEM((rows, D_MODEL), F32),
                        pltpu.VMEM((n_seq * (seq_len + 2 * POOL_HALO), D_MODEL), BF16)],
        compiler_params=_cparams(1),
        name="pool_mix",
    )(x, mod, _pool_bands(), w_pool, pool_scale)


def _rope_pair(a, cs):
    t = a * cs
    return t + pltpu.roll(t, QK_ROPE, 1)


def _store_keys(k_ref, k_nope, k_rope):
    for hd in range(N_HEADS):
        k_ref[:, hd * QK_TILE:hd * QK_TILE + QK_NOPE] = k_nope[:, hd * QK_NOPE:(hd + 1) * QK_NOPE]
        k_ref[:, hd * QK_TILE + QK_NOPE:(hd + 1) * QK_TILE] = k_rope


def _mla_proj_body(rope, state, *refs):
    refs = list(refs)
    x_ref, mod_ref = refs[:2]
    del refs[:2]
    cs_ref = refs.pop(0) if rope else None
    wdq_ref, gq_ref, wuq_ref, wdkv_ref, gkv_ref, wukv_ref = refs[:6]
    del refs[:6]
    q_ref, k_ref, v_ref = refs[:3]
    del refs[:3]
    h = _modulate(x_ref[...], mod_ref[...], 0).astype(BF16)
    cq = (_rms(_dot(h, wdq_ref[...])) * gq_ref[...]).astype(BF16)
    q = _dot(cq, wuq_ref[...]) * SM_SCALE
    n_nope = N_HEADS * QK_NOPE
    for hd in range(N_HEADS):
        a = q[:, n_nope + hd * HEAD_LANES:n_nope + (hd + 1) * HEAD_LANES]
        if rope:
            a = _rope_pair(a, cs_ref[...])
        q_ref[:, hd * QK_TILE:hd * QK_TILE + QK_NOPE] = q[:, hd * QK_NOPE:(hd + 1) * QK_NOPE].astype(BF16)
        q_ref[:, hd * QK_TILE + QK_NOPE:(hd + 1) * QK_TILE] = a.astype(BF16)
    ckv = _dot(h, wdkv_ref[...])
    c = _rms(ckv[:, :KV_RANK]) * gkv_ref[...]
    kp = ckv[:, KV_RANK:]
    if state:
        ckv_out_ref, kpe_out_ref = refs
        ckv_out_ref[...] = c
        kpe_out_ref[...] = kp[:, :QK_ROPE]
    if rope:
        kp = _rope_pair(kp, cs_ref[...])
    lane = lax.broadcasted_iota(jnp.int32, kp.shape, 1)
    kr = jnp.where(lane < QK_ROPE, kp, 0.0).astype(BF16)
    kv = _dot(c.astype(BF16), wukv_ref[...])
    _store_keys(k_ref, kv[:, :n_nope].astype(BF16), kr)
    v_ref[...] = kv[:, n_nope:].astype(BF16)


def _mla_project(x, mod, seq_len, prm, cs=None, state=False):
    n = x.shape[0]
    rope = cs is not None
    tile = min(MLA_TILE, seq_len) if (rope or mod.shape[0] > 1) else MLA_TILE
    tiles_per_seq = max(seq_len // tile, 1)
    wide = N_HEADS * QK_NOPE

    def const(shape):
        return pl.BlockSpec(shape, lambda i: (0,) * len(shape), pipeline_mode=pl.Buffered(1))

    def rows(width):
        return pl.BlockSpec((tile, width), lambda i: (i, 0))

    in_specs = [rows(D_MODEL),
                pl.BlockSpec((None, 1, N_MOD * D_MODEL),
                             (lambda i: (i // tiles_per_seq, 0, 0)) if mod.shape[0] > 1
                             else (lambda i: (0, 0, 0)))]
    args = [x, mod]
    if rope:
        in_specs.append(pl.BlockSpec((tile, HEAD_LANES), lambda i: (i % tiles_per_seq, 0)))
        args.append(cs)
    for name in ("w_dq", "g_q", "w_uq", "w_dkv", "g_kv", "w_ukv"):
        in_specs.append(const(prm[name].shape))
        args.append(prm[name])
    out_specs = [rows(N_HEADS * QK_TILE), rows(N_HEADS * QK_TILE), rows(wide)]
    out_shape = [jax.ShapeDtypeStruct((n, N_HEADS * QK_TILE), BF16),
                 jax.ShapeDtypeStruct((n, N_HEADS * QK_TILE), BF16),
                 jax.ShapeDtypeStruct((n, wide), BF16)]
    if state:
        out_specs += [rows(KV_RANK), rows(QK_ROPE)]
        out_shape += [jax.ShapeDtypeStruct((n, KV_RANK), F32), jax.ShapeDtypeStruct((n, QK_ROPE), F32)]
    return pl.pallas_call(
        functools.partial(_mla_proj_body, rope, state),
        grid=(n // tile,),
        in_specs=in_specs,
        out_specs=out_specs,
        out_shape=out_shape,
        compiler_params=_cparams(1),
        name="mla_project",
    )(*args)


def _ctx_expand_body(c_ref, kr_ref, w_ref, k_ref, v_ref):
    kv = _dot(c_ref[...].astype(BF16), w_ref[...])
    n_nope = N_HEADS * QK_NOPE
    _store_keys(k_ref, kv[:, :n_nope].astype(BF16), kr_ref[...])
    v_ref[...] = kv[:, n_nope:].astype(BF16)


def _ctx_expand(c_kv, k_rope, w_ukv):
    n = c_kv.shape[0]
    wide = N_HEADS * QK_NOPE
    return pl.pallas_call(
        _ctx_expand_body,
        grid=(n // ROW_BLOCK,),
        in_specs=[pl.BlockSpec((ROW_BLOCK, KV_RANK), lambda i: (i, 0)),
                  pl.BlockSpec((ROW_BLOCK, HEAD_LANES), lambda i: (i, 0)),
                  pl.BlockSpec(w_ukv.shape, lambda i: (0, 0))],
        out_specs=[pl.BlockSpec((ROW_BLOCK, N_HEADS * QK_TILE), lambda i: (i, 0)),
                   pl.BlockSpec((ROW_BLOCK, wide), lambda i: (i, 0))],
        out_shape=[jax.ShapeDtypeStruct((n, N_HEADS * QK_TILE), BF16), jax.ShapeDtypeStruct((n, wide), BF16)],
        compiler_params=_cparams(1),
        name="mla_ctx_expand",
    )(c_kv, k_rope, w_ukv)


def _attn_body(has_ctx, n_seq, *refs):
    refs = list(refs)
    q_ref, k_ref, v_ref = refs[:3]
    del refs[:3]
    if has_ctx:
        kc_ref, vc_ref = refs[:2]
        del refs[:2]
    x_ref, mod_ref, wo_ref, o_ref, heads_ref = refs
    n_q = q_ref.shape[0] // n_seq
    n_k = k_ref.shape[0] // n_seq
    for seq in range(n_seq):
        rows = slice(seq * n_q, (seq + 1) * n_q)
        keys = slice(seq * n_k, (seq + 1) * n_k)
        for hd in range(N_HEADS):
            qk = slice(hd * QK_TILE, (hd + 1) * QK_TILE)
            vcols = slice(hd * V_HEAD, (hd + 1) * V_HEAD)
            q = q_ref[rows, qk]
            s = _dot_nt(q, k_ref[keys, qk])
            m = jnp.max(s, axis=-1, keepdims=True)
            if has_ctx:
                sc = _dot_nt(q, kc_ref[:, qk])
                m = jnp.maximum(m, jnp.max(sc, axis=-1, keepdims=True))
            p = jnp.exp(s - m)
            l = jnp.sum(p, axis=-1, keepdims=True)
            o = _dot(p.astype(BF16), v_ref[keys, vcols])
            if has_ctx:
                pc = jnp.exp(sc - m)
                l = l + jnp.sum(pc, axis=-1, keepdims=True)
                o = o + _dot(pc.astype(BF16), vc_ref[:, vcols])
            heads_ref[rows, vcols] = (o / l).astype(BF16)
    y = _dot(heads_ref[...], wo_ref[...])
    o_ref[...] = x_ref[...] + _gate(mod_ref[...], 0) * y


def _mla_context_body(n_seq, x_ref, mod_ref, wdq_ref, gq_ref, wuq_ref, wdkv_ref, gkv_ref, wukv_ref, wo_ref,
                      o_ref, ckv_ref, kpe_ref, q_ref, k_ref, v_ref, heads_ref):
    _mla_proj_body(False, True, x_ref, mod_ref, wdq_ref, gq_ref, wuq_ref, wdkv_ref, gkv_ref, wukv_ref,
                   q_ref, k_ref, v_ref, ckv_ref, kpe_ref)
    _attn_body(False, n_seq, q_ref, k_ref, v_ref, x_ref, mod_ref, wo_ref, o_ref, heads_ref)


def _mla_context(x, mod, seq_len, prm):
    n = x.shape[0]
    n_seq = ATT_SEQS
    rows = n_seq * seq_len
    assert mod.shape[0] == 1 and n % rows == 0
    wide = N_HEADS * QK_NOPE

    def const(a):
        return pl.BlockSpec(a.shape, lambda i: (0,) * a.ndim, pipeline_mode=pl.Buffered(1))

    def tokens(width):
        return pl.BlockSpec((rows, width), lambda i: (i, 0))

    weights = [prm[name] for name in ("w_dq", "g_q", "w_uq", "w_dkv", "g_kv", "w_ukv", "w_o")]
    return pl.pallas_call(
        functools.partial(_mla_context_body, n_seq),
        grid=(n // rows,),
        in_specs=[tokens(D_MODEL), pl.BlockSpec((None, 1, N_MOD * D_MODEL), lambda i: (0, 0, 0))]
        + [const(w) for w in weights],
        out_specs=[tokens(D_MODEL), tokens(KV_RANK), tokens(QK_ROPE)],
        out_shape=[jax.ShapeDtypeStruct((n, D_MODEL), F32), jax.ShapeDtypeStruct((n, KV_RANK), F32),
                   jax.ShapeDtypeStruct((n, QK_ROPE), F32)],
        scratch_shapes=[pltpu.VMEM((rows, N_HEADS * QK_TILE), BF16), pltpu.VMEM((rows, N_HEADS * QK_TILE), BF16),
                        pltpu.VMEM((rows, wide), BF16), pltpu.VMEM((rows, wide), BF16)],
        compiler_params=_cparams(1, ATT_VMEM_LIMIT_BYTES),
        name="mla_context",
    )(x, mod, *weights)


def _mla_attend(x, mod, seq_len, q, k, v, w_o, ctx=None):
    n = x.shape[0]
    tq = min(ATT_Q_TILE, seq_len)
    q_tiles = seq_len // tq
    wide = N_HEADS * QK_NOPE
    has_ctx = ctx is not None
    batch = n // seq_len
    grouped = q_tiles == 1 and not has_ctx and mod.shape[0] == 1 and batch % ATT_SEQS == 0
    n_seq = ATT_SEQS if grouped else 1

    def q_rows(width):
        return pl.BlockSpec((n_seq * tq, width), lambda b, r: (b * q_tiles + r, 0))

    def seq_rows(length, width):
        mode = pl.Buffered(1) if q_tiles > 1 else None
        return pl.BlockSpec((n_seq * length, width), lambda b, r: (b, 0), pipeline_mode=mode)

    in_specs = [q_rows(N_HEADS * QK_TILE), seq_rows(seq_len, N_HEADS * QK_TILE), seq_rows(seq_len, wide)]
    args = [q, k, v]
    if has_ctx:
        kc, vc = ctx
        past = kc.shape[0] // (n // seq_len)
        in_specs += [seq_rows(past, N_HEADS * QK_TILE), seq_rows(past, wide)]
        args += [kc, vc]
    in_specs += [q_rows(D_MODEL),
                 pl.BlockSpec((None, 1, N_MOD * D_MODEL),
                              (lambda b, r: (b, 0, 0)) if mod.shape[0] > 1 else (lambda b, r: (0, 0, 0))),
                 pl.BlockSpec(w_o.shape, lambda b, r: (0, 0))]
    args += [x, mod, w_o]
    return pl.pallas_call(
        functools.partial(_attn_body, has_ctx, n_seq),
        grid=(batch // n_seq, q_tiles),
        in_specs=in_specs,
        out_specs=q_rows(D_MODEL),
        out_shape=jax.ShapeDtypeStruct((n, D_MODEL), F32),
        scratch_shapes=[pltpu.VMEM((n_seq * tq, wide), BF16)],
        compiler_params=_cparams(2, ATT_VMEM_LIMIT_BYTES),
        name="mla_attend",
    )(*args)


def _rope_table(n_tok):
    rows = n_tok // GRID_W
    row = np.repeat(np.arange(rows), GRID_W).astype(np.float32)
    col = np.tile(np.arange(GRID_W), rows).astype(np.float32)
    inv = (1.0 / (ROPE_BASE ** (np.arange(0, ROPE_AXIS, 2, dtype=np.float32) / ROPE_AXIS))).astype(np.float32)
    ang_r = row[:, None] * inv
    ang_c = col[:, None] * inv
    cos = np.concatenate([np.cos(ang_r), np.cos(ang_r), np.cos(ang_c), np.cos(ang_c)], axis=1)
    sin = np.concatenate([np.sin(ang_r), np.sin(ang_r), np.sin(ang_c), np.sin(ang_c)], axis=1)
    return jnp.asarray(np.concatenate([cos, sin], axis=1), F32)


def _swap_cols(w):
    half = ROPE_AXIS // 2
    return jnp.concatenate([-w[..., half:2 * half], w[..., 0:half],
                            -w[..., 3 * half:4 * half], w[..., 2 * half:3 * half]], axis=-1)


def _mla_weights(w_dq, g_q, w_uq, w_dkv, g_kv, w_ukv, w_o):
    wq = w_uq.reshape(Q_RANK, N_HEADS, QK_NOPE + QK_ROPE)
    wq_pe = wq[:, :, QK_NOPE:]
    wq_pairs = jnp.concatenate([wq_pe, _swap_cols(wq_pe)], axis=-1)
    w_uq2 = jnp.concatenate([wq[:, :, :QK_NOPE].reshape(Q_RANK, -1), wq_pairs.reshape(Q_RANK, -1)], axis=1)
    w_pe = w_dkv[:, KV_RANK:]
    w_dkv2 = jnp.concatenate([w_dkv[:, :KV_RANK], w_pe, _swap_cols(w_pe)], axis=1)
    wkv = w_ukv.reshape(KV_RANK, N_HEADS, QK_NOPE + V_HEAD)
    w_ukv2 = jnp.concatenate([wkv[:, :, :QK_NOPE].reshape(KV_RANK, -1),
                              wkv[:, :, QK_NOPE:].reshape(KV_RANK, -1)], axis=1)
    return {"w_dq": w_dq.astype(BF16), "g_q": g_q.reshape(1, Q_RANK), "w_uq": w_uq2.astype(BF16),
            "w_dkv": w_dkv2.astype(BF16), "g_kv": g_kv.reshape(1, KV_RANK), "w_ukv": w_ukv2.astype(BF16),
            "w_o": w_o.astype(BF16)}


def kernel(x_prompt, x_sample, cache_ckv, cache_kpe, c, c_ctx, w_ada, b_ada, w_fnet, w_pool, pool_scale,
           w_dq, g_q, w_uq, w_dkv, g_kv, w_ukv, w_o, w_ffn_in, w_ffn_conv, b_ffn_conv, w_ffn_out, g_final):
    batch, seq, _ = x_prompt.shape
    dec_batch, dec_seq, _ = x_sample.shape
    past = cache_ckv.shape[2]

    cond = jnp.concatenate([c_ctx[None, :], c, jnp.zeros((SUBLANES - 1 - dec_batch, D_MODEL), F32)], axis=0)
    b_ada = b_ada.reshape(DEPTH, 1, N_MOD * D_MODEL)
    mods = {0: _ada_first_layer(cond, w_ada, b_ada)}
    ada_layers_per_call = (batch * seq // FFN_TILE) // (N_MOD * D_MODEL // ADA_SHARE_COLS)
    b_conv = b_ffn_conv[:, None, :]

    xp = x_prompt.reshape(batch * seq, D_MODEL)
    xs = x_sample.reshape(dec_batch * dec_seq, D_MODEL)
    states = []
    j_f = j_p = j_a = 0
    for i in range(DEPTH):
        mod_p = mods[i][0:1].reshape(1, 1, N_MOD * D_MODEL)
        mod_s = mods[i][1:1 + dec_batch].reshape(dec_batch, 1, N_MOD * D_MODEL)
        kind = i % N_MIXERS
        if kind == 0:
            w = w_fnet[j_f].astype(BF16)
            xp = _fourier_mix(xp, mod_p, seq, w)
            if i == 0:
                xs, w_in_bf, w_out_bf = _fourier_mix(xs, mod_s, dec_seq, w, cast_first=(w_ffn_in, w_ffn_out))
            else:
                xs = _fourier_mix(xs, mod_s, dec_seq, w)
            j_f += 1
        elif kind == 1:
            w = w_pool[j_p].astype(BF16)
            ps = pool_scale[j_p].reshape(1, D_MODEL)
            xp = _pool_mix(xp, mod_p, seq, w, ps)
            xs = _pool_mix(xs, mod_s, dec_seq, w, ps)
            j_p += 1
        else:
            prm = _mla_weights(w_dq[j_a], g_q[j_a], w_uq[j_a], w_dkv[j_a], g_kv[j_a], w_ukv[j_a], w_o[j_a])
            xp, ckv_p, kpe_p = _mla_context(xp, mod_p, seq, prm)
            states.append((ckv_p.reshape(batch, seq, KV_RANK), kpe_p.reshape(batch, seq, QK_ROPE)))
            q, k, v = _mla_project(xs, mod_s, dec_seq, prm, cs=_rope_table(dec_seq))
            kpe_c = cache_kpe[:, j_a].reshape(dec_batch * past, QK_ROPE)
            krc = jnp.concatenate([kpe_c, jnp.zeros_like(kpe_c)], axis=1).astype(BF16)
            kc, vc = _ctx_expand(cache_ckv[:, j_a].reshape(dec_batch * past, KV_RANK), krc, prm["w_ukv"])
            xs = _mla_attend(xs, mod_s, dec_seq, q, k, v, prm["w_o"], ctx=(kc, vc))
            j_a += 1
        g_fin = g_final.reshape(1, D_MODEL) if i == DEPTH - 1 else None
        ffn_w = (w_in_bf, w_ffn_conv, b_conv, w_out_bf)
        if i + 1 < DEPTH:
            todo = [l for l in range(i + 1, DEPTH) if l not in mods][:ada_layers_per_call]
            assert i + 1 in mods or (todo and todo[0] == i + 1)
            ada_later = (cond, w_ada, b_ada, tuple(todo)) if todo else None
            out = _conv_ffn(xp, mod_p, seq, i, *ffn_w, cast_next=(w_ffn_in, w_ffn_out), ada_later=ada_later)
            xp, w_in_bf, w_out_bf = out[:3]
            for n_done, l in enumerate(todo):
                mods[l] = out[3][n_done]
        else:
            xp, = _conv_ffn(xp, mod_p, seq, i, *ffn_w, g_final=g_fin)
        xs, = _conv_ffn(xs, mod_s, dec_seq, i, *ffn_w, g_final=g_fin)

    state_ckv = jnp.stack([s[0] for s in states], axis=1)
    state_kpe = jnp.stack([s[1] for s in states], axis=1)
    return (xp.reshape(batch, seq, D_MODEL), xs.reshape(dec_batch, dec_seq, D_MODEL), state_ckv, state_kpe)
```

```python
import functools

import numpy as np
import jax
import jax.numpy as jnp
from jax import lax
from jax.experimental import pallas as pl
from jax.experimental.pallas import tpu as pltpu

F32 = jnp.float32
BF16 = jnp.bfloat16

D_MODEL = 1024
DEPTH = 4
GRID_W = 64
N_MIXERS = 3
N_FOURIER_GROUPS = 4
FOURIER_GROUP = D_MODEL // N_FOURIER_GROUPS
POOL_WINDOWS = (2, 4, 8, 16)
POOL_GROUP = D_MODEL // len(POOL_WINDOWS)
N_HEADS = 8
QK_NOPE = 128
QK_ROPE = 64
V_HEAD = 128
Q_RANK = 768
KV_RANK = 512
ROPE_AXIS = QK_ROPE // 2
ROPE_BASE = 10000.0
SM_SCALE = (QK_NOPE + QK_ROPE) ** -0.5
D_FF = 2816
N_MOD = 6
EPS = 1e-6

SUBLANES = 8
LANES = 128
BF16_SUBLANES = 16
VMEM_LIMIT_BYTES = 56 * 1024 * 1024
ATT_VMEM_LIMIT_BYTES = 58 * 1024 * 1024

ROW_BLOCK = 256
FFN_CHUNK = 512
FFN_TILE = 512
MLA_TILE = 1024
ATT_Q_TILE = 512
ADA_COLS = 2048
ADA_SHARE_COLS = 1024
POOL_HALO = 128
POOL_SEQS = 4
FNET_SEQS = 4
ATT_SEQS = 4
FNET_STEP_BLOCKS = 2
HEAD_LANES = 2 * QK_ROPE
QK_TILE = QK_NOPE + HEAD_LANES


def _cparams(n_axes, vmem_limit_bytes=VMEM_LIMIT_BYTES):
    return pltpu.CompilerParams(dimension_semantics=("arbitrary",) * n_axes,
                                vmem_limit_bytes=vmem_limit_bytes)


def _rms(x):
    return x * lax.rsqrt(jnp.mean(x * x, axis=-1, keepdims=True) + EPS)


def _modulate(x, mod, sub):
    shift = mod[:, (3 * sub) * D_MODEL:(3 * sub + 1) * D_MODEL]
    scale = mod[:, (3 * sub + 1) * D_MODEL:(3 * sub + 2) * D_MODEL]
    return _rms(x) * (1.0 + scale) + shift


def _gate(mod, sub):
    return mod[:, (3 * sub + 2) * D_MODEL:(3 * sub + 3) * D_MODEL]


def _dot(a, b):
    return jnp.dot(a, b, preferred_element_type=F32)


def _dot_nt(a, b):
    return lax.dot_general(a, b, (((1,), (1,)), ((), ())), preferred_element_type=F32)


def _ada_body(cond_ref, w_ref, b_ref, o_ref):
    a = jax.nn.silu(cond_ref[...]).astype(BF16)
    o_ref[...] = _dot(a, w_ref[...].astype(BF16)) + b_ref[...]


def _ada_first_layer(cond, w_ada, b_ada):
    n_out = N_MOD * D_MODEL
    return pl.pallas_call(
        _ada_body,
        grid=(n_out // ADA_COLS,),
        in_specs=[
            pl.BlockSpec((SUBLANES, D_MODEL), lambda j: (0, 0)),
            pl.BlockSpec((None, D_MODEL, ADA_COLS), lambda j: (0, 0, j)),
            pl.BlockSpec((None, 1, ADA_COLS), lambda j: (0, 0, j)),
        ],
        out_specs=pl.BlockSpec((SUBLANES, ADA_COLS), lambda j: (0, j)),
        out_shape=jax.ShapeDtypeStruct((SUBLANES, n_out), F32),
        compiler_params=_cparams(1),
        name="ada_mod",
    )(cond, w_ada, b_ada)


def _ffn_chunks():
    edges = list(range(0, D_FF, FFN_CHUNK)) + [D_FF]
    return list(zip(edges[:-1], edges[1:]))


def _ffn_body(seq_len, n_seg, seg, final, cast_next, ada_later, *refs):
    refs = list(refs)
    x_ref, xp_ref, xn_ref, mod_ref, win_ref, wc_ref, bc_ref, wout_ref = refs[:8]
    del refs[:8]
    gfin_ref = refs.pop(0) if final else None
    side_in = [refs.pop(0) for _ in range(2 * cast_next + 3 * ada_later)]
    o_ref = refs.pop(0)
    side_out = [refs.pop(0) for _ in range(2 * cast_next + ada_later)]
    if cast_next:
        side_out[0][...] = side_in[0][...].astype(BF16)
        side_out[1][...] = side_in[1][...].astype(BF16)
    if ada_later:
        _ada_body(*side_in[-3:], side_out[-1])
    hext_ref, uga_ref, uva_ref, ugb_ref, uvb_ref, acta_ref, actb_ref, acc_ref = refs
    tile = n_seg * seg
    gap = BF16_SUBLANES
    i = pl.program_id(0)
    mod = mod_ref[...]
    x = x_ref[...]

    start = i * tile
    prev_ok = (start % seq_len != 0).astype(F32)
    next_ok = ((start + tile) % seq_len != 0).astype(F32)
    zeros8 = jnp.zeros((SUBLANES, D_MODEL), F32)
    h_prev = _modulate(xp_ref[...], mod, 1) * prev_ok
    h_next = _modulate(xn_ref[...], mod, 1) * next_ok
    hext_ref[0:gap, :] = jnp.concatenate([zeros8, h_prev], axis=0).astype(BF16)
    h = _modulate(x, mod, 1).astype(BF16)
    for k in range(n_seg):
        base = gap + k * (seg + gap)
        hext_ref[base:base + seg, :] = h[k * seg:(k + 1) * seg, :]
        if k == n_seg - 1:
            tail = jnp.concatenate([h_next, zeros8], axis=0).astype(BF16)
        else:
            tail = jnp.zeros((gap, D_MODEL), BF16)
        hext_ref[base + seg:base + seg + gap, :] = tail
    acc_ref[...] = jnp.zeros_like(acc_ref)

    def up(cols, ug_ref, uv_ref):
        lo, hi = cols
        ug_ref[:, :hi - lo] = _dot(hext_ref[...], win_ref[:, lo:hi])
        uv_ref[:, :hi - lo] = _dot(hext_ref[...], win_ref[:, D_FF + lo:D_FF + hi])

    def conv(u_ref, b, lo, hi):
        wc = wc_ref[:, lo:hi]
        return (u_ref[b - 1:b - 1 + seg, :hi - lo] * wc[0:1, :] + u_ref[b:b + seg, :hi - lo] * wc[1:2, :]
                + u_ref[b + 1:b + 1 + seg, :hi - lo] * wc[2:3, :] + bc_ref[:, lo:hi])

    def down(cols, ug_ref, uv_ref, act_ref):
        lo, hi = cols
        for k in range(n_seg):
            b = gap + k * (seg + gap)
            a = jax.nn.silu(conv(ug_ref, b, lo, hi)) * conv(uv_ref, b, D_FF + lo, D_FF + hi)
            act_ref[k * seg:(k + 1) * seg, :hi - lo] = a.astype(BF16)
        acc_ref[...] += _dot(act_ref[:, :hi - lo], wout_ref[lo:hi, :])

    chunks = _ffn_chunks()
    u_bufs = ((uga_ref, uva_ref), (ugb_ref, uvb_ref))
    act_bufs = (acta_ref, actb_ref)
    up(chunks[0], *u_bufs[0])
    for c, cols in enumerate(chunks):
        if c + 1 < len(chunks):
            up(chunks[c + 1], *u_bufs[(c + 1) % 2])
        down(cols, *u_bufs[c % 2], act_bufs[c % 2])
    y = x + _gate(mod, 1) * acc_ref[...]
    if final:
        y = _rms(y) * gfin_ref[...]
    o_ref[...] = y


def _conv_ffn(x, mod, seq_len, layer, w_in, w_conv, b_conv, w_out, g_final=None, cast_next=None,
              ada_later=None):
    n = x.shape[0]
    tile = FFN_TILE
    seg = min(seq_len, tile)
    n_seg = tile // seg
    rows = BF16_SUBLANES + n_seg * (seg + BF16_SUBLANES)
    per_seq_mod = mod.shape[0] > 1
    final = g_final is not None
    blocks8 = n // SUBLANES
    t8 = tile // SUBLANES

    def resident(a):
        return pl.BlockSpec(a.shape, lambda i: (0,) * a.ndim, pipeline_mode=pl.Buffered(1))

    def resident_layer(a):
        shape = a.shape[1:]
        return pl.BlockSpec((None,) + shape, lambda i: (layer,) + (0,) * len(shape),
                            pipeline_mode=pl.Buffered(1))

    in_specs = [
        pl.BlockSpec((tile, D_MODEL), lambda i: (i, 0)),
        pl.BlockSpec((SUBLANES, D_MODEL), lambda i: (jnp.maximum(i * t8 - 1, 0), 0)),
        pl.BlockSpec((SUBLANES, D_MODEL), lambda i: (jnp.minimum((i + 1) * t8, blocks8 - 1), 0)),
        pl.BlockSpec((None, 1, N_MOD * D_MODEL),
                     (lambda i: ((i * tile) // seq_len, 0, 0)) if per_seq_mod else (lambda i: (0, 0, 0))),
        resident(w_in), resident_layer(w_conv), resident_layer(b_conv), resident(w_out),
    ]
    args = [x, x, x, mod, w_in, w_conv, b_conv, w_out]
    if final:
        in_specs.append(pl.BlockSpec((1, D_MODEL), lambda i: (0, 0)))
        args.append(g_final)
    out_specs = [pl.BlockSpec((tile, D_MODEL), lambda i: (i, 0))]
    out_shape = [jax.ShapeDtypeStruct((n, D_MODEL), F32)]
    if cast_next is not None:
        steps = n // tile
        for a in cast_next:
            rows_step = a.shape[1] // steps
            assert rows_step * steps == a.shape[1] and rows_step % BF16_SUBLANES == 0
            in_specs.append(pl.BlockSpec((None, rows_step, a.shape[2]), lambda i: (layer + 1, i, 0)))
            out_specs.append(pl.BlockSpec((rows_step, a.shape[2]), lambda i: (i, 0)))
            out_shape.append(jax.ShapeDtypeStruct(a.shape[1:], BF16))
            args.append(a)
    if ada_later is not None:
        cond, w_ada, b_ada, layers = ada_later
        blocks_per_layer = N_MOD * D_MODEL // ADA_SHARE_COLS
        n_share = len(layers) * blocks_per_layer
        assert n_share <= n // tile and list(layers) == list(range(layers[0], layers[0] + len(layers)))

        def share(i):
            s = jnp.minimum(i, n_share - 1)
            return s // blocks_per_layer, s % blocks_per_layer

        in_specs += [
            pl.BlockSpec((SUBLANES, D_MODEL), lambda i: (0, 0)),
            pl.BlockSpec((None, D_MODEL, ADA_SHARE_COLS), lambda i: (layers[0] + share(i)[0], 0, share(i)[1])),
            pl.BlockSpec((None, 1, ADA_SHARE_COLS), lambda i: (layers[0] + share(i)[0], 0, share(i)[1])),
        ]
        args += [cond, w_ada, b_ada]
        out_specs.append(pl.BlockSpec((None, SUBLANES, ADA_SHARE_COLS), lambda i: (share(i)[0], 0, share(i)[1])))
        out_shape.append(jax.ShapeDtypeStruct((len(layers), SUBLANES, N_MOD * D_MODEL), F32))
    return pl.pallas_call(
        functools.partial(_ffn_body, seq_len, n_seg, seg, final, cast_next is not None, ada_later is not None),
        grid=(n // tile,),
        in_specs=in_specs,
        out_specs=out_specs,
        out_shape=out_shape,
        scratch_shapes=[pltpu.VMEM((rows, D_MODEL), BF16)]
        + [pltpu.VMEM((rows, FFN_CHUNK), F32)] * 4
        + [pltpu.VMEM((tile, FFN_CHUNK), BF16)] * 2
        + [pltpu.VMEM((tile, D_MODEL), F32)],
        compiler_params=_cparams(1),
        name="conv_ffn",
    )(*args)


def _fnet_body(seq_len, n_seq, x_ref, mod_ref, csc_ref, csrow_ref, w_ref, o_ref):
    mod = mod_ref[...]
    x = x_ref[...]
    h = _modulate(x, mod, 0).astype(BF16)
    table = csc_ref[...].astype(BF16)
    parts = [_dot(h[:, g * FOURIER_GROUP:(g + 1) * FOURIER_GROUP], table) for g in range(N_FOURIER_GROUPS)]
    cos_part = jnp.concatenate([p[:, :FOURIER_GROUP] for p in parts], axis=1).astype(BF16)
    sin_part = jnp.concatenate([p[:, FOURIER_GROUP:] for p in parts], axis=1).astype(BF16)
    positions = csrow_ref[...].astype(BF16)
    scale = (seq_len * FOURIER_GROUP) ** -0.5
    f = []
    for s in range(n_seq):
        rows = slice(s * seq_len, (s + 1) * seq_len)
        stack = jnp.concatenate([cos_part[rows], sin_part[rows]], axis=0)
        f.append((_dot(positions, stack) * scale).astype(BF16))
    y = _dot(jnp.concatenate(f, axis=0), w_ref[...])
    o_ref[...] = x + _gate(mod, 0) * y


def _fnet_split_body(seq_len, cast_first, *refs):
    if cast_first:
        (x_ref, mod_ref, csc_ref, even_ref, odd_ref, perm_ref, w_ref, win_ref, wout_ref,
         o_ref, win_bf_ref, wout_bf_ref, plus_ref, minus_ref) = refs
        win_bf_ref[...] = win_ref[...].astype(BF16)
        wout_bf_ref[...] = wout_ref[...].astype(BF16)
    else:
        x_ref, mod_ref, csc_ref, even_ref, odd_ref, perm_ref, w_ref, o_ref, plus_ref, minus_ref = refs
    r = pl.program_id(1)
    mod = mod_ref[...]
    half = seq_len // 2

    @pl.when(r == 0)
    def _():
        def rows(j, carry):
            r0 = pl.multiple_of(j * ROW_BLOCK, ROW_BLOCK)
            h_lo = _modulate(x_ref[pl.ds(r0, ROW_BLOCK), :], mod, 0).astype(BF16)
            h_hi = _modulate(x_ref[pl.ds(half + r0, ROW_BLOCK), :], mod, 0).astype(BF16)
            table = csc_ref[...].astype(BF16)
            for g in range(N_FOURIER_GROUPS):
                cols = slice(g * FOURIER_GROUP, (g + 1) * FOURIER_GROUP)
                y_lo = _dot(h_lo[:, cols], table)
                y_hi = _dot(h_hi[:, cols], table)
                for ref, y in ((plus_ref, y_lo + y_hi), (minus_ref, y_lo - y_hi)):
                    ref[pl.ds(r0, ROW_BLOCK), cols] = y[:, :FOURIER_GROUP].astype(BF16)
                    ref[pl.ds(half + r0, ROW_BLOCK), cols] = y[:, FOURIER_GROUP:].astype(BF16)
            return carry
        lax.fori_loop(0, half // ROW_BLOCK, rows, 0)

    scale = (seq_len * FOURIER_GROUP) ** -0.5
    even = (_dot(even_ref[...].astype(BF16), plus_ref[...]) * scale).astype(BF16)
    odd = (_dot(odd_ref[...].astype(BF16), minus_ref[...]) * scale).astype(BF16)
    half_block = ROW_BLOCK // 2
    f = []
    for j in range(FNET_STEP_BLOCKS):
        rows = slice(j * half_block, (j + 1) * half_block)
        f.append(_dot(perm_ref[...], jnp.concatenate([even[rows], odd[rows]], axis=0)).astype(BF16))
    y = _dot(jnp.concatenate(f, axis=0), w_ref[...])
    step_rows = FNET_STEP_BLOCKS * ROW_BLOCK
    r0 = pl.multiple_of(r * step_rows, step_rows)
    o_ref[...] = x_ref[pl.ds(r0, step_rows), :] + _gate(mod, 0) * y


def _channel_table():
    c = np.arange(FOURIER_GROUP)
    ang = 2.0 * np.pi * ((c[:, None] * c[None, :]) % FOURIER_GROUP) / FOURIER_GROUP
    return jnp.asarray(np.concatenate([np.cos(ang), np.sin(ang)], axis=1), F32)


def _position_table(freqs, positions, seq_len):
    ang = 2.0 * np.pi * ((freqs[:, None] * positions[None, :]) % seq_len) / seq_len
    return jnp.asarray(np.concatenate([np.cos(ang), -np.sin(ang)], axis=1), F32)


def _fourier_mix(x, mod, seq_len, w, cast_first=None):
    n = x.shape[0]
    batch = n // seq_len
    n_row_blocks = seq_len // ROW_BLOCK
    mod_spec = pl.BlockSpec((None, 1, N_MOD * D_MODEL),
                            (lambda b, r: (b, 0, 0)) if mod.shape[0] > 1 else (lambda b, r: (0, 0, 0)))
    common = dict(
        grid=(batch, n_row_blocks),
        out_specs=pl.BlockSpec((ROW_BLOCK, D_MODEL), lambda b, r: (b * n_row_blocks + r, 0)),
        out_shape=jax.ShapeDtypeStruct((n, D_MODEL), F32),
        compiler_params=_cparams(2),
        name="fourier_mix",
    )
    x_spec = pl.BlockSpec((seq_len, D_MODEL), lambda b, r: (b, 0))
    csc_spec = pl.BlockSpec((FOURIER_GROUP, 2 * FOURIER_GROUP), lambda b, r: (0, 0))
    w_spec = pl.BlockSpec((D_MODEL, D_MODEL), lambda b, r: (0, 0))
    t = np.arange(seq_len)
    if n_row_blocks == 1:
        n_seq = FNET_SEQS if (mod.shape[0] == 1 and batch % FNET_SEQS == 0) else 1
        rows = pl.BlockSpec((n_seq * seq_len, D_MODEL), lambda b, r: (b, 0))
        common.update(grid=(batch // n_seq, 1), out_specs=rows)
        return pl.pallas_call(
            functools.partial(_fnet_body, seq_len, n_seq),
            in_specs=[rows, mod_spec, csc_spec, pl.BlockSpec((seq_len, 2 * seq_len), lambda b, r: (0, 0)), w_spec],
            **common,
        )(x, mod, _channel_table(), _position_table(t, t, seq_len), w)
    half_block = ROW_BLOCK // 2
    u = np.arange(seq_len // 2)
    i = np.arange(half_block)
    perm = np.zeros((ROW_BLOCK, ROW_BLOCK), np.float32)
    perm[2 * i, i] = 1.0
    perm[2 * i + 1, half_block + i] = 1.0
    step_rows = FNET_STEP_BLOCKS * ROW_BLOCK
    steps = seq_len // step_rows
    table_spec = pl.BlockSpec((step_rows // 2, seq_len), lambda b, r: (r, 0))
    common.update(grid=(batch, steps),
                  out_specs=pl.BlockSpec((step_rows, D_MODEL), lambda b, r: (b * steps + r, 0)))
    in_specs = [x_spec, mod_spec, csc_spec, table_spec, table_spec,
                pl.BlockSpec((ROW_BLOCK, ROW_BLOCK), lambda b, r: (0, 0)), w_spec]
    args = [x, mod, _channel_table(), _position_table(2 * u, u, seq_len), _position_table(2 * u + 1, u, seq_len),
            jnp.asarray(perm, BF16), w]
    if cast_first is not None:
        out_specs, out_shape = [common["out_specs"]], [common["out_shape"]]
        for a in cast_first:
            rows_step = a.shape[1] // (batch * steps)
            assert rows_step * batch * steps == a.shape[1] and rows_step % BF16_SUBLANES == 0
            in_specs.append(pl.BlockSpec((None, rows_step, a.shape[2]), lambda b, r: (0, b * steps + r, 0)))
            out_specs.append(pl.BlockSpec((rows_step, a.shape[2]), lambda b, r: (b * steps + r, 0)))
            out_shape.append(jax.ShapeDtypeStruct(a.shape[1:], BF16))
            args.append(a)
        common.update(out_specs=out_specs, out_shape=out_shape)
    return pl.pallas_call(
        functools.partial(_fnet_split_body, seq_len, cast_first is not None),
        in_specs=in_specs,
        scratch_shapes=[pltpu.VMEM((seq_len, D_MODEL), BF16)] * 2,
        **common,
    )(*args)


def _pool_bands():
    i = np.arange(ROW_BLOCK)[:, None]
    c = np.arange(2 * ROW_BLOCK)[None, :] - POOL_HALO
    bands = [((c >= i - w // 2) & (c < i - w // 2 + w)) for w in POOL_WINDOWS]
    return jnp.asarray(np.stack(bands), BF16)


def _pool_body(seq_len, n_seq, x_ref, mod_ref, band_ref, w_ref, ps_ref, o_ref, h_ref, hb_ref):
    mod = mod_ref[...]
    n_blocks = seq_len // ROW_BLOCK
    padded = seq_len + 2 * POOL_HALO
    zeros = jnp.zeros((POOL_HALO, D_MODEL), BF16)
    for s in range(n_seq):
        hb_ref[s * padded:s * padded + POOL_HALO, :] = zeros
        hb_ref[(s + 1) * padded - POOL_HALO:(s + 1) * padded, :] = zeros
        for j in range(n_blocks):
            rows = slice(s * seq_len + j * ROW_BLOCK, s * seq_len + (j + 1) * ROW_BLOCK)
            h = _modulate(x_ref[rows, :], mod, 0)
            h_ref[rows, :] = h
            start = s * padded + POOL_HALO + j * ROW_BLOCK
            hb_ref[start:start + ROW_BLOCK, :] = h.astype(BF16)

    gate = _gate(mod, 0)
    for g, w in enumerate(POOL_WINDOWS):
        cols = slice(g * POOL_GROUP, (g + 1) * POOL_GROUP)
        pooled = []
        for s in range(n_seq):
            for j in range(n_blocks):
                rows = slice(s * seq_len + j * ROW_BLOCK, s * seq_len + (j + 1) * ROW_BLOCK)
                slab = slice(s * padded + j * ROW_BLOCK, s * padded + (j + 2) * ROW_BLOCK)
                t = j * ROW_BLOCK + lax.broadcasted_iota(jnp.int32, (ROW_BLOCK, POOL_GROUP), 0)
                total = _dot(band_ref[g], hb_ref[slab, cols])
                cnt = jnp.minimum(t - w // 2 + w, seq_len) - jnp.maximum(t - w // 2, 0)
                pooled.append((total / cnt.astype(F32) - h_ref[rows, cols]).astype(BF16))
        y = _dot(jnp.concatenate(pooled, axis=0), w_ref[g]) * ps_ref[:, cols]
        o_ref[:, cols] = x_ref[:, cols] + gate[:, cols] * y


def _pool_mix(x, mod, seq_len, w_pool, pool_scale):
    n = x.shape[0]
    batch = n // seq_len
    assert POOL_HALO >= max(POOL_WINDOWS) // 2 and 2 * POOL_HALO == ROW_BLOCK
    n_seq = POOL_SEQS if (seq_len == ROW_BLOCK and mod.shape[0] == 1 and batch % POOL_SEQS == 0) else 1
    rows = n_seq * seq_len
    return pl.pallas_call(
        functools.partial(_pool_body, seq_len, n_seq),
        grid=(batch // n_seq,),
        in_specs=[
            pl.BlockSpec((rows, D_MODEL), lambda b: (b, 0)),
            pl.BlockSpec((None, 1, N_MOD * D_MODEL),
                         (lambda b: (b, 0, 0)) if mod.shape[0] > 1 else (lambda b: (0, 0, 0))),
            pl.BlockSpec((len(POOL_WINDOWS), ROW_BLOCK, 2 * ROW_BLOCK), lambda b: (0, 0, 0)),
            pl.BlockSpec((len(POOL_WINDOWS), POOL_GROUP, POOL_GROUP), lambda b: (0, 0, 0)),
            pl.BlockSpec((1, D_MODEL), lambda b: (0, 0)),
        ],
        out_specs=pl.BlockSpec((rows, D_MODEL), lambda b: (b, 0)),
        out_shape=jax.ShapeDtypeStruct((n, D_MODEL), F32),
        scratch_shapes=[pltpu.VMEM((rows, D_MODEL), F32),
                        pltpu.VMEM((n_seq * (seq_len + 2 * POOL_HALO), D_MODEL), BF16)],
        compiler_params=_cparams(1),
        name="pool_mix",
    )(x, mod, _pool_bands(), w_pool, pool_scale)


def _rope_pair(a, cs):
    t = a * cs
    return t + pltpu.roll(t, QK_ROPE, 1)


def _store_keys(k_ref, k_nope, k_rope):
    for hd in range(N_HEADS):
        k_ref[:, hd * QK_TILE:hd * QK_TILE + QK_NOPE] = k_nope[:, hd * QK_NOPE:(hd + 1) * QK_NOPE]
        k_ref[:, hd * QK_TILE + QK_NOPE:(hd + 1) * QK_TILE] = k_rope


def _mla_proj_body(rope, state, *refs):
    refs = list(refs)
    x_ref, mod_ref = refs[:2]
    del refs[:2]
    cs_ref = refs.pop(0) if rope else None
    wdq_ref, gq_ref, wuq_ref, wdkv_ref, gkv_ref, wukv_ref = refs[:6]
    del refs[:6]
    q_ref, k_ref, v_ref = refs[:3]
    del refs[:3]
    h = _modulate(x_ref[...], mod_ref[...], 0).astype(BF16)
    cq = (_rms(_dot(h, wdq_ref[...])) * gq_ref[...]).astype(BF16)
    q = _dot(cq, wuq_ref[...]) * SM_SCALE
    n_nope = N_HEADS * QK_NOPE
    for hd in range(N_HEADS):
        a = q[:, n_nope + hd * HEAD_LANES:n_nope + (hd + 1) * HEAD_LANES]
        if rope:
            a = _rope_pair(a, cs_ref[...])
        q_ref[:, hd * QK_TILE:hd * QK_TILE + QK_NOPE] = q[:, hd * QK_NOPE:(hd + 1) * QK_NOPE].astype(BF16)
        q_ref[:, hd * QK_TILE + QK_NOPE:(hd + 1) * QK_TILE] = a.astype(BF16)
    ckv = _dot(h, wdkv_ref[...])
    c = _rms(ckv[:, :KV_RANK]) * gkv_ref[...]
    kp = ckv[:, KV_RANK:]
    if state:
        ckv_out_ref, kpe_out_ref = refs
        ckv_out_ref[...] = c
        kpe_out_ref[...] = kp[:, :QK_ROPE]
    if rope:
        kp = _rope_pair(kp, cs_ref[...])
    lane = lax.broadcasted_iota(jnp.int32, kp.shape, 1)
    kr = jnp.where(lane < QK_ROPE, kp, 0.0).astype(BF16)
    kv = _dot(c.astype(BF16), wukv_ref[...])
    _store_keys(k_ref, kv[:, :n_nope].astype(BF16), kr)
    v_ref[...] = kv[:, n_nope:].astype(BF16)


def _mla_project(x, mod, seq_len, prm, cs=None, state=False):
    n = x.shape[0]
    rope = cs is not None
    tile = min(MLA_TILE, seq_len) if (rope or mod.shape[0] > 1) else MLA_TILE
    tiles_per_seq = max(seq_len // tile, 1)
    wide = N_HEADS * QK_NOPE

    def const(shape):
        return pl.BlockSpec(shape, lambda i: (0,) * len(shape), pipeline_mode=pl.Buffered(1))

    def rows(width):
        return pl.BlockSpec((tile, width), lambda i: (i, 0))

    in_specs = [rows(D_MODEL),
                pl.BlockSpec((None, 1, N_MOD * D_MODEL),
                             (lambda i: (i // tiles_per_seq, 0, 0)) if mod.shape[0] > 1
                             else (lambda i: (0, 0, 0)))]
    args = [x, mod]
    if rope:
        in_specs.append(pl.BlockSpec((tile, HEAD_LANES), lambda i: (i % tiles_per_seq, 0)))
        args.append(cs)
    for name in ("w_dq", "g_q", "w_uq", "w_dkv", "g_kv", "w_ukv"):
        in_specs.append(const(prm[name].shape))
        args.append(prm[name])
    out_specs = [rows(N_HEADS * QK_TILE), rows(N_HEADS * QK_TILE), rows(wide)]
    out_shape = [jax.ShapeDtypeStruct((n, N_HEADS * QK_TILE), BF16),
                 jax.ShapeDtypeStruct((n, N_HEADS * QK_TILE), BF16),
                 jax.ShapeDtypeStruct((n, wide), BF16)]
    if state:
        out_specs += [rows(KV_RANK), rows(QK_ROPE)]
        out_shape += [jax.ShapeDtypeStruct((n, KV_RANK), F32), jax.ShapeDtypeStruct((n, QK_ROPE), F32)]
    return pl.pallas_call(
        functools.partial(_mla_proj_body, rope, state),
        grid=(n // tile,),
        in_specs=in_specs,
        out_specs=out_specs,
        out_shape=out_shape,
        compiler_params=_cparams(1),
        name="mla_project",
    )(*args)


def _ctx_expand_body(c_ref, kr_ref, w_ref, k_ref, v_ref):
    kv = _dot(c_ref[...].astype(BF16), w_ref[...])
    n_nope = N_HEADS * QK_NOPE
    _store_keys(k_ref, kv[:, :n_nope].astype(BF16), kr_ref[...])
    v_ref[...] = kv[:, n_nope:].astype(BF16)


def _ctx_expand(c_kv, k_rope, w_ukv):
    n = c_kv.shape[0]
    wide = N_HEADS * QK_NOPE
    return pl.pallas_call(
        _ctx_expand_body,
        grid=(n // ROW_BLOCK,),
        in_specs=[pl.BlockSpec((ROW_BLOCK, KV_RANK), lambda i: (i, 0)),
                  pl.BlockSpec((ROW_BLOCK, HEAD_LANES), lambda i: (i, 0)),
                  pl.BlockSpec(w_ukv.shape, lambda i: (0, 0))],
        out_specs=[pl.BlockSpec((ROW_BLOCK, N_HEADS * QK_TILE), lambda i: (i, 0)),
                   pl.BlockSpec((ROW_BLOCK, wide), lambda i: (i, 0))],
        out_shape=[jax.ShapeDtypeStruct((n, N_HEADS * QK_TILE), BF16), jax.ShapeDtypeStruct((n, wide), BF16)],
        compiler_params=_cparams(1),
        name="mla_ctx_expand",
    )(c_kv, k_rope, w_ukv)


def _attn_body(has_ctx, n_seq, *refs):
    refs = list(refs)
    q_ref, k_ref, v_ref = refs[:3]
    del refs[:3]
    if has_ctx:
        kc_ref, vc_ref = refs[:2]
        del refs[:2]
    x_ref, mod_ref, wo_ref, o_ref, heads_ref = refs
    n_q = q_ref.shape[0] // n_seq
    n_k = k_ref.shape[0] // n_seq
    for seq in range(n_seq):
        rows = slice(seq * n_q, (seq + 1) * n_q)
        keys = slice(seq * n_k, (seq + 1) * n_k)
        for hd in range(N_HEADS):
            qk = slice(hd * QK_TILE, (hd + 1) * QK_TILE)
            vcols = slice(hd * V_HEAD, (hd + 1) * V_HEAD)
            q = q_ref[rows, qk]
            s = _dot_nt(q, k_ref[keys, qk])
            m = jnp.max(s, axis=-1, keepdims=True)
            if has_ctx:
                sc = _dot_nt(q, kc_ref[:, qk])
                m = jnp.maximum(m, jnp.max(sc, axis=-1, keepdims=True))
            p = jnp.exp(s - m)
            l = jnp.sum(p, axis=-1, keepdims=True)
            o = _dot(p.astype(BF16), v_ref[keys, vcols])
            if has_ctx:
                pc = jnp.exp(sc - m)
                l = l + jnp.sum(pc, axis=-1, keepdims=True)
                o = o + _dot(pc.astype(BF16), vc_ref[:, vcols])
            heads_ref[rows, vcols] = (o / l).astype(BF16)
    y = _dot(heads_ref[...], wo_ref[...])
    o_ref[...] = x_ref[...] + _gate(mod_ref[...], 0) * y


def _mla_context_body(n_seq, x_ref, mod_ref, wdq_ref, gq_ref, wuq_ref, wdkv_ref, gkv_ref, wukv_ref, wo_ref,
                      o_ref, ckv_ref, kpe_ref, q_ref, k_ref, v_ref, heads_ref):
    _mla_proj_body(False, True, x_ref, mod_ref, wdq_ref, gq_ref, wuq_ref, wdkv_ref, gkv_ref, wukv_ref,
                   q_ref, k_ref, v_ref, ckv_ref, kpe_ref)
    _attn_body(False, n_seq, q_ref, k_ref, v_ref, x_ref, mod_ref, wo_ref, o_ref, heads_ref)


def _mla_context(x, mod, seq_len, prm):
    n = x.shape[0]
    n_seq = ATT_SEQS
    rows = n_seq * seq_len
    assert mod.shape[0] == 1 and n % rows == 0
    wide = N_HEADS * QK_NOPE

    def const(a):
        return pl.BlockSpec(a.shape, lambda i: (0,) * a.ndim, pipeline_mode=pl.Buffered(1))

    def tokens(width):
        return pl.BlockSpec((rows, width), lambda i: (i, 0))

    weights = [prm[name] for name in ("w_dq", "g_q", "w_uq", "w_dkv", "g_kv", "w_ukv", "w_o")]
    return pl.pallas_call(
        functools.partial(_mla_context_body, n_seq),
        grid=(n // rows,),
        in_specs=[tokens(D_MODEL), pl.BlockSpec((None, 1, N_MOD * D_MODEL), lambda i: (0, 0, 0))]
        + [const(w) for w in weights],
        out_specs=[tokens(D_MODEL), tokens(KV_RANK), tokens(QK_ROPE)],
        out_shape=[jax.ShapeDtypeStruct((n, D_MODEL), F32), jax.ShapeDtypeStruct((n, KV_RANK), F32),
                   jax.ShapeDtypeStruct((n, QK_ROPE), F32)],
        scratch_shapes=[pltpu.VMEM((rows, N_HEADS * QK_TILE), BF16), pltpu.VMEM((rows, N_HEADS * QK_TILE), BF16),
                        pltpu.VMEM((rows, wide), BF16), pltpu.VMEM((rows, wide), BF16)],
        compiler_params=_cparams(1, ATT_VMEM_LIMIT_BYTES),
        name="mla_context",
    )(x, mod, *weights)


def _mla_attend(x, mod, seq_len, q, k, v, w_o, ctx=None):
    n = x.shape[0]
    tq = min(ATT_Q_TILE, seq_len)
    q_tiles = seq_len // tq
    wide = N_HEADS * QK_NOPE
    has_ctx = ctx is not None
    batch = n // seq_len
    grouped = q_tiles == 1 and not has_ctx and mod.shape[0] == 1 and batch % ATT_SEQS == 0
    n_seq = ATT_SEQS if grouped else 1

    def q_rows(width):
        return pl.BlockSpec((n_seq * tq, width), lambda b, r: (b * q_tiles + r, 0))

    def seq_rows(length, width):
        mode = pl.Buffered(1) if q_tiles > 1 else None
        return pl.BlockSpec((n_seq * length, width), lambda b, r: (b, 0), pipeline_mode=mode)

    in_specs = [q_rows(N_HEADS * QK_TILE), seq_rows(seq_len, N_HEADS * QK_TILE), seq_rows(seq_len, wide)]
    args = [q, k, v]
    if has_ctx:
        kc, vc = ctx
        past = kc.shape[0] // (n // seq_len)
        in_specs += [seq_rows(past, N_HEADS * QK_TILE), seq_rows(past, wide)]
        args += [kc, vc]
    in_specs += [q_rows(D_MODEL),
                 pl.BlockSpec((None, 1, N_MOD * D_MODEL),
                              (lambda b, r: (b, 0, 0)) if mod.shape[0] > 1 else (lambda b, r: (0, 0, 0))),
                 pl.BlockSpec(w_o.shape, lambda b, r: (0, 0))]
    args += [x, mod, w_o]
    return pl.pallas_call(
        functools.partial(_attn_body, has_ctx, n_seq),
        grid=(batch // n_seq, q_tiles),
        in_specs=in_specs,
        out_specs=q_rows(D_MODEL),
        out_shape=jax.ShapeDtypeStruct((n, D_MODEL), F32),
        scratch_shapes=[pltpu.VMEM((n_seq * tq, wide), BF16)],
        compiler_params=_cparams(2, ATT_VMEM_LIMIT_BYTES),
        name="mla_attend",
    )(*args)


def _rope_table(n_tok):
    rows = n_tok // GRID_W
    row = np.repeat(np.arange(rows), GRID_W).astype(np.float32)
    col = np.tile(np.arange(GRID_W), rows).astype(np.float32)
    inv = (1.0 / (ROPE_BASE ** (np.arange(0, ROPE_AXIS, 2, dtype=np.float32) / ROPE_AXIS))).astype(np.float32)
    ang_r = row[:, None] * inv
    ang_c = col[:, None] * inv
    cos = np.concatenate([np.cos(ang_r), np.cos(ang_r), np.cos(ang_c), np.cos(ang_c)], axis=1)
    sin = np.concatenate([np.sin(ang_r), np.sin(ang_r), np.sin(ang_c), np.sin(ang_c)], axis=1)
    return jnp.asarray(np.concatenate([cos, sin], axis=1), F32)


def _swap_cols(w):
    half = ROPE_AXIS // 2
    return jnp.concatenate([-w[..., half:2 * half], w[..., 0:half],
                            -w[..., 3 * half:4 * half], w[..., 2 * half:3 * half]], axis=-1)


def _mla_weights(w_dq, g_q, w_uq, w_dkv, g_kv, w_ukv, w_o):
    wq = w_uq.reshape(Q_RANK, N_HEADS, QK_NOPE + QK_ROPE)
    wq_pe = wq[:, :, QK_NOPE:]
    wq_pairs = jnp.concatenate([wq_pe, _swap_cols(wq_pe)], axis=-1)
    w_uq2 = jnp.concatenate([wq[:, :, :QK_NOPE].reshape(Q_RANK, -1), wq_pairs.reshape(Q_RANK, -1)], axis=1)
    w_pe = w_dkv[:, KV_RANK:]
    w_dkv2 = jnp.concatenate([w_dkv[:, :KV_RANK], w_pe, _swap_cols(w_pe)], axis=1)
    wkv = w_ukv.reshape(KV_RANK, N_HEADS, QK_NOPE + V_HEAD)
    w_ukv2 = jnp.concatenate([wkv[:, :, :QK_NOPE].reshape(KV_RANK, -1),
                              wkv[:, :, QK_NOPE:].reshape(KV_RANK, -1)], axis=1)
    return {"w_dq": w_dq.astype(BF16), "g_q": g_q.reshape(1, Q_RANK), "w_uq": w_uq2.astype(BF16),
            "w_dkv": w_dkv2.astype(BF16), "g_kv": g_kv.reshape(1, KV_RANK), "w_ukv": w_ukv2.astype(BF16),
            "w_o": w_o.astype(BF16)}


def kernel(x_prompt, x_sample, cache_ckv, cache_kpe, c, c_ctx, w_ada, b_ada, w_fnet, w_pool, pool_scale,
           w_dq, g_q, w_uq, w_dkv, g_kv, w_ukv, w_o, w_ffn_in, w_ffn_conv, b_ffn_conv, w_ffn_out, g_final):
    batch, seq, _ = x_prompt.shape
    dec_batch, dec_seq, _ = x_sample.shape
    past = cache_ckv.shape[2]

    cond = jnp.concatenate([c_ctx[None, :], c, jnp.zeros((SUBLANES - 1 - dec_batch, D_MODEL), F32)], axis=0)
    b_ada = b_ada.reshape(DEPTH, 1, N_MOD * D_MODEL)
    mods = {0: _ada_first_layer(cond, w_ada, b_ada)}
    ada_layers_per_call = (batch * seq // FFN_TILE) // (N_MOD * D_MODEL // ADA_SHARE_COLS)
    b_conv = b_ffn_conv[:, None, :]

    xp = x_prompt.reshape(batch * seq, D_MODEL)
    xs = x_sample.reshape(dec_batch * dec_seq, D_MODEL)
    states = []
    j_f = j_p = j_a = 0
    for i in range(DEPTH):
        mod_p = mods[i][0:1].reshape(1, 1, N_MOD * D_MODEL)
        mod_s = mods[i][1:1 + dec_batch].reshape(dec_batch, 1, N_MOD * D_MODEL)
        kind = i % N_MIXERS
        if kind == 0:
            w = w_fnet[j_f].astype(BF16)
            xp = _fourier_mix(xp, mod_p, seq, w)
            if i == 0:
                xs, w_in_bf, w_out_bf = _fourier_mix(xs, mod_s, dec_seq, w, cast_first=(w_ffn_in, w_ffn_out))
            else:
                xs = _fourier_mix(xs, mod_s, dec_seq, w)
            j_f += 1
        elif kind == 1:
            w = w_pool[j_p].astype(BF16)
            ps = pool_scale[j_p].reshape(1, D_MODEL)
            xp = _pool_mix(xp, mod_p, seq, w, ps)
            xs = _pool_mix(xs, mod_s, dec_seq, w, ps)
            j_p += 1
        else:
            prm = _mla_weights(w_dq[j_a], g_q[j_a], w_uq[j_a], w_dkv[j_a], g_kv[j_a], w_ukv[j_a], w_o[j_a])
            xp, ckv_p, kpe_p = _mla_context(xp, mod_p, seq, prm)
            states.append((ckv_p.reshape(batch, seq, KV_RANK), kpe_p.reshape(batch, seq, QK_ROPE)))
            q, k, v = _mla_project(xs, mod_s, dec_seq, prm, cs=_rope_table(dec_seq))
            kpe_c = cache_kpe[:, j_a].reshape(dec_batch * past, QK_ROPE)
            krc = jnp.concatenate([kpe_c, jnp.zeros_like(kpe_c)], axis=1).astype(BF16)
            kc, vc = _ctx_expand(cache_ckv[:, j_a].reshape(dec_batch * past, KV_RANK), krc, prm["w_ukv"])
            xs = _mla_attend(xs, mod_s, dec_seq, q, k, v, prm["w_o"], ctx=(kc, vc))
            j_a += 1
        g_fin = g_final.reshape(1, D_MODEL) if i == DEPTH - 1 else None
        ffn_w = (w_in_bf, w_ffn_conv, b_conv, w_out_bf)
        if i + 1 < DEPTH:
            todo = [l for l in range(i + 1, DEPTH) if l not in mods][:ada_layers_per_call]
            assert i + 1 in mods or (todo and todo[0] == i + 1)
            ada_later = (cond, w_ada, b_ada, tuple(todo)) if todo else None
            out = _conv_ffn(xp, mod_p, seq, i, *ffn_w, cast_next=(w_ffn_in, w_ffn_out), ada_later=ada_later)
            xp, w_in_bf, w_out_bf = out[:3]
            for n_done, l in enumerate(todo):
                mods[l] = out[3][n_done]
        else:
            xp, = _conv_ffn(xp, mod_p, seq, i, *ffn_w, g_final=g_fin)
        xs, = _conv_ffn(xs, mod_s, dec_seq, i, *ffn_w, g_final=g_fin)

    state_ckv = jnp.stack([s[0] for s in states], axis=1)
    state_kpe = jnp.stack([s[1] for s in states], axis=1)
    return (xp.reshape(batch, seq, D_MODEL), xs.reshape(dec_batch, dec_seq, D_MODEL), state_ckv, state_kpe)
```
